```python
import math
import jax, jax.numpy as jnp
from jax import lax
import numpy as np

D_MODEL = 1024
BATCH = 8
SEQ = 2048
DEPTH = 1
DEC_BATCH = 128
DEC_SEQ = 1
PAST_LEN = 16384
PAGE_SIZE = 128

DN_HEADS = 8
DN_DK = 128
DN_DV = 128
DN_QK_W = DN_HEADS * DN_DK
DN_V_W = DN_HEADS * DN_DV
CONV_W = 4
CONV_DIM = 2 * DN_QK_W + DN_V_W
GDN_CHUNK = 64
SWA_Q_HEADS = 16
SWA_KV_HEADS = 2
SWA_HD = 64
SWA_GROUP = SWA_Q_HEADS // SWA_KV_HEADS
SWA_Q_W = SWA_Q_HEADS * SWA_HD
SWA_KV_W = SWA_KV_HEADS * SWA_HD
WINDOW = 128
SWA_BLOCK = 128
D_FF = 2816
HALF_STEP = 0.5
N_MOD = 9
EPS = 1e-6
MASK_VALUE = -1e30

IN_SPLITS = (CONV_DIM, DN_V_W, DN_HEADS, DN_HEADS, SWA_Q_W, SWA_KV_W, SWA_KV_W, D_MODEL, D_MODEL)
IN_W = sum(IN_SPLITS)
IN_OFFSETS = [int(o) for o in np.cumsum(IN_SPLITS)[:-1]]

kernel_name = "hybrid_gdn_swa_sink_macaron_adaln_step"


def rms_norm(x, w):
    xf = x.astype(jnp.float32)
    y = xf * lax.rsqrt(jnp.mean(xf * xf, axis=-1, keepdims=True) + EPS)
    return (y * w.astype(jnp.float32)).astype(x.dtype)


def l2_normalize(x):
    return x * lax.rsqrt(jnp.sum(x * x, axis=-1, keepdims=True) + EPS)


def swiglu(h, w13, w2):
    gate, up = jnp.split(h @ w13, 2, axis=-1)
    return (jax.nn.silu(gate) * up) @ w2


def causal_dwconv(x_ext, w):
    return lax.conv_general_dilated(x_ext, w[:, None, :].astype(x_ext.dtype), window_strides=(1,),
                                    padding='VALID', dimension_numbers=('NWC', 'WIO', 'NWC'),
                                    feature_group_count=x_ext.shape[-1])


def sink_softmax(s, sink):
    m = jnp.maximum(jnp.max(s, axis=-1, keepdims=True), sink)
    p = jnp.exp(s - m)
    return p / (jnp.sum(p, axis=-1, keepdims=True) + jnp.exp(sink - m))


def gdn_chunked(q, k, v, g, beta):
    b, t, h, _ = q.shape
    n = t // GDN_CHUNK

    def blocks(x):
        x = jnp.moveaxis(x, 2, 1)
        return x.reshape((b, h, n, GDN_CHUNK) + x.shape[3:])

    q, k, v, g, beta = blocks(q), blocks(k), blocks(v), blocks(g), blocks(beta)
    gc = jnp.cumsum(g, axis=-1)
    lower = jnp.tril(jnp.ones((GDN_CHUNK, GDN_CHUNK), bool))
    strict = jnp.tril(jnp.ones((GDN_CHUNK, GDN_CHUNK), bool), -1)
    decay = jnp.where(lower, jnp.exp(jnp.where(lower, gc[..., :, None] - gc[..., None, :], 0.0)), 0.0)
    kb = k * beta[..., None]
    a_mat = jnp.where(strict, jnp.einsum('bhncd,bhnsd->bhncs', kb, k) * decay, 0.0) \
        + jnp.eye(GDN_CHUNK, dtype=jnp.float32)
    rhs = jnp.concatenate([v * beta[..., None], kb * jnp.exp(gc)[..., None]], axis=-1)
    sol = lax.linalg.triangular_solve(a_mat, rhs, left_side=True, lower=True, unit_diagonal=True)
    u, w = sol[..., :DN_DV], sol[..., DN_DV:]
    qk = jnp.einsum('bhncd,bhnsd->bhncs', q, k) * decay
    q_dec = q * jnp.exp(gc)[..., None]
    k_dec = k * jnp.exp(gc[..., -1:] - gc)[..., None]
    g_last = jnp.exp(gc[..., -1])

    def step(S, inp):
        qd_i, kd_i, u_i, w_i, qk_i, gl_i = inp
        v_new = u_i - jnp.einsum('bhcd,bhde->bhce', w_i, S)
        o = jnp.einsum('bhcd,bhde->bhce', qd_i, S) + jnp.einsum('bhcs,bhse->bhce', qk_i, v_new)
        S = S * gl_i[..., None, None] + jnp.einsum('bhcd,bhce->bhde', kd_i, v_new)
        return S, o

    xs = tuple(jnp.moveaxis(a, 2, 0) for a in (q_dec, k_dec, u, w, qk, g_last))
    S0 = jnp.zeros((b, h, DN_DK, DN_DV), jnp.float32)
    S, o = lax.scan(step, S0, xs)
    o = jnp.moveaxis(o, 0, 2).reshape(b, h, t, DN_DV)
    return jnp.moveaxis(o, 1, 2), S


def gdn_recurrent(q, k, v, g, beta, S0):
    def step(S, inp):
        q_t, k_t, v_t, g_t, b_t = inp
        S = S * jnp.exp(g_t)[..., None, None]
        kv = jnp.einsum('bhd,bhde->bhe', k_t, S)
        S = S + jnp.einsum('bhd,bhe->bhde', k_t, (v_t - kv) * b_t[..., None])
        return S, jnp.einsum('bhd,bhde->bhe', q_t, S)

    xs = tuple(jnp.swapaxes(a, 0, 1) for a in (q, k, v, g, beta))
    S, o = lax.scan(step, S0, xs)
    return jnp.swapaxes(o, 0, 1), S


def swa_prompt(q, k, v, sinks):
    b, t = q.shape[:2]
    nb = t // SWA_BLOCK
    qb = q.reshape(b, nb, SWA_BLOCK, SWA_KV_HEADS, SWA_GROUP, SWA_HD)

    def with_prev(x):
        x = x.reshape(b, nb, SWA_BLOCK, SWA_KV_HEADS, SWA_HD)
        prev = jnp.pad(x, ((0, 0), (1, 0), (0, 0), (0, 0), (0, 0)))[:, :-1]
        return jnp.concatenate([prev, x], axis=2)

    kc, vc = with_prev(k), with_prev(v)
    s = jnp.einsum('bnqkgd,bnskd->bnkgqs', qb, kc, preferred_element_type=jnp.float32) * (SWA_HD ** -0.5)
    start = jnp.arange(nb)[:, None, None] * SWA_BLOCK
    qpos = start + jnp.arange(SWA_BLOCK)[None, :, None]
    kpos = start - SWA_BLOCK + jnp.arange(2 * SWA_BLOCK)[None, None, :]
    dist = qpos - kpos
    mask = (kpos >= 0) & (dist >= 0) & (dist < WINDOW)
    s = jnp.where(mask[None, :, None, None], s, MASK_VALUE)
    sink = sinks.astype(jnp.float32).reshape(SWA_KV_HEADS, SWA_GROUP)[:, :, None, None]
    p = sink_softmax(s, sink)
    o = jnp.einsum('bnkgqs,bnskd->bnqkgd', p.astype(vc.dtype), vc)
    return o.reshape(b, t, SWA_Q_W)


def swa_sample(q, k, v, k_buf, v_buf, sinks):
    b, t = q.shape[:2]
    L = k_buf.shape[1]
    kc = jnp.concatenate([k_buf.astype(k.dtype), k], axis=1)
    vc = jnp.concatenate([v_buf.astype(v.dtype), v], axis=1)
    qb = q.reshape(b, t, SWA_KV_HEADS, SWA_GROUP, SWA_HD)
    s = jnp.einsum('bqkgd,bskd->bkgqs', qb, kc, preferred_element_type=jnp.float32) * (SWA_HD ** -0.5)
    dist = (jnp.arange(t)[:, None] + L) - jnp.arange(L + t)[None, :]
    mask = (dist >= 0) & (dist < WINDOW)
    s = jnp.where(mask, s, MASK_VALUE)
    sink = sinks.astype(jnp.float32).reshape(SWA_KV_HEADS, SWA_GROUP)[:, :, None, None]
    p = sink_softmax(s, sink)
    o = jnp.einsum('bkgqs,bskd->bqkgd', p.astype(vc.dtype), vc)
    return o.reshape(b, t, SWA_Q_W), kc[:, -L:], vc[:, -L:]


def decoder_layer(x, c, lp, past):
    b, t, _ = x.shape
    f32 = jnp.float32
    mod = jax.nn.silu(c) @ lp['w_ada'] + lp['b_ada']
    sh1, sc1, g1, sh2, sc2, g2, sh3, sc3, g3 = [m[:, None, :] for m in jnp.split(mod, N_MOD, axis=-1)]

    h = rms_norm(x, lp['norm_ffn1']) * (1 + sc1) + sh1
    x = x + HALF_STEP * g1 * swiglu(h, lp['w13_ffn1'], lp['w2_ffn1'])

    h = rms_norm(x, lp['norm_mix']) * (1 + sc2) + sh2
    u = h @ lp['w_in']
    qkv, z_dn, a_dn, b_dn, q_sw, k_sw, v_sw, gate_dn, gate_sw = jnp.split(u, IN_OFFSETS, axis=-1)

    if past is None:
        x_ext = jnp.pad(qkv, ((0, 0), (CONV_W - 1, 0), (0, 0)))
    else:
        x_ext = jnp.concatenate([past[1].astype(qkv.dtype), qkv], axis=1)
    conv_new = x_ext[:, -(CONV_W - 1):]
    qkv_c = jax.nn.silu(causal_dwconv(x_ext, lp['conv_w'])).astype(f32)
    qd, kd, vd = jnp.split(qkv_c, [DN_QK_W, 2 * DN_QK_W], axis=-1)
    qd = l2_normalize(qd.reshape(b, t, DN_HEADS, DN_DK)) * (DN_DK ** -0.5)
    kd = l2_normalize(kd.reshape(b, t, DN_HEADS, DN_DK))
    vd = vd.reshape(b, t, DN_HEADS, DN_DV)
    g_log = -jnp.exp(lp['a_log'].astype(f32)) * jax.nn.softplus(a_dn.astype(f32) + lp['dt_bias'].astype(f32))
    beta = jax.nn.sigmoid(b_dn.astype(f32))
    if past is None:
        o_dn, S_new = gdn_chunked(qd, kd, vd, g_log, beta)
    else:
        o_dn, S_new = gdn_recurrent(qd, kd, vd, g_log, beta, past[0].astype(f32))
    o_dn = o_dn * lax.rsqrt(jnp.mean(o_dn * o_dn, axis=-1, keepdims=True) + EPS) * lp['dn_norm'].astype(f32)
    o_dn = o_dn * jax.nn.silu(z_dn.astype(f32).reshape(b, t, DN_HEADS, DN_DV))
    y_dn = o_dn.reshape(b, t, DN_V_W).astype(x.dtype)

    q_sw = q_sw.reshape(b, t, SWA_Q_HEADS, SWA_HD)
    k_sw = k_sw.reshape(b, t, SWA_KV_HEADS, SWA_HD)
    v_sw = v_sw.reshape(b, t, SWA_KV_HEADS, SWA_HD)
    if past is None:
        y_sw = swa_prompt(q_sw, k_sw, v_sw, lp['sinks'])
        keep = min(WINDOW, t)
        k_buf, v_buf = k_sw[:, -keep:], v_sw[:, -keep:]
    else:
        y_sw, k_buf, v_buf = swa_sample(q_sw, k_sw, v_sw, past[2], past[3], lp['sinks'])

    merged = jax.nn.sigmoid(gate_dn) * (y_dn @ lp['w_br_dn']) + jax.nn.sigmoid(gate_sw) * (y_sw @ lp['w_br_swa'])
    x = x + g2 * (merged @ lp['w_out'])

    h = rms_norm(x, lp['norm_ffn2']) * (1 + sc3) + sh3
    x = x + HALF_STEP * g3 * swiglu(h, lp['w13_ffn2'], lp['w2_ffn2'])
    return x, (S_new.astype(x.dtype), conv_new, k_buf, v_buf)


def setup_inputs(seed: int = 0) -> dict:
    key = jax.random.key(seed)
    ks = jax.random.split(key, 32)
    f32 = jnp.float32
    L = min(WINDOW, PAST_LEN)

    def nrm(k, shape, scale):
        return scale * jax.random.normal(k, shape, f32)

    def gain(k, shape):
        return 1.0 + 0.01 * jax.random.normal(k, shape, f32)

    dt = jnp.exp(jax.random.uniform(ks[20], (DEPTH, DN_HEADS), f32, math.log(1e-3), math.log(1e-1)))
    return {
        'x_prompt': nrm(ks[0], (BATCH, SEQ, D_MODEL), 1.0),
        'x_sample': nrm(ks[1], (DEC_BATCH, DEC_SEQ, D_MODEL), 1.0),
        'state_dn': nrm(ks[2], (DEPTH, DEC_BATCH, DN_HEADS, DN_DK, DN_DV), 0.1),
        'state_conv': nrm(ks[3], (DEPTH, DEC_BATCH, CONV_W - 1, CONV_DIM), 1.0),
        'cache_swa_k': nrm(ks[4], (DEPTH, DEC_BATCH, L, SWA_KV_HEADS, SWA_HD), 1.0),
        'cache_swa_v': nrm(ks[5], (DEPTH, DEC_BATCH, L, SWA_KV_HEADS, SWA_HD), 1.0),
        'c_prompt': nrm(ks[6], (BATCH, D_MODEL), 1.0),
        'c_sample': nrm(ks[7], (DEC_BATCH, D_MODEL), 1.0),
        'w_ada': nrm(ks[8], (DEPTH, D_MODEL, N_MOD * D_MODEL), 0.5 * D_MODEL ** -0.5),
        'b_ada': nrm(ks[9], (DEPTH, N_MOD * D_MODEL), 0.01),
        'norm_ffn1': gain(ks[10], (DEPTH, D_MODEL)),
        'w13_ffn1': nrm(ks[11], (DEPTH, D_MODEL, 2 * D_FF), D_MODEL ** -0.5),
        'w2_ffn1': nrm(ks[12], (DEPTH, D_FF, D_MODEL), D_FF ** -0.5),
        'norm_mix': gain(ks[13], (DEPTH, D_MODEL)),
        'w_in': nrm(ks[14], (DEPTH, D_MODEL, IN_W), D_MODEL ** -0.5),
        'conv_w': nrm(ks[15], (DEPTH, CONV_W, CONV_DIM), CONV_W ** -0.5),
        'a_log': jnp.log(jax.random.uniform(ks[16], (DEPTH, DN_HEADS), f32, 1.0, 16.0)),
        'dt_bias': dt + jnp.log(-jnp.expm1(-dt)),
        'dn_norm': gain(ks[17], (DEPTH, DN_DV)),
        'sinks': nrm(ks[18], (DEPTH, SWA_Q_HEADS), 0.5),
        'w_br_dn': nrm(ks[19], (DEPTH, DN_V_W, D_MODEL), DN_V_W ** -0.5),
        'w_br_swa': nrm(ks[21], (DEPTH, SWA_Q_W, D_MODEL), SWA_Q_W ** -0.5),
        'w_out': nrm(ks[22], (DEPTH, D_MODEL, D_MODEL), D_MODEL ** -0.5),
        'norm_ffn2': gain(ks[23], (DEPTH, D_MODEL)),
        'w13_ffn2': nrm(ks[24], (DEPTH, D_MODEL, 2 * D_FF), D_MODEL ** -0.5),
        'w2_ffn2': nrm(ks[25], (DEPTH, D_FF, D_MODEL), D_FF ** -0.5),
        'final_norm': gain(ks[26], (D_MODEL,)),
    }


def reference(x_prompt, x_sample, state_dn, state_conv, cache_swa_k, cache_swa_v, c_prompt, c_sample,
              w_ada, b_ada, norm_ffn1, w13_ffn1, w2_ffn1, norm_mix, w_in, conv_w, a_log, dt_bias,
              dn_norm, sinks, w_br_dn, w_br_swa, w_out, norm_ffn2, w13_ffn2, w2_ffn2, final_norm):
    y_p, y_s = x_prompt, x_sample
    st_p, st_s = [], []
    for l in range(DEPTH):
        lp = dict(w_ada=w_ada[l], b_ada=b_ada[l], norm_ffn1=norm_ffn1[l], w13_ffn1=w13_ffn1[l],
                  w2_ffn1=w2_ffn1[l], norm_mix=norm_mix[l], w_in=w_in[l], conv_w=conv_w[l],
                  a_log=a_log[l], dt_bias=dt_bias[l], dn_norm=dn_norm[l], sinks=sinks[l],
                  w_br_dn=w_br_dn[l], w_br_swa=w_br_swa[l], w_out=w_out[l], norm_ffn2=norm_ffn2[l],
                  w13_ffn2=w13_ffn2[l], w2_ffn2=w2_ffn2[l])
        y_p, sp = decoder_layer(y_p, c_prompt, lp, None)
        y_s, ss = decoder_layer(y_s, c_sample, lp, (state_dn[l], state_conv[l], cache_swa_k[l], cache_swa_v[l]))
        st_p.append(sp)
        st_s.append(ss)
    y_p = rms_norm(y_p, final_norm)
    y_s = rms_norm(y_s, final_norm)
    dn_p = jnp.stack([s[0] for s in st_p])
    dn_s = jnp.stack([s[0] for s in st_s])
    conv_p = jnp.stack([s[1] for s in st_p])
    conv_s = jnp.stack([s[1] for s in st_s])
    k_p = jnp.stack([s[2] for s in st_p])
    k_s = jnp.stack([s[2] for s in st_s])
    v_p = jnp.stack([s[3] for s in st_p])
    v_s = jnp.stack([s[3] for s in st_s])
    return (y_p, y_s, dn_p, dn_s, conv_p, conv_s, k_p, k_s, v_p, v_s)
```

```python
import functools

import jax
import jax.numpy as jnp
from jax import lax
from jax.experimental import pallas as pl
from jax.experimental.pallas import tpu as pltpu

F32 = jnp.float32
BF16 = jnp.bfloat16

D_MODEL = 1024
DN_HEADS = 8
DN_DK = 128
DN_DV = 128
DN_QK_W = DN_HEADS * DN_DK
DN_V_W = DN_HEADS * DN_DV
CONV_W = 4
CONV_DIM = 2 * DN_QK_W + DN_V_W
SWA_Q_HEADS = 16
SWA_KV_HEADS = 2
SWA_HD = 64
SWA_Q_W = SWA_Q_HEADS * SWA_HD
SWA_KV_W = SWA_KV_HEADS * SWA_HD
WINDOW = 128
D_FF = 2816
HALF_STEP = 0.5
N_MOD = 9
EPS = 1e-6
MASK_VALUE = -1e30
IN_SPLITS = (CONV_DIM, DN_V_W, DN_HEADS, DN_HEADS, SWA_Q_W, SWA_KV_W, SWA_KV_W, D_MODEL, D_MODEL)

LANES = 128
SUBLANES = 8
VMEM_LIMIT_BYTES = 56 * 1024 * 1024

GDN_CHUNK = 128
SWA_BLOCK = 128
FF_CHUNK = 256
ROW_TILE = 512
INPROJ_ROW_TILE = 256


def _cparams(*sem):
    return pltpu.CompilerParams(dimension_semantics=sem, vmem_limit_bytes=VMEM_LIMIT_BYTES)


def _resident(shape):
    nd = len(shape)
    return pl.BlockSpec(shape, lambda *_: (0,) * nd, pipeline_mode=pl.Buffered(1))


def _dot(a, b):
    return jnp.dot(a.astype(BF16), b.astype(BF16), preferred_element_type=F32)


def _dot_nt(a, b):
    return lax.dot_general(a.astype(BF16), b.astype(BF16), (((1,), (1,)), ((), ())),
                           preferred_element_type=F32)


def _dot_tn(a, b):
    return lax.dot_general(a.astype(BF16), b.astype(BF16), (((0,), (0,)), ((), ())),
                           preferred_element_type=F32)


def _split3(a):
    hi = a.astype(BF16)
    r = a - hi.astype(F32)
    mid = r.astype(BF16)
    lo = (r - mid.astype(F32)).astype(BF16)
    return hi, mid, lo


def _dot3(a, b):
    a_hi = a.astype(BF16)
    a_lo = (a - a_hi.astype(F32)).astype(BF16)
    b_hi = b.astype(BF16)
    b_lo = (b - b_hi.astype(F32)).astype(BF16)
    d = functools.partial(jnp.dot, preferred_element_type=F32)
    return d(a_hi, b_hi) + (d(a_lo, b_hi) + d(a_hi, b_lo))


def _silu(x):
    return x * jax.nn.sigmoid(x)


def _rms_mod(x, nw, sc, sh):
    y = x * lax.rsqrt(jnp.mean(x * x, axis=-1, keepdims=True) + EPS)
    return (y * nw) * (1.0 + sc) + sh


def _ada_kernel(c_ref, w_ref, b_ref, op_ref, os_ref, *, n_prompt):
    m = _dot3(_silu(c_ref[...]), w_ref[...]) + b_ref[...]
    op_ref[...] = m[:n_prompt]
    os_ref[...] = m[n_prompt:]


def _ada(c_all, w_ada, b_ada, n_prompt):
    n_all, d = c_all.shape
    n_out = w_ada.shape[1]
    tn = D_MODEL
    return pl.pallas_call(
        functools.partial(_ada_kernel, n_prompt=n_prompt),
        grid=(n_out // tn,),
        in_specs=[pl.BlockSpec((n_all, d), lambda j: (0, 0)),
                  pl.BlockSpec((d, tn), lambda j: (0, j)),
                  pl.BlockSpec((1, tn), lambda j: (0, j))],
        out_specs=[pl.BlockSpec((n_prompt, tn), lambda j: (0, j)),
                   pl.BlockSpec((n_all - n_prompt, tn), lambda j: (0, j))],
        out_shape=[jax.ShapeDtypeStruct((n_prompt, n_out), F32),
                   jax.ShapeDtypeStruct((n_all - n_prompt, n_out), F32)],
        compiler_params=_cparams("arbitrary"),
        name="ada",
    )(c_all, w_ada, b_ada.reshape(1, n_out))


class _Rows:
    def __init__(self, n_batch, seq, row_tile):
        self.n_rows = n_batch * seq
        if seq == 1:
            self.tm = n_batch
            self.mod_shape = (1, n_batch, N_MOD * D_MODEL)
            self.mod_block = (1, n_batch, D_MODEL)
            self._tiles_per_batch = None
        else:
            self.tm = min(row_tile, seq)
            assert seq % self.tm == 0
            self.mod_shape = (n_batch, 1, N_MOD * D_MODEL)
            self.mod_block = (1, 1, D_MODEL)
            self._tiles_per_batch = seq // self.tm
        self.grid = (self.n_rows // self.tm,)

    def mod_spec(self, piece):
        if self._tiles_per_batch is None:
            return pl.BlockSpec(self.mod_block, lambda i: (0, 0, piece))
        tpb = self._tiles_per_batch
        return pl.BlockSpec(self.mod_block, lambda i: (i // tpb, 0, piece))

    def row_spec(self, width):
        return pl.BlockSpec((self.tm, width), lambda i: (i, 0))


def _ffn_kernel(x_ref, sh_ref, sc_ref, g_ref, nw_ref, w13_ref, w2_ref, fw_ref, o_ref, *, final):
    x = x_ref[...]
    h = _rms_mod(x, nw_ref[...], sc_ref[0], sh_ref[0]).astype(BF16)
    acc = jnp.zeros(x.shape, F32)
    for c0 in range(0, D_FF, FF_CHUNK):
        gate = jnp.dot(h, w13_ref[:, c0:c0 + FF_CHUNK], preferred_element_type=F32)
        up = jnp.dot(h, w13_ref[:, D_FF + c0:D_FF + c0 + FF_CHUNK], preferred_element_type=F32)
        act = (_silu(gate) * up).astype(BF16)
        acc = acc + jnp.dot(act, w2_ref[c0:c0 + FF_CHUNK, :], preferred_element_type=F32)
    y = x + (HALF_STEP * g_ref[0]) * acc
    if final:
        y = y * lax.rsqrt(jnp.mean(y * y, axis=-1, keepdims=True) + EPS) * fw_ref[...]
    o_ref[...] = y


def _ffn(x, mod, rows, pieces, nw, w13, w2, fw, final, name):
    sh, sc, g = pieces
    return pl.pallas_call(
        functools.partial(_ffn_kernel, final=final),
        grid=rows.grid,
        in_specs=[rows.row_spec(D_MODEL), rows.mod_spec(sh), rows.mod_spec(sc), rows.mod_spec(g),
                  _resident((1, D_MODEL)), _resident(w13.shape), _resident(w2.shape),
                  _resident((1, D_MODEL))],
        out_specs=rows.row_spec(D_MODEL),
        out_shape=jax.ShapeDtypeStruct((rows.n_rows, D_MODEL), F32),
        compiler_params=_cparams("parallel"),
        name=name,
    )(x, mod, mod, mod, nw, w13, w2, fw)


IN_MAIN_PIECES = (CONV_DIM, DN_V_W, SWA_Q_W, 2 * SWA_KV_W, 2 * D_MODEL)
IN_MAIN_W = sum(IN_MAIN_PIECES)
IN_COL_CHUNK = 512


def _inproj_kernel(x_ref, sh_ref, sc_ref, nw_ref, w_ref, wab_ref, *out_refs):
    h = _rms_mod(x_ref[...], nw_ref[...], sc_ref[0], sh_ref[0])
    hb = h.astype(BF16)
    off = 0
    for ref, width in zip(out_refs[:-1], IN_MAIN_PIECES):
        for c0 in range(0, width, IN_COL_CHUNK):
            cw = min(IN_COL_CHUNK, width - c0)
            ref[:, c0:c0 + cw] = jnp.dot(hb, w_ref[:, off + c0:off + c0 + cw],
                                         preferred_element_type=F32)
        off += width
    out_refs[-1][...] = _dot3(h, wab_ref[...])


def _inproj(x, mod, rows, pieces, nw, w_main, w_ab, name):
    sh, sc = pieces
    widths = IN_MAIN_PIECES + (LANES,)
    return pl.pallas_call(
        _inproj_kernel,
        grid=rows.grid,
        in_specs=[rows.row_spec(D_MODEL), rows.mod_spec(sh), rows.mod_spec(sc),
                  _resident((1, D_MODEL)), _resident(w_main.shape), _resident(w_ab.shape)],
        out_specs=[rows.row_spec(w) for w in widths],
        out_shape=[jax.ShapeDtypeStruct((rows.n_rows, w), F32) for w in widths],
        compiler_params=_cparams("parallel"),
        name=name,
    )(x, mod, mod, nw, w_main, w_ab)


def _unit_lower_inverse(a, row, col):
    n = a.shape[0]
    eye = jnp.where(row == col, 1.0, 0.0).astype(F32)
    p = jnp.where((row >> 4) == (col >> 4), -a, 0.0)
    r = eye + p
    q = _dot(p, p)
    for _ in range(2):
        r = r + _dot(q, r)
        q = _dot(q, q)
    x = r + _dot(q, r)
    s = 4
    while (1 << s) < n:
        pair = ((row >> (s + 1)) == (col >> (s + 1))) & ((row >> s) > (col >> s))
        a_off = jnp.where(pair, a, 0.0)
        x = x - _dot(x, _dot(a_off, x))
        s += 1
    return x


def _gdn_prompt_kernel(qkv_ref, z_ref, ab_ref, cw_ref, alog_ref, dtb_ref, dnw_ref,
                       y_ref, s_out_ref, xe_ref, s_ref):
    t = pl.program_id(1)
    c = GDN_CHUNK
    halo = SUBLANES

    @pl.when(t == 0)
    def _():
        s_ref[...] = jnp.zeros(s_ref.shape, F32)
        xe_ref[0:halo, :] = jnp.zeros((halo, CONV_DIM), F32)

    xe_ref[halo:halo + c, :] = qkv_ref[...]

    row = lax.broadcasted_iota(jnp.int32, (c, c), 0)
    col = lax.broadcasted_iota(jnp.int32, (c, c), 1)
    lower = row >= col
    strict = row > col

    ab = ab_ref[...]
    g_log = -jnp.exp(alog_ref[...]) * jax.nn.softplus(ab + dtb_ref[...])
    beta_all = jax.nn.sigmoid(ab)
    tri = jnp.where(lower, 1.0, 0.0).astype(BF16)
    gc = sum(jnp.dot(tri, piece, preferred_element_type=F32) for piece in _split3(g_log))
    gc_t = gc.T
    gc_last = gc[c - 1:c, :]

    def conv_silu(c0):
        y = xe_ref[pl.ds(halo, c), c0:c0 + LANES] * cw_ref[CONV_W - 1:CONV_W, c0:c0 + LANES]
        for j in range(CONV_W - 1):
            y = y + (xe_ref[pl.ds(halo - (CONV_W - 1) + j, c), c0:c0 + LANES]
                     * cw_ref[j:j + 1, c0:c0 + LANES])
        return _silu(y)

    for h in range(DN_HEADS):
        q = conv_silu(h * DN_DK)
        k = conv_silu(DN_QK_W + h * DN_DK)
        v = conv_silu(2 * DN_QK_W + h * DN_DV)
        qn = q * lax.rsqrt(jnp.sum(q * q, axis=-1, keepdims=True) + EPS) * (DN_DK ** -0.5)
        kn = k * lax.rsqrt(jnp.sum(k * k, axis=-1, keepdims=True) + EPS)
        g_col = gc[:, h:h + 1]
        g_row = gc_t[h:h + 1, :]
        g_end = gc_last[:, h:h + 1]
        beta = beta_all[:, DN_HEADS + h:DN_HEADS + h + 1]
        decay = jnp.where(lower, jnp.exp(jnp.where(lower, g_col - g_row, 0.0)), 0.0)
        kb = kn * beta
        a_mat = jnp.where(strict, _dot_nt(kb, kn) * decay, 0.0)
        x_inv = _unit_lower_inverse(a_mat, row, col)
        e_col = jnp.exp(g_col)
        u = _dot(x_inv, v * beta)
        w = _dot(x_inv, kb * e_col)
        qk = _dot_nt(qn, kn) * decay
        s_old = s_ref[h]
        v_new = u - _dot(w, s_old)
        o = _dot(qn * e_col, s_old) + _dot(qk, v_new)
        k_dec = kn * jnp.exp(g_end - g_col)
        s_ref[h] = s_old * jnp.exp(g_end) + _dot_tn(k_dec, v_new)
        o = o * lax.rsqrt(jnp.mean(o * o, axis=-1, keepdims=True) + EPS) * dnw_ref[...]
        y_ref[:, h * DN_DV:(h + 1) * DN_DV] = o * _silu(z_ref[:, h * DN_DV:(h + 1) * DN_DV])

    xe_ref[0:halo, :] = xe_ref[c:c + halo, :]

    @pl.when(t == pl.num_programs(1) - 1)
    def _():
        s_out_ref[0] = s_ref[...]


def _gdn_prompt(qkv, z, ab, conv_w, alog_pad, dtb_pad, dn_norm, n_batch, seq):
    c = GDN_CHUNK
    nt = seq // c
    row_spec = lambda w: pl.BlockSpec((c, w), lambda b, t: (b * nt + t, 0))
    return pl.pallas_call(
        _gdn_prompt_kernel,
        grid=(n_batch, nt),
        in_specs=[row_spec(CONV_DIM), row_spec(DN_V_W), row_spec(LANES),
                  pl.BlockSpec((CONV_W, CONV_DIM), lambda b, t: (0, 0)),
                  pl.BlockSpec((1, LANES), lambda b, t: (0, 0)),
                  pl.BlockSpec((1, LANES), lambda b, t: (0, 0)),
                  pl.BlockSpec((1, DN_DV), lambda b, t: (0, 0))],
        out_specs=[row_spec(DN_V_W),
                   pl.BlockSpec((1, DN_HEADS, DN_DK, DN_DV), lambda b, t: (b, 0, 0, 0))],
        out_shape=[jax.ShapeDtypeStruct((n_batch * seq, DN_V_W), F32),
                   jax.ShapeDtypeStruct((n_batch, DN_HEADS, DN_DK, DN_DV), F32)],
        scratch_shapes=[pltpu.VMEM((c + SUBLANES, CONV_DIM), F32),
                        pltpu.VMEM((DN_HEADS, DN_DK, DN_DV), F32)],
        compiler_params=_cparams("parallel", "arbitrary"),
        name="gdn_prompt",
    )(qkv, z, ab, conv_w, alog_pad, dtb_pad, dn_norm)


def _kv_head_views(k2, v2, lo):
    k2r = pltpu.roll(k2, SWA_HD, axis=1)
    v2r = pltpu.roll(v2, SWA_HD, axis=1)
    k_lo = (jnp.where(lo, k2, 0.0), jnp.where(lo, k2r, 0.0))
    k_hi = (jnp.where(lo, 0.0, k2r), jnp.where(lo, 0.0, k2))
    v_dup = (jnp.where(lo, v2, v2r), jnp.where(lo, v2r, v2))
    return k_lo, k_hi, v_dup


def _sink_softmax_pv(s, sink, v):
    m = jnp.maximum(jnp.max(s, axis=-1, keepdims=True), sink)
    p = jnp.exp(s - m)
    den = jnp.sum(p, axis=-1, keepdims=True) + jnp.exp(sink - m)
    return _dot(p, v) / den


def _swa_prompt_kernel(sink_ref, q_ref, kvc_ref, kvp_ref, o_ref):
    n = pl.program_id(1)
    blk = SWA_BLOCK
    kv = jnp.concatenate([kvp_ref[...], kvc_ref[...]], axis=0)
    lo_k = lax.broadcasted_iota(jnp.int32, (2 * blk, LANES), 1) < SWA_HD
    k_lo, k_hi, v_dup = _kv_head_views(kv[:, :LANES], kv[:, LANES:], lo_k)
    qi = lax.broadcasted_iota(jnp.int32, (blk, 2 * blk), 0)
    kj = lax.broadcasted_iota(jnp.int32, (blk, 2 * blk), 1)
    dist = qi - kj + blk
    valid = (dist >= 0) & (dist < WINDOW) & ((kj >= blk) | (n > 0))
    lo_o = lax.broadcasted_iota(jnp.int32, (blk, LANES), 1) < SWA_HD
    scale = SWA_HD ** -0.5
    pairs_per_kv = SWA_Q_HEADS // SWA_KV_HEADS // 2
    for j in range(SWA_Q_HEADS // 2):
        qp = q_ref[:, j * LANES:(j + 1) * LANES].astype(BF16)
        kvh = j // pairs_per_kv
        halves = []
        for half, keys in enumerate((k_lo[kvh], k_hi[kvh])):
            s = jnp.where(valid, _dot_nt(qp, keys) * scale, MASK_VALUE)
            halves.append(_sink_softmax_pv(s, sink_ref[2 * j + half], v_dup[kvh]))
        o_ref[:, j * LANES:(j + 1) * LANES] = jnp.where(lo_o, halves[0], halves[1])


def _swa_prompt(q, kv, sinks, n_batch, seq):
    blk = SWA_BLOCK
    nb = seq // blk
    return pl.pallas_call(
        _swa_prompt_kernel,
        grid=(n_batch, nb),
        in_specs=[pl.BlockSpec(memory_space=pltpu.SMEM),
                  pl.BlockSpec((blk, SWA_Q_W), lambda b, n: (b * nb + n, 0)),
                  pl.BlockSpec((blk, 2 * SWA_KV_W), lambda b, n: (b * nb + n, 0)),
                  pl.BlockSpec((blk, 2 * SWA_KV_W), lambda b, n: (b * nb + jnp.maximum(n - 1, 0), 0))],
        out_specs=pl.BlockSpec((blk, SWA_Q_W), lambda b, n: (b * nb + n, 0)),
        out_shape=jax.ShapeDtypeStruct((n_batch * seq, SWA_Q_W), F32),
        compiler_params=_cparams("parallel", "arbitrary"),
        name="swa_prompt",
    )(sinks, q, kv, kv)


def _mix_kernel(x_ref, ydn_ref, ysw_ref, gates_ref, g_ref, wbd_ref, wbs_ref, wout_ref, o_ref):
    a = _dot(ydn_ref[...], wbd_ref[...])
    b = _dot(ysw_ref[...], wbs_ref[...])
    merged = (jax.nn.sigmoid(gates_ref[:, :D_MODEL]) * a
              + jax.nn.sigmoid(gates_ref[:, D_MODEL:]) * b)
    o_ref[...] = x_ref[...] + g_ref[0] * _dot(merged, wout_ref[...])


def _mix(x, y_dn, y_sw, gates, mod, rows, gate_piece, w_bd, w_bs, w_out, name):
    return pl.pallas_call(
        _mix_kernel,
        grid=rows.grid,
        in_specs=[rows.row_spec(D_MODEL), rows.row_spec(DN_V_W), rows.row_spec(SWA_Q_W),
                  rows.row_spec(2 * D_MODEL), rows.mod_spec(gate_piece),
                  _resident(w_bd.shape), _resident(w_bs.shape), _resident(w_out.shape)],
        out_specs=rows.row_spec(D_MODEL),
        out_shape=jax.ShapeDtypeStruct((rows.n_rows, D_MODEL), F32),
        compiler_params=_cparams("parallel"),
        name=name,
    )(x, y_dn, y_sw, gates, mod, w_bd, w_bs, w_out)


def _gdn_step_prep_kernel(qkv_ref, cs_ref, ab_ref, cw_ref, alog_ref, dtb_ref,
                          cs_out_ref, qt_ref, kt_ref, v_ref, dec_ref, beta_ref):
    nb = qkv_ref.shape[0]
    keep = (CONV_W - 1) * CONV_DIM
    cs_out_ref[:, 0:keep - CONV_DIM] = cs_ref[:, CONV_DIM:keep]
    cs_out_ref[:, keep - CONV_DIM:keep] = qkv_ref[...]
    ab = ab_ref[...]
    dec = jnp.exp(-jnp.exp(alog_ref[...]) * jax.nn.softplus(ab + dtb_ref[...]))
    beta = jax.nn.sigmoid(ab)

    def conv_silu(c0):
        y = qkv_ref[:, c0:c0 + LANES] * cw_ref[CONV_W - 1:CONV_W, c0:c0 + LANES]
        for j in range(CONV_W - 1):
            y = y + (cs_ref[:, j * CONV_DIM + c0:j * CONV_DIM + c0 + LANES]
                     * cw_ref[j:j + 1, c0:c0 + LANES])
        return _silu(y)

    for h in range(DN_HEADS):
        q = conv_silu(h * DN_DK)
        k = conv_silu(DN_QK_W + h * DN_DK)
        qn = q * lax.rsqrt(jnp.sum(q * q, axis=-1, keepdims=True) + EPS) * (DN_DK ** -0.5)
        kn = k * lax.rsqrt(jnp.sum(k * k, axis=-1, keepdims=True) + EPS)
        qt_ref[h] = qn.T
        kt_ref[h] = kn.T
        sl = slice(h * DN_DV, (h + 1) * DN_DV)
        v_ref[:, sl] = conv_silu(2 * DN_QK_W + h * DN_DV)
        dec_ref[:, sl] = jnp.broadcast_to(dec[:, h:h + 1], (nb, DN_DV))
        beta_ref[:, sl] = jnp.broadcast_to(beta[:, DN_HEADS + h:DN_HEADS + h + 1], (nb, DN_DV))


def _gdn_step_prep(qkv, conv_state, ab, conv_w, alog_pad, dtb_pad):
    nb = qkv.shape[0]
    keep = (CONV_W - 1) * CONV_DIM
    full = lambda shape: pl.BlockSpec(shape, lambda i: (0,) * len(shape))
    return pl.pallas_call(
        _gdn_step_prep_kernel,
        grid=(1,),
        in_specs=[full((nb, CONV_DIM)), full((nb, keep)), full((nb, LANES)),
                  full((CONV_W, CONV_DIM)), full((1, LANES)), full((1, LANES))],
        out_specs=[full((nb, keep)), full((DN_HEADS, DN_DK, nb)), full((DN_HEADS, DN_DK, nb)),
                   full((nb, DN_V_W)), full((nb, DN_V_W)), full((nb, DN_V_W))],
        out_shape=[jax.ShapeDtypeStruct((nb, keep), F32),
                   jax.ShapeDtypeStruct((DN_HEADS, DN_DK, nb), F32),
                   jax.ShapeDtypeStruct((DN_HEADS, DN_DK, nb), F32),
                   jax.ShapeDtypeStruct((nb, DN_V_W), F32),
                   jax.ShapeDtypeStruct((nb, DN_V_W), F32),
                   jax.ShapeDtypeStruct((nb, DN_V_W), F32)],
        compiler_params=_cparams("arbitrary"),
        name="gdn_step_prep",
    )(qkv, conv_state, ab, conv_w, alog_pad, dtb_pad)


def _gdn_step_kernel(s_ref, qt_ref, kt_ref, v_ref, dec_ref, beta_ref, z_ref, dnw_ref,
                     s_out_ref, y_ref, o_scr):
    nb = s_ref.shape[0]
    qt = qt_ref[0]
    kt = kt_ref[0]
    for b in range(nb):
        k_col = kt[:, b:b + 1]
        q_col = qt[:, b:b + 1]
        s1 = s_ref[b, 0] * dec_ref[b:b + 1, :]
        kv = jnp.sum(s1 * k_col, axis=0, keepdims=True)
        delta = (v_ref[b:b + 1, :] - kv) * beta_ref[b:b + 1, :]
        s2 = s1 + k_col * delta
        s_out_ref[b, 0] = s2
        o_scr[b:b + 1, :] = jnp.sum(s2 * q_col, axis=0, keepdims=True)
    o = o_scr[...]
    o = o * lax.rsqrt(jnp.mean(o * o, axis=-1, keepdims=True) + EPS) * dnw_ref[...]
    y_ref[...] = o * _silu(z_ref[...])


def _gdn_step(state, qt, kt, v, dec, beta, z, dn_norm):
    nb = state.shape[0]
    head_cols = pl.BlockSpec((nb, DN_DV), lambda h: (0, h))
    head_t = pl.BlockSpec((1, DN_DK, nb), lambda h: (h, 0, 0))
    s_spec = pl.BlockSpec((nb, 1, DN_DK, DN_DV), lambda h: (0, h, 0, 0))
    return pl.pallas_call(
        _gdn_step_kernel,
        grid=(DN_HEADS,),
        in_specs=[s_spec, head_t, head_t, head_cols, head_cols, head_cols, head_cols,
                  pl.BlockSpec((1, DN_DV), lambda h: (0, 0))],
        out_specs=[s_spec, head_cols],
        out_shape=[jax.ShapeDtypeStruct(state.shape, F32),
                   jax.ShapeDtypeStruct((nb, DN_V_W), F32)],
        scratch_shapes=[pltpu.VMEM((nb, DN_DV), F32)],
        compiler_params=_cparams("parallel"),
        name="gdn_step",
    )(state, qt, kt, v, dec, beta, z, dn_norm)


SWA_STEP_BATCH = 8


def _swa_step_kernel(q_ref, kvn_ref, ck_ref, cv_ref, slo_ref, shi_ref, o_ref, ck_out_ref, cv_out_ref):
    length = ck_ref.shape[1]
    last = lax.broadcasted_iota(jnp.int32, (length, LANES), 0) == length - 1
    lo_k = lax.broadcasted_iota(jnp.int32, (length, LANES), 1) < SWA_HD
    pairs = SWA_Q_HEADS // 2
    first_kv = lax.broadcasted_iota(jnp.int32, (pairs, LANES), 0) < pairs // SWA_KV_HEADS
    lo_o = lax.broadcasted_iota(jnp.int32, (pairs, LANES), 1) < SWA_HD
    scale = SWA_HD ** -0.5
    for b in range(q_ref.shape[0]):
        k2 = jnp.where(last, kvn_ref[b:b + 1, 0:LANES], pltpu.roll(ck_ref[b], length - 1, axis=0))
        v2 = jnp.where(last, kvn_ref[b:b + 1, LANES:2 * LANES], pltpu.roll(cv_ref[b], length - 1, axis=0))
        ck_out_ref[b] = k2
        cv_out_ref[b] = v2
        k_lo, k_hi, v_dup = _kv_head_views(k2, v2, lo_k)
        q8 = q_ref[b].astype(BF16)
        halves = []
        for keys, sink_ref in ((k_lo, slo_ref), (k_hi, shi_ref)):
            s = jnp.where(first_kv, _dot_nt(q8, keys[0]), _dot_nt(q8, keys[1])) * scale
            sink = sink_ref[...]
            m = jnp.maximum(jnp.max(s, axis=-1, keepdims=True), sink)
            p = jnp.exp(s - m)
            den = jnp.sum(p, axis=-1, keepdims=True) + jnp.exp(sink - m)
            halves.append(jnp.where(first_kv, _dot(p, v_dup[0]), _dot(p, v_dup[1])) / den)
        o_ref[b] = jnp.where(lo_o, halves[0], halves[1])


def _swa_step(q3, kv_new, cache_k, cache_v, sink_lo, sink_hi):
    nb, length, _ = cache_k.shape
    tb = SWA_STEP_BATCH
    pairs = SWA_Q_HEADS // 2
    q_spec = pl.BlockSpec((tb, pairs, LANES), lambda i: (i, 0, 0))
    c_spec = pl.BlockSpec((tb, length, LANES), lambda i: (i, 0, 0))
    sink_spec = pl.BlockSpec((pairs, LANES), lambda i: (0, 0))
    return pl.pallas_call(
        _swa_step_kernel,
        grid=(nb // tb,),
        in_specs=[q_spec, pl.BlockSpec((tb, 2 * SWA_KV_W), lambda i: (i, 0)), c_spec, c_spec,
                  sink_spec, sink_spec],
        out_specs=[q_spec, c_spec, c_spec],
        out_shape=[jax.ShapeDtypeStruct(q3.shape, F32),
                   jax.ShapeDtypeStruct(cache_k.shape, F32),
                   jax.ShapeDtypeStruct(cache_v.shape, F32)],
        compiler_params=_cparams("parallel"),
        name="swa_step",
    )(q3, kv_new, cache_k, cache_v, sink_lo, sink_hi)


def _prep_layer_weights(lp):
    offs = [0]
    for wdt in IN_SPLITS:
        offs.append(offs[-1] + wdt)
    w_in = lp['w_in']
    piece = lambda i: w_in[:, offs[i]:offs[i + 1]]
    w_main = jnp.concatenate([piece(0), piece(1), piece(4), piece(5), piece(6), piece(7), piece(8)],
                             axis=1).astype(BF16)
    w_ab = jnp.pad(jnp.concatenate([piece(2), piece(3)], axis=1), ((0, 0), (0, LANES - 2 * DN_HEADS)))
    pad_heads = lambda a: jnp.pad(a.astype(F32), (0, LANES - DN_HEADS)).reshape(1, LANES)
    sinks = lp['sinks'].astype(F32)
    pairs = SWA_Q_HEADS // 2
    return dict(
        w13_ffn1=lp['w13_ffn1'].astype(BF16), w2_ffn1=lp['w2_ffn1'].astype(BF16),
        w13_ffn2=lp['w13_ffn2'].astype(BF16), w2_ffn2=lp['w2_ffn2'].astype(BF16),
        w_main=w_main, w_ab=w_ab,
        w_br_dn=lp['w_br_dn'].astype(BF16), w_br_swa=lp['w_br_swa'].astype(BF16),
        w_out=lp['w_out'].astype(BF16),
        norm_ffn1=lp['norm_ffn1'].reshape(1, D_MODEL), norm_mix=lp['norm_mix'].reshape(1, D_MODEL),
        norm_ffn2=lp['norm_ffn2'].reshape(1, D_MODEL),
        conv_w=lp['conv_w'], alog_pad=pad_heads(lp['a_log']), dtb_pad=pad_heads(lp['dt_bias']),
        dn_norm=lp['dn_norm'].reshape(1, DN_DV), sinks=sinks,
        sink_lo=jnp.broadcast_to(sinks.reshape(pairs, 2)[:, 0:1], (pairs, LANES)),
        sink_hi=jnp.broadcast_to(sinks.reshape(pairs, 2)[:, 1:2], (pairs, LANES)),
    )


def _layer(x, mod, rows, w, final_w, final, past, n_batch, seq, tag):
    x = _ffn(x, mod, rows, (0, 1, 2), w['norm_ffn1'], w['w13_ffn1'], w['w2_ffn1'], final_w,
             False, "ffn1_" + tag)
    in_rows = rows if past is not None else _Rows(n_batch, seq, INPROJ_ROW_TILE)
    qkv, z, q_sw, kv, gates, ab = _inproj(x, mod, in_rows, (3, 4), w['norm_mix'], w['w_main'], w['w_ab'],
                                         "inproj_" + tag)
    if past is None:
        y_dn, s_new = _gdn_prompt(qkv, z, ab, w['conv_w'], w['alog_pad'], w['dtb_pad'], w['dn_norm'],
                                  n_batch, seq)
        y_sw = _swa_prompt(q_sw, kv, w['sinks'], n_batch, seq)
        conv_new = qkv.reshape(n_batch, seq, CONV_DIM)[:, seq - (CONV_W - 1):]
        keep = min(WINDOW, seq)
        kv3 = kv.reshape(n_batch, seq, 2 * SWA_KV_W)[:, seq - keep:]
        k_buf = kv3[:, :, :SWA_KV_W].reshape(n_batch, keep, SWA_KV_HEADS, SWA_HD)
        v_buf = kv3[:, :, SWA_KV_W:].reshape(n_batch, keep, SWA_KV_HEADS, SWA_HD)
    else:
        s0, conv_buf, k_old, v_old = past
        length = k_old.shape[1]
        conv_new, qt, kt, v, dec, beta = _gdn_step_prep(
            qkv, conv_buf.reshape(n_batch, (CONV_W - 1) * CONV_DIM), ab, w['conv_w'],
            w['alog_pad'], w['dtb_pad'])
        s_new, y_dn = _gdn_step(s0, qt, kt, v, dec, beta, z, w['dn_norm'])
        conv_new = conv_new.reshape(n_batch, CONV_W - 1, CONV_DIM)
        o3, k_buf, v_buf = _swa_step(q_sw.reshape(n_batch, SWA_Q_HEADS // 2, LANES), kv,
                                     k_old.reshape(n_batch, length, SWA_KV_W),
                                     v_old.reshape(n_batch, length, SWA_KV_W),
                                     w['sink_lo'], w['sink_hi'])
        y_sw = o3.reshape(n_batch, SWA_Q_W)
        k_buf = k_buf.reshape(n_batch, length, SWA_KV_HEADS, SWA_HD)
        v_buf = v_buf.reshape(n_batch, length, SWA_KV_HEADS, SWA_HD)
    x = _mix(x, y_dn, y_sw, gates, mod, rows, 5, w['w_br_dn'], w['w_br_swa'], w['w_out'], "mix_" + tag)
    x = _ffn(x, mod, rows, (6, 7, 8), w['norm_ffn2'], w['w13_ffn2'], w['w2_ffn2'], final_w,
             final, "ffn2_" + tag)
    return x, (s_new, conv_new, k_buf, v_buf)


def kernel(x_prompt, x_sample, state_dn, state_conv, cache_swa_k, cache_swa_v, c_prompt, c_sample,
           w_ada, b_ada, norm_ffn1, w13_ffn1, w2_ffn1, norm_mix, w_in, conv_w, a_log, dt_bias,
           dn_norm, sinks, w_br_dn, w_br_swa, w_out, norm_ffn2, w13_ffn2, w2_ffn2, final_norm):
    n_p, seq_p, d = x_prompt.shape
    n_s, seq_s, _ = x_sample.shape
    depth = w_ada.shape[0]
    assert d == D_MODEL and seq_s == 1 and seq_p % ROW_TILE == 0 and seq_p % GDN_CHUNK == 0
    assert w_in.shape[2] == sum(IN_SPLITS) and w13_ffn1.shape[2] == 2 * D_FF
    assert cache_swa_k.shape[2] == WINDOW and n_s % SWA_STEP_BATCH == 0 and n_p % SUBLANES == 0
    rows_p = _Rows(n_p, seq_p, ROW_TILE)
    rows_s = _Rows(n_s, seq_s, ROW_TILE)
    c_all = jnp.concatenate([c_prompt, c_sample], axis=0)
    final_w = final_norm.reshape(1, D_MODEL)
    y_p = x_prompt.reshape(n_p * seq_p, d)
    y_s = x_sample.reshape(n_s * seq_s, d)
    st_p, st_s = [], []
    for l in range(depth):
        lp = dict(w_ada=w_ada[l], b_ada=b_ada[l], norm_ffn1=norm_ffn1[l], w13_ffn1=w13_ffn1[l],
                  w2_ffn1=w2_ffn1[l], norm_mix=norm_mix[l], w_in=w_in[l], conv_w=conv_w[l],
                  a_log=a_log[l], dt_bias=dt_bias[l], dn_norm=dn_norm[l], sinks=sinks[l],
                  w_br_dn=w_br_dn[l], w_br_swa=w_br_swa[l], w_out=w_out[l], norm_ffn2=norm_ffn2[l],
                  w13_ffn2=w13_ffn2[l], w2_ffn2=w2_ffn2[l])
        w = _prep_layer_weights(lp)
        mod_p, mod_s = _ada(c_all, lp['w_ada'], lp['b_ada'], n_p)
        final = l == depth - 1
        y_p, sp = _layer(y_p, mod_p.reshape(rows_p.mod_shape), rows_p, w, final_w, final, None,
                         n_p, seq_p, "prompt")
        y_s, ss = _layer(y_s, mod_s.reshape(rows_s.mod_shape), rows_s, w, final_w, final,
                         (state_dn[l], state_conv[l], cache_swa_k[l], cache_swa_v[l]),
                         n_s, seq_s, "sample")
        st_p.append(sp)
        st_s.append(ss)
    stack = lambda sts, i: jnp.stack([s[i] for s in sts])
    return (y_p.reshape(n_p, seq_p, d), y_s.reshape(n_s, seq_s, d),
            stack(st_p, 0), stack(st_s, 0), stack(st_p, 1), stack(st_s, 1),
            stack(st_p, 2), stack(st_s, 2), stack(st_p, 3), stack(st_s, 3))
```

```python
import functools

import jax
import jax.numpy as jnp
from jax import lax
from jax.experimental import pallas as pl
from jax.experimental.pallas import tpu as pltpu

F32 = jnp.float32
BF16 = jnp.bfloat16

D_MODEL = 1024
DN_HEADS = 8
DN_DK = 128
DN_DV = 128
DN_QK_W = DN_HEADS * DN_DK
DN_V_W = DN_HEADS * DN_DV
CONV_W = 4
CONV_DIM = 2 * DN_QK_W + DN_V_W
SWA_Q_HEADS = 16
SWA_KV_HEADS = 2
SWA_HD = 64
SWA_Q_W = SWA_Q_HEADS * SWA_HD
SWA_KV_W = SWA_KV_HEADS * SWA_HD
WINDOW = 128
D_FF = 2816
HALF_STEP = 0.5
N_MOD = 9
EPS = 1e-6
MASK_VALUE = -1e30
IN_SPLITS = (CONV_DIM, DN_V_W, DN_HEADS, DN_HEADS, SWA_Q_W, SWA_KV_W, SWA_KV_W, D_MODEL, D_MODEL)

LANES = 128
SUBLANES = 8
VMEM_LIMIT_BYTES = 56 * 1024 * 1024

GDN_CHUNK = 128
SWA_BLOCK = 128
FF_CHUNK = 256
ROW_TILE = 512
INPROJ_ROW_TILE = 256


def _cparams(*sem):
    return pltpu.CompilerParams(dimension_semantics=sem, vmem_limit_bytes=VMEM_LIMIT_BYTES)


def _resident(shape):
    nd = len(shape)
    return pl.BlockSpec(shape, lambda *_: (0,) * nd, pipeline_mode=pl.Buffered(1))


def _dot(a, b):
    return jnp.dot(a.astype(BF16), b.astype(BF16), preferred_element_type=F32)


def _dot_nt(a, b):
    return lax.dot_general(a.astype(BF16), b.astype(BF16), (((1,), (1,)), ((), ())),
                           preferred_element_type=F32)


def _dot_tn(a, b):
    return lax.dot_general(a.astype(BF16), b.astype(BF16), (((0,), (0,)), ((), ())),
                           preferred_element_type=F32)


def _split3(a):
    hi = a.astype(BF16)
    r = a - hi.astype(F32)
    mid = r.astype(BF16)
    lo = (r - mid.astype(F32)).astype(BF16)
    return hi, mid, lo


def _dot3(a, b):
    a_hi = a.astype(BF16)
    a_lo = (a - a_hi.astype(F32)).astype(BF16)
    b_hi = b.astype(BF16)
    b_lo = (b - b_hi.astype(F32)).astype(BF16)
    d = functools.partial(jnp.dot, preferred_element_type=F32)
    return d(a_hi, b_hi) + (d(a_lo, b_hi) + d(a_hi, b_lo))


def _silu(x):
    return x * jax.nn.sigmoid(x)


def _rms_mod(x, nw, sc, sh):
    y = x * lax.rsqrt(jnp.mean(x * x, axis=-1, keepdims=True) + EPS)
    return (y * nw) * (1.0 + sc) + sh


def _ada_kernel(c_ref, w_ref, b_ref, op_ref, os_ref, *, n_prompt):
    m = _dot3(_silu(c_ref[...]), w_ref[...]) + b_ref[...]
    op_ref[...] = m[:n_prompt]
    os_ref[...] = m[n_prompt:]


def _ada(c_all, w_ada, b_ada, n_prompt):
    n_all, d = c_all.shape
    n_out = w_ada.shape[1]
    tn = D_MODEL
    return pl.pallas_call(
        functools.partial(_ada_kernel, n_prompt=n_prompt),
        grid=(n_out // tn,),
        in_specs=[pl.BlockSpec((n_all, d), lambda j: (0, 0)),
                  pl.BlockSpec((d, tn), lambda j: (0, j)),
                  pl.BlockSpec((1, tn), lambda j: (0, j))],
        out_specs=[pl.BlockSpec((n_prompt, tn), lambda j: (0, j)),
                   pl.BlockSpec((n_all - n_prompt, tn), lambda j: (0, j))],
        out_shape=[jax.ShapeDtypeStruct((n_prompt, n_out), F32),
                   jax.ShapeDtypeStruct((n_all - n_prompt, n_out), F32)],
        compiler_params=_cparams("arbitrary"),
        name="ada",
    )(c_all, w_ada, b_ada.reshape(1, n_out))


class _Rows:
    def __init__(self, n_batch, seq, row_tile):
        self.n_rows = n_batch * seq
        if seq == 1:
            self.tm = n_batch
            self.mod_shape = (1, n_batch, N_MOD * D_MODEL)
            self.mod_block = (1, n_batch, D_MODEL)
            self._tiles_per_batch = None
        else:
            self.tm = min(row_tile, seq)
            assert seq % self.tm == 0
            self.mod_shape = (n_batch, 1, N_MOD * D_MODEL)
            self.mod_block = (1, 1, D_MODEL)
            self._tiles_per_batch = seq // self.tm
        self.grid = (self.n_rows // self.tm,)

    def mod_spec(self, piece):
        if self._tiles_per_batch is None:
            return pl.BlockSpec(self.mod_block, lambda i: (0, 0, piece))
        tpb = self._tiles_per_batch
        return pl.BlockSpec(self.mod_block, lambda i: (i // tpb, 0, piece))

    def row_spec(self, width):
        return pl.BlockSpec((self.tm, width), lambda i: (i, 0))


def _ffn_kernel(x_ref, sh_ref, sc_ref, g_ref, nw_ref, w13_ref, w2_ref, fw_ref, o_ref, *, final):
    x = x_ref[...]
    h = _rms_mod(x, nw_ref[...], sc_ref[0], sh_ref[0]).astype(BF16)
    acc = jnp.zeros(x.shape, F32)
    for c0 in range(0, D_FF, FF_CHUNK):
        gate = jnp.dot(h, w13_ref[:, c0:c0 + FF_CHUNK], preferred_element_type=F32)
        up = jnp.dot(h, w13_ref[:, D_FF + c0:D_FF + c0 + FF_CHUNK], preferred_element_type=F32)
        act = (_silu(gate) * up).astype(BF16)
        acc = acc + jnp.dot(act, w2_ref[c0:c0 + FF_CHUNK, :], preferred_element_type=F32)
    y = x + (HALF_STEP * g_ref[0]) * acc
    if final:
        y = y * lax.rsqrt(jnp.mean(y * y, axis=-1, keepdims=True) + EPS) * fw_ref[...]
    o_ref[...] = y


def _ffn(x, mod, rows, pieces, nw, w13, w2, fw, final, name):
    sh, sc, g = pieces
    return pl.pallas_call(
        functools.partial(_ffn_kernel, final=final),
        grid=rows.grid,
        in_specs=[rows.row_spec(D_MODEL), rows.mod_spec(sh), rows.mod_spec(sc), rows.mod_spec(g),
                  _resident((1, D_MODEL)), _resident(w13.shape), _resident(w2.shape),
                  _resident((1, D_MODEL))],
        out_specs=rows.row_spec(D_MODEL),
        out_shape=jax.ShapeDtypeStruct((rows.n_rows, D_MODEL), F32),
        compiler_params=_cparams("parallel"),
        name=name,
    )(x, mod, mod, mod, nw, w13, w2, fw)


IN_MAIN_PIECES = (CONV_DIM, DN_V_W, SWA_Q_W, 2 * SWA_KV_W, 2 * D_MODEL)
IN_MAIN_W = sum(IN_MAIN_PIECES)
IN_COL_CHUNK = 512


def _inproj_kernel(x_ref, sh_ref, sc_ref, nw_ref, w_ref, wab_ref, *out_refs):
    h = _rms_mod(x_ref[...], nw_ref[...], sc_ref[0], sh_ref[0])
    hb = h.astype(BF16)
    off = 0
    for ref, width in zip(out_refs[:-1], IN_MAIN_PIECES):
        for c0 in range(0, width, IN_COL_CHUNK):
            cw = min(IN_COL_CHUNK, width - c0)
            ref[:, c0:c0 + cw] = jnp.dot(hb, w_ref[:, off + c0:off + c0 + cw],
                                         preferred_element_type=F32)
        off += width
    out_refs[-1][...] = _dot3(h, wab_ref[...])


def _inproj(x, mod, rows, pieces, nw, w_main, w_ab, name):
    sh, sc = pieces
    widths = IN_MAIN_PIECES + (LANES,)
    return pl.pallas_call(
        _inproj_kernel,
        grid=rows.grid,
        in_specs=[rows.row_spec(D_MODEL), rows.mod_spec(sh), rows.mod_spec(sc),
                  _resident((1, D_MODEL)), _resident(w_main.shape), _resident(w_ab.shape)],
        out_specs=[rows.row_spec(w) for w in widths],
        out_shape=[jax.ShapeDtypeStruct((rows.n_rows, w), F32) for w in widths],
        compiler_params=_cparams("parallel"),
        name=name,
    )(x, mod, mod, nw, w_main, w_ab)


def _unit_lower_inverse(mats, row, col):
    n = mats[0].shape[0]
    eye = jnp.where(row == col, 1.0, 0.0).astype(F32)
    in_block = (row >> 4) == (col >> 4)
    p = [jnp.where(in_block, -a, 0.0) for a in mats]
    r = [eye + pi for pi in p]
    q = [_dot(pi, pi) for pi in p]
    for _ in range(2):
        rq = [_dot(qi, jnp.concatenate([ri, qi], axis=1)) for qi, ri in zip(q, r)]
        r = [ri + rqi[:, :n] for ri, rqi in zip(r, rq)]
        q = [rqi[:, n:] for rqi in rq]
    x = [ri + _dot(qi, ri) for qi, ri in zip(q, r)]
    s = 4
    while (1 << s) < n:
        pair = ((row >> (s + 1)) == (col >> (s + 1))) & ((row >> s) > (col >> s))
        t = [_dot(jnp.where(pair, a, 0.0), xi) for a, xi in zip(mats, x)]
        x = [xi - _dot(xi, ti) for xi, ti in zip(x, t)]
        s += 1
    return x


def _gdn_prompt_kernel(qkv_ref, z_ref, ab_ref, cw_ref, alog_ref, dtb_ref, dnw_ref,
                       y_ref, s_out_ref, xe_ref, s_ref):
    t = pl.program_id(1)
    c = GDN_CHUNK
    halo = SUBLANES
    heads = range(DN_HEADS)

    @pl.when(t == 0)
    def _():
        s_ref[...] = jnp.zeros(s_ref.shape, F32)
        xe_ref[0:halo, :] = jnp.zeros((halo, CONV_DIM), F32)

    xe_ref[halo:halo + c, :] = qkv_ref[...]

    row = lax.broadcasted_iota(jnp.int32, (c, c), 0)
    col = lax.broadcasted_iota(jnp.int32, (c, c), 1)
    lower = row >= col
    strict = row > col

    ab = ab_ref[...]
    g_log = -jnp.exp(alog_ref[...]) * jax.nn.softplus(ab + dtb_ref[...])
    beta_all = jax.nn.sigmoid(ab)
    tri = jnp.where(lower, 1.0, 0.0).astype(BF16)
    gc = sum(jnp.dot(tri, piece, preferred_element_type=F32) for piece in _split3(g_log))
    gc_t = gc.T
    gc_last = gc[c - 1:c, :]

    def conv_silu(c0):
        y = xe_ref[pl.ds(halo, c), c0:c0 + LANES] * cw_ref[CONV_W - 1:CONV_W, c0:c0 + LANES]
        for j in range(CONV_W - 1):
            y = y + (xe_ref[pl.ds(halo - (CONV_W - 1) + j, c), c0:c0 + LANES]
                     * cw_ref[j:j + 1, c0:c0 + LANES])
        return _silu(y)

    def l2n(a):
        return a * lax.rsqrt(jnp.sum(a * a, axis=-1, keepdims=True) + EPS)

    qn = [l2n(conv_silu(h * DN_DK)) * (DN_DK ** -0.5) for h in heads]
    kn = [l2n(conv_silu(DN_QK_W + h * DN_DK)) for h in heads]
    v = [conv_silu(2 * DN_QK_W + h * DN_DV) for h in heads]
    g_col = [gc[:, h:h + 1] for h in heads]
    g_end = [gc_last[:, h:h + 1] for h in heads]
    beta = [beta_all[:, DN_HEADS + h:DN_HEADS + h + 1] for h in heads]
    decay = [jnp.where(lower, jnp.exp(jnp.where(lower, g_col[h] - gc_t[h:h + 1, :], 0.0)), 0.0)
             for h in heads]
    kb = [kn[h] * beta[h] for h in heads]
    e_col = [jnp.exp(g_col[h]) for h in heads]

    kq = [_dot_nt(jnp.concatenate([kb[h], qn[h]], axis=0), kn[h]) for h in heads]
    a_mat = [jnp.where(strict, kq[h][:c] * decay[h], 0.0) for h in heads]
    qk = [kq[h][c:] * decay[h] for h in heads]
    x_inv = _unit_lower_inverse(a_mat, row, col)
    uw = [_dot(x_inv[h], jnp.concatenate([v[h] * beta[h], kb[h] * e_col[h]], axis=1)) for h in heads]
    s_old = [s_ref[h] for h in heads]
    ws = [_dot(jnp.concatenate([uw[h][:, DN_DV:], qn[h] * e_col[h]], axis=0), s_old[h]) for h in heads]
    v_new = [uw[h][:, :DN_DV] - ws[h][:c] for h in heads]
    o = [ws[h][c:] + _dot(qk[h], v_new[h]) for h in heads]
    for h in heads:
        k_dec = kn[h] * jnp.exp(g_end[h] - g_col[h])
        s_ref[h] = s_old[h] * jnp.exp(g_end[h]) + _dot_tn(k_dec, v_new[h])
    for h in heads:
        oh = o[h] * lax.rsqrt(jnp.mean(o[h] * o[h], axis=-1, keepdims=True) + EPS) * dnw_ref[...]
        y_ref[:, h * DN_DV:(h + 1) * DN_DV] = oh * _silu(z_ref[:, h * DN_DV:(h + 1) * DN_DV])

    xe_ref[0:halo, :] = xe_ref[c:c + halo, :]

    @pl.when(t == pl.num_programs(1) - 1)
    def _():
        s_out_ref[0] = s_ref[...]


def _gdn_prompt(qkv, z, ab, conv_w, alog_pad, dtb_pad, dn_norm, n_batch, seq):
    c = GDN_CHUNK
    nt = seq // c
    row_spec = lambda w: pl.BlockSpec((c, w), lambda b, t: (b * nt + t, 0))
    return pl.pallas_call(
        _gdn_prompt_kernel,
        grid=(n_batch, nt),
        in_specs=[row_spec(CONV_DIM), row_spec(DN_V_W), row_spec(LANES),
                  pl.BlockSpec((CONV_W, CONV_DIM), lambda b, t: (0, 0)),
                  pl.BlockSpec((1, LANES), lambda b, t: (0, 0)),
                  pl.BlockSpec((1, LANES), lambda b, t: (0, 0)),
                  pl.BlockSpec((1, DN_DV), lambda b, t: (0, 0))],
        out_specs=[row_spec(DN_V_W),
                   pl.BlockSpec((1, DN_HEADS, DN_DK, DN_DV), lambda b, t: (b, 0, 0, 0))],
        out_shape=[jax.ShapeDtypeStruct((n_batch * seq, DN_V_W), F32),
                   jax.ShapeDtypeStruct((n_batch, DN_HEADS, DN_DK, DN_DV), F32)],
        scratch_shapes=[pltpu.VMEM((c + SUBLANES, CONV_DIM), F32),
                        pltpu.VMEM((DN_HEADS, DN_DK, DN_DV), F32)],
        compiler_params=_cparams("parallel", "arbitrary"),
        name="gdn_prompt",
    )(qkv, z, ab, conv_w, alog_pad, dtb_pad, dn_norm)


def _kv_head_views(k2, v2, lo):
    k2r = pltpu.roll(k2, SWA_HD, axis=1)
    v2r = pltpu.roll(v2, SWA_HD, axis=1)
    k_lo = (jnp.where(lo, k2, 0.0), jnp.where(lo, k2r, 0.0))
    k_hi = (jnp.where(lo, 0.0, k2r), jnp.where(lo, 0.0, k2))
    v_dup = (jnp.where(lo, v2, v2r), jnp.where(lo, v2r, v2))
    return k_lo, k_hi, v_dup


def _sink_softmax_pv(s, sink, v):
    m = jnp.maximum(jnp.max(s, axis=-1, keepdims=True), sink)
    p = jnp.exp(s - m)
    den = jnp.sum(p, axis=-1, keepdims=True) + jnp.exp(sink - m)
    return _dot(p, v) / den


def _swa_prompt_kernel(sink_ref, q_ref, kvc_ref, kvp_ref, o_ref):
    n = pl.program_id(1)
    blk = SWA_BLOCK
    kv = jnp.concatenate([kvp_ref[...], kvc_ref[...]], axis=0)
    lo_k = lax.broadcasted_iota(jnp.int32, (2 * blk, LANES), 1) < SWA_HD
    k_lo, k_hi, v_dup = _kv_head_views(kv[:, :LANES], kv[:, LANES:], lo_k)
    qi = lax.broadcasted_iota(jnp.int32, (blk, 2 * blk), 0)
    kj = lax.broadcasted_iota(jnp.int32, (blk, 2 * blk), 1)
    dist = qi - kj + blk
    valid = (dist >= 0) & (dist < WINDOW) & ((kj >= blk) | (n > 0))
    lo_o = lax.broadcasted_iota(jnp.int32, (blk, LANES), 1) < SWA_HD
    scale = SWA_HD ** -0.5
    pairs_per_kv = SWA_Q_HEADS // SWA_KV_HEADS // 2
    for j in range(SWA_Q_HEADS // 2):
        qp = q_ref[:, j * LANES:(j + 1) * LANES].astype(BF16)
        kvh = j // pairs_per_kv
        halves = []
        for half, keys in enumerate((k_lo[kvh], k_hi[kvh])):
            s = jnp.where(valid, _dot_nt(qp, keys) * scale, MASK_VALUE)
            halves.append(_sink_softmax_pv(s, sink_ref[2 * j + half], v_dup[kvh]))
        o_ref[:, j * LANES:(j + 1) * LANES] = jnp.where(lo_o, halves[0], halves[1])


def _swa_prompt(q, kv, sinks, n_batch, seq):
    blk = SWA_BLOCK
    nb = seq // blk
    return pl.pallas_call(
        _swa_prompt_kernel,
        grid=(n_batch, nb),
        in_specs=[pl.BlockSpec(memory_space=pltpu.SMEM),
                  pl.BlockSpec((blk, SWA_Q_W), lambda b, n: (b * nb + n, 0)),
                  pl.BlockSpec((blk, 2 * SWA_KV_W), lambda b, n: (b * nb + n, 0)),
                  pl.BlockSpec((blk, 2 * SWA_KV_W), lambda b, n: (b * nb + jnp.maximum(n - 1, 0), 0))],
        out_specs=pl.BlockSpec((blk, SWA_Q_W), lambda b, n: (b * nb + n, 0)),
        out_shape=jax.ShapeDtypeStruct((n_batch * seq, SWA_Q_W), F32),
        compiler_params=_cparams("parallel", "arbitrary"),
        name="swa_prompt",
    )(sinks, q, kv, kv)


def _mix_kernel(x_ref, ydn_ref, ysw_ref, gates_ref, g_ref, wbd_ref, wbs_ref, wout_ref, o_ref):
    a = _dot(ydn_ref[...], wbd_ref[...])
    b = _dot(ysw_ref[...], wbs_ref[...])
    merged = (jax.nn.sigmoid(gates_ref[:, :D_MODEL]) * a
              + jax.nn.sigmoid(gates_ref[:, D_MODEL:]) * b)
    o_ref[...] = x_ref[...] + g_ref[0] * _dot(merged, wout_ref[...])


def _mix(x, y_dn, y_sw, gates, mod, rows, gate_piece, w_bd, w_bs, w_out, name):
    return pl.pallas_call(
        _mix_kernel,
        grid=rows.grid,
        in_specs=[rows.row_spec(D_MODEL), rows.row_spec(DN_V_W), rows.row_spec(SWA_Q_W),
                  rows.row_spec(2 * D_MODEL), rows.mod_spec(gate_piece),
                  _resident(w_bd.shape), _resident(w_bs.shape), _resident(w_out.shape)],
        out_specs=rows.row_spec(D_MODEL),
        out_shape=jax.ShapeDtypeStruct((rows.n_rows, D_MODEL), F32),
        compiler_params=_cparams("parallel"),
        name=name,
    )(x, y_dn, y_sw, gates, mod, w_bd, w_bs, w_out)


def _gdn_step_prep_kernel(qkv_ref, cs_ref, ab_ref, cw_ref, alog_ref, dtb_ref,
                          cs_out_ref, qt_ref, kt_ref, v_ref, dec_ref, beta_ref):
    nb = qkv_ref.shape[0]
    keep = (CONV_W - 1) * CONV_DIM
    cs_out_ref[:, 0:keep - CONV_DIM] = cs_ref[:, CONV_DIM:keep]
    cs_out_ref[:, keep - CONV_DIM:keep] = qkv_ref[...]
    ab = ab_ref[...]
    dec = jnp.exp(-jnp.exp(alog_ref[...]) * jax.nn.softplus(ab + dtb_ref[...]))
    beta = jax.nn.sigmoid(ab)

    def conv_silu(c0):
        y = qkv_ref[:, c0:c0 + LANES] * cw_ref[CONV_W - 1:CONV_W, c0:c0 + LANES]
        for j in range(CONV_W - 1):
            y = y + (cs_ref[:, j * CONV_DIM + c0:j * CONV_DIM + c0 + LANES]
                     * cw_ref[j:j + 1, c0:c0 + LANES])
        return _silu(y)

    for h in range(DN_HEADS):
        q = conv_silu(h * DN_DK)
        k = conv_silu(DN_QK_W + h * DN_DK)
        qn = q * lax.rsqrt(jnp.sum(q * q, axis=-1, keepdims=True) + EPS) * (DN_DK ** -0.5)
        kn = k * lax.rsqrt(jnp.sum(k * k, axis=-1, keepdims=True) + EPS)
        qt_ref[h] = qn.T
        kt_ref[h] = kn.T
        sl = slice(h * DN_DV, (h + 1) * DN_DV)
        v_ref[:, sl] = conv_silu(2 * DN_QK_W + h * DN_DV)
        dec_ref[:, sl] = jnp.broadcast_to(dec[:, h:h + 1], (nb, DN_DV))
        beta_ref[:, sl] = jnp.broadcast_to(beta[:, DN_HEADS + h:DN_HEADS + h + 1], (nb, DN_DV))


def _gdn_step_prep(qkv, conv_state, ab, conv_w, alog_pad, dtb_pad):
    nb = qkv.shape[0]
    keep = (CONV_W - 1) * CONV_DIM
    full = lambda shape: pl.BlockSpec(shape, lambda i: (0,) * len(shape))
    return pl.pallas_call(
        _gdn_step_prep_kernel,
        grid=(1,),
        in_specs=[full((nb, CONV_DIM)), full((nb, keep)), full((nb, LANES)),
                  full((CONV_W, CONV_DIM)), full((1, LANES)), full((1, LANES))],
        out_specs=[full((nb, keep)), full((DN_HEADS, DN_DK, nb)), full((DN_HEADS, DN_DK, nb)),
                   full((nb, DN_V_W)), full((nb, DN_V_W)), full((nb, DN_V_W))],
        out_shape=[jax.ShapeDtypeStruct((nb, keep), F32),
                   jax.ShapeDtypeStruct((DN_HEADS, DN_DK, nb), F32),
                   jax.ShapeDtypeStruct((DN_HEADS, DN_DK, nb), F32),
                   jax.ShapeDtypeStruct((nb, DN_V_W), F32),
                   jax.ShapeDtypeStruct((nb, DN_V_W), F32),
                   jax.ShapeDtypeStruct((nb, DN_V_W), F32)],
        compiler_params=_cparams("arbitrary"),
        name="gdn_step_prep",
    )(qkv, conv_state, ab, conv_w, alog_pad, dtb_pad)


def _gdn_step_kernel(s_ref, qt_ref, kt_ref, v_ref, dec_ref, beta_ref, z_ref, dnw_ref,
                     s_out_ref, y_ref, o_scr):
    nb = s_ref.shape[0]
    qt = qt_ref[0]
    kt = kt_ref[0]
    for b in range(nb):
        k_col = kt[:, b:b + 1]
        q_col = qt[:, b:b + 1]
        s1 = s_ref[b, 0] * dec_ref[b:b + 1, :]
        kv = jnp.sum(s1 * k_col, axis=0, keepdims=True)
        delta = (v_ref[b:b + 1, :] - kv) * beta_ref[b:b + 1, :]
        s2 = s1 + k_col * delta
        s_out_ref[b, 0] = s2
        o_scr[b:b + 1, :] = jnp.sum(s2 * q_col, axis=0, keepdims=True)
    o = o_scr[...]
    o = o * lax.rsqrt(jnp.mean(o * o, axis=-1, keepdims=True) + EPS) * dnw_ref[...]
    y_ref[...] = o * _silu(z_ref[...])


def _gdn_step(state, qt, kt, v, dec, beta, z, dn_norm):
    nb = state.shape[0]
    head_cols = pl.BlockSpec((nb, DN_DV), lambda h: (0, h))
    head_t = pl.BlockSpec((1, DN_DK, nb), lambda h: (h, 0, 0))
    s_spec = pl.BlockSpec((nb, 1, DN_DK, DN_DV), lambda h: (0, h, 0, 0))
    return pl.pallas_call(
        _gdn_step_kernel,
        grid=(DN_HEADS,),
        in_specs=[s_spec, head_t, head_t, head_cols, head_cols, head_cols, head_cols,
                  pl.BlockSpec((1, DN_DV), lambda h: (0, 0))],
        out_specs=[s_spec, head_cols],
        out_shape=[jax.ShapeDtypeStruct(state.shape, F32),
                   jax.ShapeDtypeStruct((nb, DN_V_W), F32)],
        scratch_shapes=[pltpu.VMEM((nb, DN_DV), F32)],
        compiler_params=_cparams("parallel"),
        name="gdn_step",
    )(state, qt, kt, v, dec, beta, z, dn_norm)


SWA_STEP_BATCH = 8


def _swa_step_kernel(q_ref, kvn_ref, ck_ref, cv_ref, slo_ref, shi_ref, o_ref, ck_out_ref, cv_out_ref):
    length = ck_ref.shape[1]
    last = lax.broadcasted_iota(jnp.int32, (length, LANES), 0) == length - 1
    lo_k = lax.broadcasted_iota(jnp.int32, (length, LANES), 1) < SWA_HD
    pairs = SWA_Q_HEADS // 2
    first_kv = lax.broadcasted_iota(jnp.int32, (pairs, LANES), 0) < pairs // SWA_KV_HEADS
    lo_o = lax.broadcasted_iota(jnp.int32, (pairs, LANES), 1) < SWA_HD
    scale = SWA_HD ** -0.5
    for b in range(q_ref.shape[0]):
        k2 = jnp.where(last, kvn_ref[b:b + 1, 0:LANES], pltpu.roll(ck_ref[b], length - 1, axis=0))
        v2 = jnp.where(last, kvn_ref[b:b + 1, LANES:2 * LANES], pltpu.roll(cv_ref[b], length - 1, axis=0))
        ck_out_ref[b] = k2
        cv_out_ref[b] = v2
        k_lo, k_hi, v_dup = _kv_head_views(k2, v2, lo_k)
        q8 = q_ref[b].astype(BF16)
        halves = []
        for keys, sink_ref in ((k_lo, slo_ref), (k_hi, shi_ref)):
            s = jnp.where(first_kv, _dot_nt(q8, keys[0]), _dot_nt(q8, keys[1])) * scale
            sink = sink_ref[...]
            m = jnp.maximum(jnp.max(s, axis=-1, keepdims=True), sink)
            p = jnp.exp(s - m)
            den = jnp.sum(p, axis=-1, keepdims=True) + jnp.exp(sink - m)
            halves.append(jnp.where(first_kv, _dot(p, v_dup[0]), _dot(p, v_dup[1])) / den)
        o_ref[b] = jnp.where(lo_o, halves[0], halves[1])


def _swa_step(q3, kv_new, cache_k, cache_v, sink_lo, sink_hi):
    nb, length, _ = cache_k.shape
    tb = SWA_STEP_BATCH
    pairs = SWA_Q_HEADS // 2
    q_spec = pl.BlockSpec((tb, pairs, LANES), lambda i: (i, 0, 0))
    c_spec = pl.BlockSpec((tb, length, LANES), lambda i: (i, 0, 0))
    sink_spec = pl.BlockSpec((pairs, LANES), lambda i: (0, 0))
    return pl.pallas_call(
        _swa_step_kernel,
        grid=(nb // tb,),
        in_specs=[q_spec, pl.BlockSpec((tb, 2 * SWA_KV_W), lambda i: (i, 0)), c_spec, c_spec,
                  sink_spec, sink_spec],
        out_specs=[q_spec, c_spec, c_spec],
        out_shape=[jax.ShapeDtypeStruct(q3.shape, F32),
                   jax.ShapeDtypeStruct(cache_k.shape, F32),
                   jax.ShapeDtypeStruct(cache_v.shape, F32)],
        compiler_params=_cparams("parallel"),
        name="swa_step",
    )(q3, kv_new, cache_k, cache_v, sink_lo, sink_hi)


def _prep_layer_weights(lp):
    offs = [0]
    for wdt in IN_SPLITS:
        offs.append(offs[-1] + wdt)
    w_in = lp['w_in']
    piece = lambda i: w_in[:, offs[i]:offs[i + 1]]
    w_main = jnp.concatenate([piece(0), piece(1), piece(4), piece(5), piece(6), piece(7), piece(8)],
                             axis=1).astype(BF16)
    w_ab = jnp.pad(jnp.concatenate([piece(2), piece(3)], axis=1), ((0, 0), (0, LANES - 2 * DN_HEADS)))
    pad_heads = lambda a: jnp.pad(a.astype(F32), (0, LANES - DN_HEADS)).reshape(1, LANES)
    sinks = lp['sinks'].astype(F32)
    pairs = SWA_Q_HEADS // 2
    return dict(
        w13_ffn1=lp['w13_ffn1'].astype(BF16), w2_ffn1=lp['w2_ffn1'].astype(BF16),
        w13_ffn2=lp['w13_ffn2'].astype(BF16), w2_ffn2=lp['w2_ffn2'].astype(BF16),
        w_main=w_main, w_ab=w_ab,
        w_br_dn=lp['w_br_dn'].astype(BF16), w_br_swa=lp['w_br_swa'].astype(BF16),
        w_out=lp['w_out'].astype(BF16),
        norm_ffn1=lp['norm_ffn1'].reshape(1, D_MODEL), norm_mix=lp['norm_mix'].reshape(1, D_MODEL),
        norm_ffn2=lp['norm_ffn2'].reshape(1, D_MODEL),
        conv_w=lp['conv_w'], alog_pad=pad_heads(lp['a_log']), dtb_pad=pad_heads(lp['dt_bias']),
        dn_norm=lp['dn_norm'].reshape(1, DN_DV), sinks=sinks,
        sink_lo=jnp.broadcast_to(sinks.reshape(pairs, 2)[:, 0:1], (pairs, LANES)),
        sink_hi=jnp.broadcast_to(sinks.reshape(pairs, 2)[:, 1:2], (pairs, LANES)),
    )


def _layer(x, mod, rows, w, final_w, final, past, n_batch, seq, tag):
    x = _ffn(x, mod, rows, (0, 1, 2), w['norm_ffn1'], w['w13_ffn1'], w['w2_ffn1'], final_w,
             False, "ffn1_" + tag)
    in_rows = rows if past is not None else _Rows(n_batch, seq, INPROJ_ROW_TILE)
    qkv, z, q_sw, kv, gates, ab = _inproj(x, mod, in_rows, (3, 4), w['norm_mix'], w['w_main'], w['w_ab'],
                                         "inproj_" + tag)
    if past is None:
        y_dn, s_new = _gdn_prompt(qkv, z, ab, w['conv_w'], w['alog_pad'], w['dtb_pad'], w['dn_norm'],
                                  n_batch, seq)
        y_sw = _swa_prompt(q_sw, kv, w['sinks'], n_batch, seq)
        conv_new = qkv.reshape(n_batch, seq, CONV_DIM)[:, seq - (CONV_W - 1):]
        keep = min(WINDOW, seq)
        kv3 = kv.reshape(n_batch, seq, 2 * SWA_KV_W)[:, seq - keep:]
        k_buf = kv3[:, :, :SWA_KV_W].reshape(n_batch, keep, SWA_KV_HEADS, SWA_HD)
        v_buf = kv3[:, :, SWA_KV_W:].reshape(n_batch, keep, SWA_KV_HEADS, SWA_HD)
    else:
        s0, conv_buf, k_old, v_old = past
        length = k_old.shape[1]
        conv_new, qt, kt, v, dec, beta = _gdn_step_prep(
            qkv, conv_buf.reshape(n_batch, (CONV_W - 1) * CONV_DIM), ab, w['conv_w'],
            w['alog_pad'], w['dtb_pad'])
        s_new, y_dn = _gdn_step(s0, qt, kt, v, dec, beta, z, w['dn_norm'])
        conv_new = conv_new.reshape(n_batch, CONV_W - 1, CONV_DIM)
        o3, k_buf, v_buf = _swa_step(q_sw.reshape(n_batch, SWA_Q_HEADS // 2, LANES), kv,
                                     k_old.reshape(n_batch, length, SWA_KV_W),
                                     v_old.reshape(n_batch, length, SWA_KV_W),
                                     w['sink_lo'], w['sink_hi'])
        y_sw = o3.reshape(n_batch, SWA_Q_W)
        k_buf = k_buf.reshape(n_batch, length, SWA_KV_HEADS, SWA_HD)
        v_buf = v_buf.reshape(n_batch, length, SWA_KV_HEADS, SWA_HD)
    x = _mix(x, y_dn, y_sw, gates, mod, rows, 5, w['w_br_dn'], w['w_br_swa'], w['w_out'], "mix_" + tag)
    x = _ffn(x, mod, rows, (6, 7, 8), w['norm_ffn2'], w['w13_ffn2'], w['w2_ffn2'], final_w,
             final, "ffn2_" + tag)
    return x, (s_new, conv_new, k_buf, v_buf)


def kernel(x_prompt, x_sample, state_dn, state_conv, cache_swa_k, cache_swa_v, c_prompt, c_sample,
           w_ada, b_ada, norm_ffn1, w13_ffn1, w2_ffn1, norm_mix, w_in, conv_w, a_log, dt_bias,
           dn_norm, sinks, w_br_dn, w_br_swa, w_out, norm_ffn2, w13_ffn2, w2_ffn2, final_norm):
    n_p, seq_p, d = x_prompt.shape
    n_s, seq_s, _ = x_sample.shape
    depth = w_ada.shape[0]
    assert d == D_MODEL and seq_s == 1 and seq_p % ROW_TILE == 0 and seq_p % GDN_CHUNK == 0
    assert w_in.shape[2] == sum(IN_SPLITS) and w13_ffn1.shape[2] == 2 * D_FF
    assert cache_swa_k.shape[2] == WINDOW and n_s % SWA_STEP_BATCH == 0 and n_p % SUBLANES == 0
    rows_p = _Rows(n_p, seq_p, ROW_TILE)
    rows_s = _Rows(n_s, seq_s, ROW_TILE)
    c_all = jnp.concatenate([c_prompt, c_sample], axis=0)
    final_w = final_norm.reshape(1, D_MODEL)
    y_p = x_prompt.reshape(n_p * seq_p, d)
    y_s = x_sample.reshape(n_s * seq_s, d)
    st_p, st_s = [], []
    for l in range(depth):
        lp = dict(w_ada=w_ada[l], b_ada=b_ada[l], norm_ffn1=norm_ffn1[l], w13_ffn1=w13_ffn1[l],
                  w2_ffn1=w2_ffn1[l], norm_mix=norm_mix[l], w_in=w_in[l], conv_w=conv_w[l],
                  a_log=a_log[l], dt_bias=dt_bias[l], dn_norm=dn_norm[l], sinks=sinks[l],
                  w_br_dn=w_br_dn[l], w_br_swa=w_br_swa[l], w_out=w_out[l], norm_ffn2=norm_ffn2[l],
                  w13_ffn2=w13_ffn2[l], w2_ffn2=w2_ffn2[l])
        w = _prep_layer_weights(lp)
        mod_p, mod_s = _ada(c_all, lp['w_ada'], lp['b_ada'], n_p)
        final = l == depth - 1
        y_p, sp = _layer(y_p, mod_p.reshape(rows_p.mod_shape), rows_p, w, final_w, final, None,
                         n_p, seq_p, "prompt")
        y_s, ss = _layer(y_s, mod_s.reshape(rows_s.mod_shape), rows_s, w, final_w, final,
                         (state_dn[l], state_conv[l], cache_swa_k[l], cache_swa_v[l]),
                         n_s, seq_s, "sample")
        st_p.append(sp)
        st_s.append(ss)
    stack = lambda sts, i: sts[0][i][None] if depth == 1 else jnp.stack([s[i] for s in sts])
    return (y_p.reshape(n_p, seq_p, d), y_s.reshape(n_s, seq_s, d),
            stack(st_p, 0), stack(st_s, 0), stack(st_p, 1), stack(st_s, 1),
            stack(st_p, 2), stack(st_s, 2), stack(st_p, 3), stack(st_s, 3))
```

```python
import functools

import jax
import jax.numpy as jnp
from jax import lax
from jax.experimental import pallas as pl
from jax.experimental.pallas import tpu as pltpu

F32 = jnp.float32
BF16 = jnp.bfloat16

D_MODEL = 1024
DN_HEADS = 8
DN_DK = 128
DN_DV = 128
DN_QK_W = DN_HEADS * DN_DK
DN_V_W = DN_HEADS * DN_DV
CONV_W = 4
CONV_DIM = 2 * DN_QK_W + DN_V_W
SWA_Q_HEADS = 16
SWA_KV_HEADS = 2
SWA_HD = 64
SWA_Q_W = SWA_Q_HEADS * SWA_HD
SWA_KV_W = SWA_KV_HEADS * SWA_HD
WINDOW = 128
D_FF = 2816
HALF_STEP = 0.5
N_MOD = 9
EPS = 1e-6
MASK_VALUE = -1e30
IN_SPLITS = (CONV_DIM, DN_V_W, DN_HEADS, DN_HEADS, SWA_Q_W, SWA_KV_W, SWA_KV_W, D_MODEL, D_MODEL)

LANES = 128
SUBLANES = 8
VMEM_LIMIT_BYTES = 56 * 1024 * 1024

GDN_CHUNK = 128
SWA_BLOCK = 128
FF_CHUNK = 256
ROW_TILE = 512
INPROJ_ROW_TILE = 256
IN_COL_CHUNK = 512


def _cparams(*sem):
    return pltpu.CompilerParams(dimension_semantics=sem, vmem_limit_bytes=VMEM_LIMIT_BYTES)


def _resident(shape):
    nd = len(shape)
    return pl.BlockSpec(shape, lambda *_: (0,) * nd, pipeline_mode=pl.Buffered(1))


def _dot(a, b):
    return jnp.dot(a.astype(BF16), b.astype(BF16), preferred_element_type=F32)


def _dot_nt(a, b):
    return lax.dot_general(a.astype(BF16), b.astype(BF16), (((1,), (1,)), ((), ())),
                           preferred_element_type=F32)


def _dot_tn(a, b):
    return lax.dot_general(a.astype(BF16), b.astype(BF16), (((0,), (0,)), ((), ())),
                           preferred_element_type=F32)


def _split3(a):
    hi = a.astype(BF16)
    r = a - hi.astype(F32)
    mid = r.astype(BF16)
    lo = (r - mid.astype(F32)).astype(BF16)
    return hi, mid, lo


def _dot3(a, b):
    a_hi = a.astype(BF16)
    a_lo = (a - a_hi.astype(F32)).astype(BF16)
    b_hi = b.astype(BF16)
    b_lo = (b - b_hi.astype(F32)).astype(BF16)
    d = functools.partial(jnp.dot, preferred_element_type=F32)
    return d(a_hi, b_hi) + (d(a_lo, b_hi) + d(a_hi, b_lo))


def _silu(x):
    return x * jax.nn.sigmoid(x)


def _rms_mod(x, nw, sc, sh):
    y = x * lax.rsqrt(jnp.mean(x * x, axis=-1, keepdims=True) + EPS)
    return (y * nw) * (1.0 + sc) + sh


def _ada_kernel(c_ref, w_ref, b_ref, op_ref, os_ref, *, n_prompt):
    m = _dot3(_silu(c_ref[...]), w_ref[...]) + b_ref[...]
    op_ref[...] = m[:n_prompt]
    os_ref[...] = m[n_prompt:]


def _ada(c_all, w_ada, b_ada, n_prompt):
    n_all, d = c_all.shape
    n_out = w_ada.shape[1]
    tn = D_MODEL
    return pl.pallas_call(
        functools.partial(_ada_kernel, n_prompt=n_prompt),
        grid=(n_out // tn,),
        in_specs=[pl.BlockSpec((n_all, d), lambda j: (0, 0)),
                  pl.BlockSpec((d, tn), lambda j: (0, j)),
                  pl.BlockSpec((1, tn), lambda j: (0, j))],
        out_specs=[pl.BlockSpec((n_prompt, tn), lambda j: (0, j)),
                   pl.BlockSpec((n_all - n_prompt, tn), lambda j: (0, j))],
        out_shape=[jax.ShapeDtypeStruct((n_prompt, n_out), F32),
                   jax.ShapeDtypeStruct((n_all - n_prompt, n_out), F32)],
        compiler_params=_cparams("arbitrary"),
        name="ada",
    )(c_all, w_ada, b_ada.reshape(1, n_out))


class _Rows:
    def __init__(self, n_batch, seq, row_tile):
        self.n_batch = n_batch
        self.n_rows = n_batch * seq
        if seq == 1:
            self.tm = n_batch
            self.mod_shape = (1, n_batch, N_MOD * D_MODEL)
            self.mod_block = (1, n_batch, D_MODEL)
            self.tiles_per_batch = None
        else:
            self.tm = min(row_tile, seq)
            assert seq % self.tm == 0
            self.mod_shape = (n_batch, 1, N_MOD * D_MODEL)
            self.mod_block = (1, 1, D_MODEL)
            self.tiles_per_batch = seq // self.tm
        self.grid = (self.n_rows // self.tm,)

    def mod_spec(self, piece):
        if self.tiles_per_batch is None:
            return pl.BlockSpec(self.mod_block, lambda i: (0, 0, piece))
        tpb = self.tiles_per_batch
        return pl.BlockSpec(self.mod_block, lambda i: (i // tpb, 0, piece))

    def row_spec(self, width):
        return pl.BlockSpec((self.tm, width), lambda i: (i, 0))


def _ffn_half_step(x, nw, sc, sh, g, w13_ref, w2_ref):
    h = _rms_mod(x, nw, sc, sh).astype(BF16)
    acc = jnp.zeros(x.shape, F32)
    for c0 in range(0, D_FF, FF_CHUNK):
        gate = jnp.dot(h, w13_ref[:, c0:c0 + FF_CHUNK], preferred_element_type=F32)
        up = jnp.dot(h, w13_ref[:, D_FF + c0:D_FF + c0 + FF_CHUNK], preferred_element_type=F32)
        act = (_silu(gate) * up).astype(BF16)
        acc = acc + jnp.dot(act, w2_ref[c0:c0 + FF_CHUNK, :], preferred_element_type=F32)
    return x + (HALF_STEP * g) * acc


def _ffn_kernel(x_ref, sh_ref, sc_ref, g_ref, nw_ref, w13_ref, w2_ref, o_ref):
    o_ref[...] = _ffn_half_step(x_ref[...], nw_ref[...], sc_ref[0], sh_ref[0], g_ref[0], w13_ref, w2_ref)


def _ffn(x, mod, rows, pieces, nw, w13, w2, name):
    sh, sc, g = pieces
    return pl.pallas_call(
        _ffn_kernel,
        grid=rows.grid,
        in_specs=[rows.row_spec(D_MODEL), rows.mod_spec(sh), rows.mod_spec(sc), rows.mod_spec(g),
                  _resident((1, D_MODEL)), _resident(w13.shape), _resident(w2.shape)],
        out_specs=rows.row_spec(D_MODEL),
        out_shape=jax.ShapeDtypeStruct((rows.n_rows, D_MODEL), F32),
        compiler_params=_cparams("parallel"),
        name=name,
    )(x, mod, mod, mod, nw, w13, w2)


IN_A_PIECES = (CONV_DIM, DN_V_W)
IN_B_PIECES = (SWA_Q_W, 2 * SWA_KV_W, 2 * D_MODEL)
IN_OUT_DTYPES = (F32, BF16, BF16, F32, BF16, F32)


def _inproj_kernel(x_ref, sh_ref, sc_ref, nw_ref, wa_ref, wb_ref, wab_ref, cw_ref, *rest,
                   conv, tiles_per_batch):
    if conv:
        qkv_ref, z_ref, qsw_ref, kv_ref, gates_ref, ab_ref, tail_ref, xe_ref = rest
    else:
        qkv_ref, z_ref, qsw_ref, kv_ref, gates_ref, ab_ref = rest
    tm = x_ref.shape[0]
    halo = SUBLANES
    h = _rms_mod(x_ref[...], nw_ref[...], sc_ref[0], sh_ref[0])
    hb = h.astype(BF16)

    def project(w_ref, off, width, store):
        for c0 in range(0, width, IN_COL_CHUNK):
            cw = min(IN_COL_CHUNK, width - c0)
            store(c0, cw, jnp.dot(hb, w_ref[:, off + c0:off + c0 + cw], preferred_element_type=F32))

    def store_to(ref, col0=0):
        def store(c0, cw, val):
            ref[:, col0 + c0:col0 + c0 + cw] = val.astype(ref.dtype)
        return store

    if conv:
        @pl.when(pl.program_id(0) % tiles_per_batch == 0)
        def _():
            xe_ref[...] = jnp.zeros(xe_ref.shape, F32)

        def conv_silu(c0, raw):
            cols = slice(c0, c0 + raw.shape[1])
            ext = jnp.concatenate([xe_ref[:, cols], raw], axis=0)
            y = raw * cw_ref[CONV_W - 1:CONV_W, cols]
            for j in range(CONV_W - 1):
                lag = CONV_W - 1 - j
                y = y + ext[halo - lag:halo - lag + tm] * cw_ref[j:j + 1, cols]
            qkv_ref[:, cols] = _silu(y)
            xe_ref[:, cols] = raw[tm - halo:]

    others = []

    def defer(w_ref, off, width, ref):
        for c0 in range(0, width, IN_COL_CHUNK):
            cw = min(IN_COL_CHUNK, width - c0)
            others.append(functools.partial(project, w_ref, off + c0, cw, store_to(ref, col0=c0)))

    defer(wa_ref, CONV_DIM, DN_V_W, z_ref)
    off = 0
    for ref, width in zip((qsw_ref, kv_ref, gates_ref), IN_B_PIECES):
        defer(wb_ref, off, width, ref)
        off += width

    if conv:
        chunks = list(range(0, CONV_DIM, IN_COL_CHUNK))
        mm = lambda c0: jnp.dot(hb, wa_ref[:, c0:c0 + IN_COL_CHUNK], preferred_element_type=F32)
        raw = mm(chunks[0])
        for k, c0 in enumerate(chunks):
            nxt = mm(chunks[k + 1]) if k + 1 < len(chunks) else None
            for g in range(0, IN_COL_CHUNK, LANES):
                conv_silu(c0 + g, raw[:, g:g + LANES])
                if g % (2 * LANES) == 0 and others:
                    others.pop(0)()
            raw = nxt
        tail_ref[0] = xe_ref[...]
    else:
        project(wa_ref, 0, CONV_DIM, store_to(qkv_ref))
    for thunk in others:
        thunk()
    ab_ref[...] = _dot3(h, wab_ref[...])


def _inproj(x, mod, rows, pieces, nw, w_a, w_b, w_ab, conv_w, conv, name):
    sh, sc = pieces
    widths = IN_A_PIECES + IN_B_PIECES + (LANES,)
    out_specs = [rows.row_spec(w) for w in widths]
    out_shape = [jax.ShapeDtypeStruct((rows.n_rows, w), dt) for w, dt in zip(widths, IN_OUT_DTYPES)]
    scratch = []
    if conv:
        tpb = rows.tiles_per_batch
        out_specs.append(pl.BlockSpec((1, SUBLANES, CONV_DIM), lambda i: (i // tpb, 0, 0)))
        out_shape.append(jax.ShapeDtypeStruct((rows.n_batch, SUBLANES, CONV_DIM), F32))
        scratch.append(pltpu.VMEM((SUBLANES, CONV_DIM), F32))
    return pl.pallas_call(
        functools.partial(_inproj_kernel, conv=conv, tiles_per_batch=rows.tiles_per_batch),
        grid=rows.grid,
        in_specs=[rows.row_spec(D_MODEL), rows.mod_spec(sh), rows.mod_spec(sc),
                  _resident((1, D_MODEL)), _resident(w_a.shape), _resident(w_b.shape),
                  _resident(w_ab.shape), _resident(conv_w.shape)],
        out_specs=out_specs,
        out_shape=out_shape,
        scratch_shapes=scratch,
        compiler_params=_cparams("arbitrary"),
        name=name,
    )(x, mod, mod, nw, w_a, w_b, w_ab, conv_w)


def _unit_lower_inverse(mats, row, col):
    n = mats[0].shape[0]
    eye = jnp.where(row == col, 1.0, 0.0).astype(F32)
    in_block = (row >> 4) == (col >> 4)
    p = [jnp.where(in_block, -a, 0.0) for a in mats]
    r = [eye + pi for pi in p]
    q = [_dot(pi, pi) for pi in p]
    for _ in range(2):
        rq = [_dot(qi, jnp.concatenate([ri, qi], axis=1)) for qi, ri in zip(q, r)]
        r = [ri + rqi[:, :n] for ri, rqi in zip(r, rq)]
        q = [rqi[:, n:] for rqi in rq]
    x = [ri + _dot(qi, ri) for qi, ri in zip(q, r)]
    s = 4
    while (1 << s) < n:
        pair = ((row >> (s + 1)) == (col >> (s + 1))) & ((row >> s) > (col >> s))
        t = [_dot(jnp.where(pair, a, 0.0), xi) for a, xi in zip(mats, x)]
        x = [xi - _dot(xi, ti) for xi, ti in zip(x, t)]
        s += 1
    return x


def _gdn_prompt_kernel(qkv_ref, z_ref, ab_ref, alog_ref, dtb_ref, dnw_ref, y_ref, s_out_ref, s_ref):
    t = pl.program_id(1)
    c = GDN_CHUNK
    heads = range(DN_HEADS)

    @pl.when(t == 0)
    def _():
        s_ref[...] = jnp.zeros(s_ref.shape, F32)

    row = lax.broadcasted_iota(jnp.int32, (c, c), 0)
    col = lax.broadcasted_iota(jnp.int32, (c, c), 1)
    lower = row >= col
    strict = row > col

    ab = ab_ref[...]
    g_log = -jnp.exp(alog_ref[...]) * jax.nn.softplus(ab + dtb_ref[...])
    beta_all = jax.nn.sigmoid(ab)
    tri = jnp.where(lower, 1.0, 0.0).astype(BF16)
    gc = sum(jnp.dot(tri, piece, preferred_element_type=F32) for piece in _split3(g_log))
    gc_t = gc.T
    gc_last = gc[c - 1:c, :]

    def l2n(a):
        return a * lax.rsqrt(jnp.sum(a * a, axis=-1, keepdims=True) + EPS)

    def cols(c0):
        return qkv_ref[:, c0:c0 + LANES]

    qn = [l2n(cols(h * DN_DK)) * (DN_DK ** -0.5) for h in heads]
    kn = [l2n(cols(DN_QK_W + h * DN_DK)) for h in heads]
    v = [cols(2 * DN_QK_W + h * DN_DV) for h in heads]
    g_col = [gc[:, h:h + 1] for h in heads]
    g_end = [gc_last[:, h:h + 1] for h in heads]
    beta = [beta_all[:, DN_HEADS + h:DN_HEADS + h + 1] for h in heads]
    decay = [jnp.where(lower, jnp.exp(jnp.where(lower, g_col[h] - gc_t[h:h + 1, :], 0.0)), 0.0)
             for h in heads]
    kb = [kn[h] * beta[h] for h in heads]
    e_col = [jnp.exp(g_col[h]) for h in heads]

    kq = [_dot_nt(jnp.concatenate([kb[h], qn[h]], axis=0), kn[h]) for h in heads]
    a_mat = [jnp.where(strict, kq[h][:c] * decay[h], 0.0) for h in heads]
    qk = [kq[h][c:] * decay[h] for h in heads]
    x_inv = _unit_lower_inverse(a_mat, row, col)
    uw = [_dot(x_inv[h], jnp.concatenate([v[h] * beta[h], kb[h] * e_col[h]], axis=1)) for h in heads]
    s_old = [s_ref[h] for h in heads]
    ws = [_dot(jnp.concatenate([uw[h][:, DN_DV:], qn[h] * e_col[h]], axis=0), s_old[h]) for h in heads]
    v_new = [uw[h][:, :DN_DV] - ws[h][:c] for h in heads]
    o = [ws[h][c:] + _dot(qk[h], v_new[h]) for h in heads]
    for h in heads:
        k_dec = kn[h] * jnp.exp(g_end[h] - g_col[h])
        s_ref[h] = s_old[h] * jnp.exp(g_end[h]) + _dot_tn(k_dec, v_new[h])
    for h in heads:
        oh = o[h] * lax.rsqrt(jnp.mean(o[h] * o[h], axis=-1, keepdims=True) + EPS) * dnw_ref[...]
        zh = z_ref[:, h * DN_DV:(h + 1) * DN_DV].astype(F32)
        y_ref[:, h * DN_DV:(h + 1) * DN_DV] = (oh * _silu(zh)).astype(y_ref.dtype)

    @pl.when(t == pl.num_programs(1) - 1)
    def _():
        s_out_ref[0] = s_ref[...]


def _gdn_prompt(qkv, z, ab, alog_pad, dtb_pad, dn_norm, n_batch, seq):
    c = GDN_CHUNK
    nt = seq // c
    row_spec = lambda w: pl.BlockSpec((c, w), lambda b, t: (b * nt + t, 0))
    return pl.pallas_call(
        _gdn_prompt_kernel,
        grid=(n_batch, nt),
        in_specs=[row_spec(CONV_DIM), row_spec(DN_V_W), row_spec(LANES),
                  pl.BlockSpec((1, LANES), lambda b, t: (0, 0)),
                  pl.BlockSpec((1, LANES), lambda b, t: (0, 0)),
                  pl.BlockSpec((1, DN_DV), lambda b, t: (0, 0))],
        out_specs=[row_spec(DN_V_W),
                   pl.BlockSpec((1, DN_HEADS, DN_DK, DN_DV), lambda b, t: (b, 0, 0, 0))],
        out_shape=[jax.ShapeDtypeStruct((n_batch * seq, DN_V_W), BF16),
                   jax.ShapeDtypeStruct((n_batch, DN_HEADS, DN_DK, DN_DV), F32)],
        scratch_shapes=[pltpu.VMEM((DN_HEADS, DN_DK, DN_DV), F32)],
        compiler_params=_cparams("parallel", "arbitrary"),
        name="gdn_prompt",
    )(qkv, z, ab, alog_pad, dtb_pad, dn_norm)


def _kv_head_views(k2, v2, lo):
    k2r = pltpu.roll(k2, SWA_HD, axis=1)
    v2r = pltpu.roll(v2, SWA_HD, axis=1)
    k_lo = (jnp.where(lo, k2, 0.0), jnp.where(lo, k2r, 0.0))
    k_hi = (jnp.where(lo, 0.0, k2r), jnp.where(lo, 0.0, k2))
    v_dup = (jnp.where(lo, v2, v2r), jnp.where(lo, v2r, v2))
    return k_lo, k_hi, v_dup


def _swa_prompt_kernel(sink_ref, q_ref, kvc_ref, kvp_ref, o_ref):
    n = pl.program_id(1)
    blk = SWA_BLOCK
    lo = lax.broadcasted_iota(jnp.int32, (blk, LANES), 1) < SWA_HD
    kc_lo, kc_hi, vc_dup = _kv_head_views(kvc_ref[:, :LANES], kvc_ref[:, LANES:], lo)
    kp_lo, kp_hi, vp_dup = _kv_head_views(kvp_ref[:, :LANES], kvp_ref[:, LANES:], lo)
    own = (lax.broadcasted_iota(jnp.int32, (blk, blk), 0)
           >= lax.broadcasted_iota(jnp.int32, (blk, blk), 1))
    has_prev = n > 0
    pairs_per_kv = SWA_Q_HEADS // SWA_KV_HEADS // 2
    heads = [(j, half) for j in range(SWA_Q_HEADS // 2) for half in range(2)]
    keys_c, keys_p = (kc_lo, kc_hi), (kp_lo, kp_hi)
    qp = [q_ref[:, j * LANES:(j + 1) * LANES] * (SWA_HD ** -0.5)
          for j in range(SWA_Q_HEADS // 2)]
    s_own = [_dot_nt(qp[j], keys_c[half][j // pairs_per_kv]) for j, half in heads]
    s_prev = [_dot_nt(qp[j], keys_p[half][j // pairs_per_kv]) for j, half in heads]
    p, den = [], []
    for i, (j, half) in enumerate(heads):
        s = jnp.where(own, s_own[i], jnp.where(has_prev, s_prev[i], MASK_VALUE))
        sink = sink_ref[2 * j + half]
        m = jnp.maximum(jnp.max(s, axis=-1, keepdims=True), sink)
        e = jnp.exp(s - m)
        p.append(e)
        den.append(jnp.sum(e, axis=-1, keepdims=True) + jnp.exp(sink - m))
    pv_own = [_dot(jnp.where(own, p[i], 0.0), vc_dup[j // pairs_per_kv]) for i, (j, _) in enumerate(heads)]
    pv_prev = [_dot(jnp.where(own, 0.0, p[i]), vp_dup[j // pairs_per_kv]) for i, (j, _) in enumerate(heads)]
    out = [(pv_own[i] + pv_prev[i]) / den[i] for i in range(len(heads))]
    for j in range(SWA_Q_HEADS // 2):
        o_ref[:, j * LANES:(j + 1) * LANES] = jnp.where(lo, out[2 * j], out[2 * j + 1]).astype(o_ref.dtype)


def _swa_prompt(q, kv, sinks, n_batch, seq):
    blk = SWA_BLOCK
    nb = seq // blk
    return pl.pallas_call(
        _swa_prompt_kernel,
        grid=(n_batch, nb),
        in_specs=[pl.BlockSpec(memory_space=pltpu.SMEM),
                  pl.BlockSpec((blk, SWA_Q_W), lambda b, n: (b * nb + n, 0)),
                  pl.BlockSpec((blk, 2 * SWA_KV_W), lambda b, n: (b * nb + n, 0)),
                  pl.BlockSpec((blk, 2 * SWA_KV_W), lambda b, n: (b * nb + jnp.maximum(n - 1, 0), 0))],
        out_specs=pl.BlockSpec((blk, SWA_Q_W), lambda b, n: (b * nb + n, 0)),
        out_shape=jax.ShapeDtypeStruct((n_batch * seq, SWA_Q_W), BF16),
        compiler_params=_cparams("parallel", "arbitrary"),
        name="swa_prompt",
    )(sinks, q, kv, kv)


def _mix_ffn_kernel(x_ref, ydn_ref, ysw_ref, gates_ref, g2_ref, sh_ref, sc_ref, g3_ref, nw_ref,
                    wbd_ref, wbs_ref, wout_ref, w13_ref, w2_ref, fw_ref, o_ref, *, final):
    a = _dot(ydn_ref[...], wbd_ref[...])
    b = _dot(ysw_ref[...], wbs_ref[...])
    merged = (jax.nn.sigmoid(gates_ref[:, :D_MODEL].astype(F32)) * a
              + jax.nn.sigmoid(gates_ref[:, D_MODEL:].astype(F32)) * b)
    x = x_ref[...] + g2_ref[0] * _dot(merged, wout_ref[...])
    y = _ffn_half_step(x, nw_ref[...], sc_ref[0], sh_ref[0], g3_ref[0], w13_ref, w2_ref)
    if final:
        y = y * lax.rsqrt(jnp.mean(y * y, axis=-1, keepdims=True) + EPS) * fw_ref[...]
    o_ref[...] = y


def _mix_ffn(x, y_dn, y_sw, gates, mod, rows, w, final_w, final, name):
    weights = (w['w_br_dn'], w['w_br_swa'], w['w_out'], w['w13_ffn2'], w['w2_ffn2'])
    return pl.pallas_call(
        functools.partial(_mix_ffn_kernel, final=final),
        grid=rows.grid,
        in_specs=[rows.row_spec(D_MODEL), rows.row_spec(DN_V_W), rows.row_spec(SWA_Q_W),
                  rows.row_spec(2 * D_MODEL), rows.mod_spec(5), rows.mod_spec(6), rows.mod_spec(7),
                  rows.mod_spec(8), _resident((1, D_MODEL))]
                 + [_resident(a.shape) for a in weights] + [_resident((1, D_MODEL))],
        out_specs=rows.row_spec(D_MODEL),
        out_shape=jax.ShapeDtypeStruct((rows.n_rows, D_MODEL), F32),
        compiler_params=_cparams("parallel"),
        name=name,
    )(x, y_dn, y_sw, gates, mod, mod, mod, mod, w['norm_ffn2'], *weights, final_w)


def _gdn_step_prep_kernel(qkv_ref, cs_ref, ab_ref, cw_ref, alog_ref, dtb_ref,
                          cs_out_ref, qt_ref, kt_ref, v_ref, dec_ref, beta_ref):
    nb = qkv_ref.shape[0]
    for j in range(CONV_W - 2):
        cs_out_ref[:, j, :] = cs_ref[:, j + 1, :]
    cs_out_ref[:, CONV_W - 2, :] = qkv_ref[...]
    ab = ab_ref[...]
    dec = jnp.exp(-jnp.exp(alog_ref[...]) * jax.nn.softplus(ab + dtb_ref[...]))
    beta = jax.nn.sigmoid(ab)

    def conv_silu(c0):
        cols = slice(c0, c0 + LANES)
        y = qkv_ref[:, cols] * cw_ref[CONV_W - 1:CONV_W, cols]
        for j in range(CONV_W - 1):
            y = y + cs_ref[:, j, cols] * cw_ref[j:j + 1, cols]
        return _silu(y)

    for h in range(DN_HEADS):
        q = conv_silu(h * DN_DK)
        k = conv_silu(DN_QK_W + h * DN_DK)
        qn = q * lax.rsqrt(jnp.sum(q * q, axis=-1, keepdims=True) + EPS) * (DN_DK ** -0.5)
        kn = k * lax.rsqrt(jnp.sum(k * k, axis=-1, keepdims=True) + EPS)
        qt_ref[h] = qn.T
        kt_ref[h] = kn.T
        sl = slice(h * DN_DV, (h + 1) * DN_DV)
        v_ref[:, sl] = conv_silu(2 * DN_QK_W + h * DN_DV)
        dec_ref[:, sl] = jnp.broadcast_to(dec[:, h:h + 1], (nb, DN_DV))
        beta_ref[:, sl] = jnp.broadcast_to(beta[:, DN_HEADS + h:DN_HEADS + h + 1], (nb, DN_DV))


def _gdn_step_prep(qkv, conv_state, ab, conv_w, alog_pad, dtb_pad):
    nb = qkv.shape[0]
    full = _resident
    return pl.pallas_call(
        _gdn_step_prep_kernel,
        grid=(1,),
        in_specs=[full((nb, CONV_DIM)), full(conv_state.shape), full((nb, LANES)),
                  full((CONV_W, CONV_DIM)), full((1, LANES)), full((1, LANES))],
        out_specs=[full(conv_state.shape), full((DN_HEADS, DN_DK, nb)), full((DN_HEADS, DN_DK, nb)),
                   full((nb, DN_V_W)), full((nb, DN_V_W)), full((nb, DN_V_W))],
        out_shape=[jax.ShapeDtypeStruct(conv_state.shape, F32),
                   jax.ShapeDtypeStruct((DN_HEADS, DN_DK, nb), F32),
                   jax.ShapeDtypeStruct((DN_HEADS, DN_DK, nb), F32),
                   jax.ShapeDtypeStruct((nb, DN_V_W), F32),
                   jax.ShapeDtypeStruct((nb, DN_V_W), F32),
                   jax.ShapeDtypeStruct((nb, DN_V_W), F32)],
        compiler_params=_cparams("arbitrary"),
        name="gdn_step_prep",
    )(qkv, conv_state, ab, conv_w, alog_pad, dtb_pad)


def _gdn_step_kernel(s_ref, qt_ref, kt_ref, v_ref, dec_ref, beta_ref, z_ref, dnw_ref,
                     s_out_ref, y_ref, o_scr):
    nb = s_ref.shape[0]
    qt = qt_ref[0]
    kt = kt_ref[0]
    for b in range(nb):
        k_col = kt[:, b:b + 1]
        q_col = qt[:, b:b + 1]
        s1 = s_ref[b, 0] * dec_ref[b:b + 1, :]
        kv = jnp.sum(s1 * k_col, axis=0, keepdims=True)
        delta = (v_ref[b:b + 1, :] - kv) * beta_ref[b:b + 1, :]
        s2 = s1 + k_col * delta
        s_out_ref[b, 0] = s2
        o_scr[b:b + 1, :] = jnp.sum(s2 * q_col, axis=0, keepdims=True)
    o = o_scr[...]
    o = o * lax.rsqrt(jnp.mean(o * o, axis=-1, keepdims=True) + EPS) * dnw_ref[...]
    y_ref[...] = (o * _silu(z_ref[...].astype(F32))).astype(y_ref.dtype)


def _gdn_step(state, qt, kt, v, dec, beta, z, dn_norm):
    nb = state.shape[0]
    head_cols = pl.BlockSpec((nb, DN_DV), lambda h: (0, h))
    head_t = pl.BlockSpec((1, DN_DK, nb), lambda h: (h, 0, 0))
    s_spec = pl.BlockSpec((nb, 1, DN_DK, DN_DV), lambda h: (0, h, 0, 0))
    return pl.pallas_call(
        _gdn_step_kernel,
        grid=(DN_HEADS,),
        in_specs=[s_spec, head_t, head_t, head_cols, head_cols, head_cols, head_cols,
                  pl.BlockSpec((1, DN_DV), lambda h: (0, 0))],
        out_specs=[s_spec, head_cols],
        out_shape=[jax.ShapeDtypeStruct(state.shape, F32),
                   jax.ShapeDtypeStruct((nb, DN_V_W), BF16)],
        scratch_shapes=[pltpu.VMEM((nb, DN_DV), F32)],
        compiler_params=_cparams("parallel"),
        name="gdn_step",
    )(state, qt, kt, v, dec, beta, z, dn_norm)


SWA_STEP_BATCH = 8


def _swa_step_kernel(q_ref, kvn_ref, ck_ref, cv_ref, slo_ref, shi_ref, o_ref, ck_out_ref, cv_out_ref):
    length = ck_ref.shape[1]
    last = lax.broadcasted_iota(jnp.int32, (length, LANES), 0) == length - 1
    lo_k = lax.broadcasted_iota(jnp.int32, (length, LANES), 1) < SWA_HD
    pairs = SWA_Q_HEADS // 2
    first_kv = lax.broadcasted_iota(jnp.int32, (pairs, LANES), 0) < pairs // SWA_KV_HEADS
    lo_o = lax.broadcasted_iota(jnp.int32, (pairs, LANES), 1) < SWA_HD
    scale = SWA_HD ** -0.5
    for b in range(q_ref.shape[0]):
        k2 = jnp.where(last, kvn_ref[b:b + 1, 0:LANES], pltpu.roll(ck_ref[b], length - 1, axis=0))
        v2 = jnp.where(last, kvn_ref[b:b + 1, LANES:2 * LANES], pltpu.roll(cv_ref[b], length - 1, axis=0))
        ck_out_ref[b] = k2
        cv_out_ref[b] = v2
        k_lo, k_hi, v_dup = _kv_head_views(k2, v2, lo_k)
        q8 = q_ref[b]
        halves = []
        for keys, sink_ref in ((k_lo, slo_ref), (k_hi, shi_ref)):
            s = jnp.where(first_kv, _dot_nt(q8, keys[0]), _dot_nt(q8, keys[1])) * scale
            sink = sink_ref[...]
            m = jnp.maximum(jnp.max(s, axis=-1, keepdims=True), sink)
            p = jnp.exp(s - m)
            den = jnp.sum(p, axis=-1, keepdims=True) + jnp.exp(sink - m)
            halves.append(jnp.where(first_kv, _dot(p, v_dup[0]), _dot(p, v_dup[1])) / den)
        o_ref[b] = jnp.where(lo_o, halves[0], halves[1]).astype(o_ref.dtype)


def _swa_step(q3, kv_new, cache_k, cache_v, sink_lo, sink_hi):
    nb, length, _ = cache_k.shape
    tb = SWA_STEP_BATCH
    pairs = SWA_Q_HEADS // 2
    q_spec = pl.BlockSpec((tb, pairs, LANES), lambda i: (i, 0, 0))
    c_spec = pl.BlockSpec((tb, length, LANES), lambda i: (i, 0, 0))
    sink_spec = pl.BlockSpec((pairs, LANES), lambda i: (0, 0))
    return pl.pallas_call(
        _swa_step_kernel,
        grid=(nb // tb,),
        in_specs=[q_spec, pl.BlockSpec((tb, 2 * SWA_KV_W), lambda i: (i, 0)), c_spec, c_spec,
                  sink_spec, sink_spec],
        out_specs=[q_spec, c_spec, c_spec],
        out_shape=[jax.ShapeDtypeStruct(q3.shape, BF16),
                   jax.ShapeDtypeStruct(cache_k.shape, F32),
                   jax.ShapeDtypeStruct(cache_v.shape, F32)],
        compiler_params=_cparams("parallel"),
        name="swa_step",
    )(q3, kv_new, cache_k, cache_v, sink_lo, sink_hi)


def _prep_layer_weights(lp):
    offs = [0]
    for wdt in IN_SPLITS:
        offs.append(offs[-1] + wdt)
    w_in = lp['w_in']
    w_ab = jnp.pad(w_in[:, offs[2]:offs[4]], ((0, 0), (0, LANES - 2 * DN_HEADS)))
    pad_heads = lambda a: jnp.pad(a.astype(F32), (0, LANES - DN_HEADS)).reshape(1, LANES)
    sinks = lp['sinks'].astype(F32)
    pairs = SWA_Q_HEADS // 2
    return dict(
        w13_ffn1=lp['w13_ffn1'].astype(BF16), w2_ffn1=lp['w2_ffn1'].astype(BF16),
        w13_ffn2=lp['w13_ffn2'].astype(BF16), w2_ffn2=lp['w2_ffn2'].astype(BF16),
        w_a=w_in[:, :offs[2]].astype(BF16), w_b=w_in[:, offs[4]:].astype(BF16), w_ab=w_ab,
        w_br_dn=lp['w_br_dn'].astype(BF16), w_br_swa=lp['w_br_swa'].astype(BF16),
        w_out=lp['w_out'].astype(BF16),
        norm_ffn1=lp['norm_ffn1'].reshape(1, D_MODEL), norm_mix=lp['norm_mix'].reshape(1, D_MODEL),
        norm_ffn2=lp['norm_ffn2'].reshape(1, D_MODEL),
        conv_w=lp['conv_w'], alog_pad=pad_heads(lp['a_log']), dtb_pad=pad_heads(lp['dt_bias']),
        dn_norm=lp['dn_norm'].reshape(1, DN_DV), sinks=sinks,
        sink_lo=jnp.broadcast_to(sinks.reshape(pairs, 2)[:, 0:1], (pairs, LANES)),
        sink_hi=jnp.broadcast_to(sinks.reshape(pairs, 2)[:, 1:2], (pairs, LANES)),
    )


def _layer(x, mod, rows, w, final_w, final, past, n_batch, seq, tag):
    x = _ffn(x, mod, rows, (0, 1, 2), w['norm_ffn1'], w['w13_ffn1'], w['w2_ffn1'], "ffn1_" + tag)
    if past is None:
        in_rows = _Rows(n_batch, seq, INPROJ_ROW_TILE)
        qkv, z, q_sw, kv, gates, ab, tail = _inproj(
            x, mod, in_rows, (3, 4), w['norm_mix'], w['w_a'], w['w_b'], w['w_ab'], w['conv_w'],
            True, "inproj_" + tag)
        y_dn, s_new = _gdn_prompt(qkv, z, ab, w['alog_pad'], w['dtb_pad'], w['dn_norm'], n_batch, seq)
        y_sw = _swa_prompt(q_sw, kv, w['sinks'], n_batch, seq)
        conv_new = tail[:, SUBLANES - (CONV_W - 1):]
        keep = min(WINDOW, seq)
        kv3 = kv.reshape(n_batch, seq, 2 * SWA_KV_W)[:, seq - keep:]
        k_buf = kv3[:, :, :SWA_KV_W].reshape(n_batch, keep, SWA_KV_HEADS, SWA_HD)
        v_buf = kv3[:, :, SWA_KV_W:].reshape(n_batch, keep, SWA_KV_HEADS, SWA_HD)
    else:
        s0, conv_buf, k_old, v_old = past
        length = k_old.shape[1]
        qkv, z, q_sw, kv, gates, ab = _inproj(
            x, mod, rows, (3, 4), w['norm_mix'], w['w_a'], w['w_b'], w['w_ab'], w['conv_w'],
            False, "inproj_" + tag)
        conv_new, qt, kt, v, dec, beta = _gdn_step_prep(qkv, conv_buf, ab, w['conv_w'],
                                                        w['alog_pad'], w['dtb_pad'])
        s_new, y_dn = _gdn_step(s0, qt, kt, v, dec, beta, z, w['dn_norm'])
        o3, k_buf, v_buf = _swa_step(q_sw.reshape(n_batch, SWA_Q_HEADS // 2, LANES), kv,
                                     k_old.reshape(n_batch, length, SWA_KV_W),
                                     v_old.reshape(n_batch, length, SWA_KV_W),
                                     w['sink_lo'], w['sink_hi'])
        y_sw = o3.reshape(n_batch, SWA_Q_W)
        k_buf = k_buf.reshape(n_batch, length, SWA_KV_HEADS, SWA_HD)
        v_buf = v_buf.reshape(n_batch, length, SWA_KV_HEADS, SWA_HD)
    x = _mix_ffn(x, y_dn, y_sw, gates, mod, rows, w, final_w, final, "mix_ffn2_" + tag)
    return x, (s_new, conv_new, k_buf, v_buf)


def kernel(x_prompt, x_sample, state_dn, state_conv, cache_swa_k, cache_swa_v, c_prompt, c_sample,
           w_ada, b_ada, norm_ffn1, w13_ffn1, w2_ffn1, norm_mix, w_in, conv_w, a_log, dt_bias,
           dn_norm, sinks, w_br_dn, w_br_swa, w_out, norm_ffn2, w13_ffn2, w2_ffn2, final_norm):
    n_p, seq_p, d = x_prompt.shape
    n_s, seq_s, _ = x_sample.shape
    depth = w_ada.shape[0]
    assert d == D_MODEL and seq_s == 1 and seq_p % ROW_TILE == 0 and seq_p % GDN_CHUNK == 0
    assert w_in.shape[2] == sum(IN_SPLITS) and w13_ffn1.shape[2] == 2 * D_FF
    assert cache_swa_k.shape[2] == WINDOW and n_s % SWA_STEP_BATCH == 0 and n_p % SUBLANES == 0
    rows_p = _Rows(n_p, seq_p, ROW_TILE)
    rows_s = _Rows(n_s, seq_s, ROW_TILE)
    c_all = jnp.concatenate([c_prompt, c_sample], axis=0)
    final_w = final_norm.reshape(1, D_MODEL)
    y_p = x_prompt.reshape(n_p * seq_p, d)
    y_s = x_sample.reshape(n_s * seq_s, d)
    st_p, st_s = [], []
    for l in range(depth):
        lp = dict(w_ada=w_ada[l], b_ada=b_ada[l], norm_ffn1=norm_ffn1[l], w13_ffn1=w13_ffn1[l],
                  w2_ffn1=w2_ffn1[l], norm_mix=norm_mix[l], w_in=w_in[l], conv_w=conv_w[l],
                  a_log=a_log[l], dt_bias=dt_bias[l], dn_norm=dn_norm[l], sinks=sinks[l],
                  w_br_dn=w_br_dn[l], w_br_swa=w_br_swa[l], w_out=w_out[l], norm_ffn2=norm_ffn2[l],
                  w13_ffn2=w13_ffn2[l], w2_ffn2=w2_ffn2[l])
        w = _prep_layer_weights(lp)
        mod_p, mod_s = _ada(c_all, lp['w_ada'], lp['b_ada'], n_p)
        final = l == depth - 1
        y_p, sp = _layer(y_p, mod_p.reshape(rows_p.mod_shape), rows_p, w, final_w, final, None,
                         n_p, seq_p, "prompt")
        y_s, ss = _layer(y_s, mod_s.reshape(rows_s.mod_shape), rows_s, w, final_w, final,
                         (state_dn[l], state_conv[l], cache_swa_k[l], cache_swa_v[l]),
                         n_s, seq_s, "sample")
        st_p.append(sp)
        st_s.append(ss)
    stack = lambda sts, i: sts[0][i][None] if depth == 1 else jnp.stack([s[i] for s in sts])
    return (y_p.reshape(n_p, seq_p, d), y_s.reshape(n_s, seq_s, d),
            stack(st_p, 0), stack(st_s, 0), stack(st_p, 1), stack(st_s, 1),
            stack(st_p, 2), stack(st_s, 2), stack(st_p, 3), stack(st_s, 3))
```

```python
import functools

import jax
import jax.numpy as jnp
from jax import lax
from jax.experimental import pallas as pl
from jax.experimental.pallas import tpu as pltpu

F32 = jnp.float32
BF16 = jnp.bfloat16

D_MODEL = 1024
DN_HEADS = 8
DN_DK = 128
DN_DV = 128
DN_QK_W = DN_HEADS * DN_DK
DN_V_W = DN_HEADS * DN_DV
CONV_W = 4
CONV_DIM = 2 * DN_QK_W + DN_V_W
SWA_Q_HEADS = 16
SWA_KV_HEADS = 2
SWA_HD = 64
SWA_Q_W = SWA_Q_HEADS * SWA_HD
SWA_KV_W = SWA_KV_HEADS * SWA_HD
WINDOW = 128
D_FF = 2816
HALF_STEP = 0.5
N_MOD = 9
EPS = 1e-6
MASK_VALUE = -1e30
IN_SPLITS = (CONV_DIM, DN_V_W, DN_HEADS, DN_HEADS, SWA_Q_W, SWA_KV_W, SWA_KV_W, D_MODEL, D_MODEL)

LANES = 128
SUBLANES = 8
VMEM_LIMIT_BYTES = 56 * 1024 * 1024

GDN_CHUNK = 128
GDN_CHUNKS_PER_STEP = 4
SWA_BLOCK = 128
SWA_BLOCKS_PER_STEP = 2
FF_CHUNK = 256
ROW_TILE = 512
INPROJ_ROW_TILE = 256
IN_COL_CHUNK = 512


def _cparams(*sem):
    return pltpu.CompilerParams(dimension_semantics=sem, vmem_limit_bytes=VMEM_LIMIT_BYTES)


def _resident(shape):
    nd = len(shape)
    return pl.BlockSpec(shape, lambda *_: (0,) * nd, pipeline_mode=pl.Buffered(1))


def _dot(a, b):
    return jnp.dot(a.astype(BF16), b.astype(BF16), preferred_element_type=F32)


def _dot_nt(a, b):
    return lax.dot_general(a.astype(BF16), b.astype(BF16), (((1,), (1,)), ((), ())),
                           preferred_element_type=F32)


def _dot_tn(a, b):
    return lax.dot_general(a.astype(BF16), b.astype(BF16), (((0,), (0,)), ((), ())),
                           preferred_element_type=F32)


def _split3(a):
    hi = a.astype(BF16)
    r = a - hi.astype(F32)
    mid = r.astype(BF16)
    lo = (r - mid.astype(F32)).astype(BF16)
    return hi, mid, lo


def _dot3(a, b):
    a_hi = a.astype(BF16)
    a_lo = (a - a_hi.astype(F32)).astype(BF16)
    b_hi = b.astype(BF16)
    b_lo = (b - b_hi.astype(F32)).astype(BF16)
    d = functools.partial(jnp.dot, preferred_element_type=F32)
    return d(a_hi, b_hi) + (d(a_lo, b_hi) + d(a_hi, b_lo))


def _silu(x):
    return x * jax.nn.sigmoid(x)


def _rms_mod(x, nw, sc, sh):
    y = x * lax.rsqrt(jnp.mean(x * x, axis=-1, keepdims=True) + EPS)
    return (y * nw) * (1.0 + sc) + sh


def _ada_kernel(c_ref, w_ref, b_ref, op_ref, os_ref, *, n_prompt):
    m = _dot3(_silu(c_ref[...]), w_ref[...]) + b_ref[...]
    op_ref[...] = m[:n_prompt]
    os_ref[...] = m[n_prompt:]


def _ada(c_all, w_ada, b_ada, n_prompt):
    n_all, d = c_all.shape
    n_out = w_ada.shape[1]
    tn = D_MODEL
    return pl.pallas_call(
        functools.partial(_ada_kernel, n_prompt=n_prompt),
        grid=(n_out // tn,),
        in_specs=[pl.BlockSpec((n_all, d), lambda j: (0, 0)),
                  pl.BlockSpec((d, tn), lambda j: (0, j)),
                  pl.BlockSpec((1, tn), lambda j: (0, j))],
        out_specs=[pl.BlockSpec((n_prompt, tn), lambda j: (0, j)),
                   pl.BlockSpec((n_all - n_prompt, tn), lambda j: (0, j))],
        out_shape=[jax.ShapeDtypeStruct((n_prompt, n_out), F32),
                   jax.ShapeDtypeStruct((n_all - n_prompt, n_out), F32)],
        compiler_params=_cparams("arbitrary"),
        name="ada",
    )(c_all, w_ada, b_ada.reshape(1, n_out))


class _Rows:
    def __init__(self, n_batch, seq, row_tile):
        self.n_batch = n_batch
        self.n_rows = n_batch * seq
        if seq == 1:
            self.tm = n_batch
            self.mod_shape = (1, n_batch, N_MOD * D_MODEL)
            self.mod_block = (1, n_batch, D_MODEL)
            self.tiles_per_batch = None
        else:
            self.tm = min(row_tile, seq)
            assert seq % self.tm == 0
            self.mod_shape = (n_batch, 1, N_MOD * D_MODEL)
            self.mod_block = (1, 1, D_MODEL)
            self.tiles_per_batch = seq // self.tm
        self.grid = (self.n_rows // self.tm,)

    def mod_spec(self, piece):
        if self.tiles_per_batch is None:
            return pl.BlockSpec(self.mod_block, lambda i: (0, 0, piece))
        tpb = self.tiles_per_batch
        return pl.BlockSpec(self.mod_block, lambda i: (i // tpb, 0, piece))

    def row_spec(self, width):
        return pl.BlockSpec((self.tm, width), lambda i: (i, 0))


def _ffn_half_step(x, nw, sc, sh, g, w13_ref, w2_ref):
    h = _rms_mod(x, nw, sc, sh).astype(BF16)
    acc = jnp.zeros(x.shape, F32)
    for c0 in range(0, D_FF, FF_CHUNK):
        gate = jnp.dot(h, w13_ref[:, c0:c0 + FF_CHUNK], preferred_element_type=F32)
        up = jnp.dot(h, w13_ref[:, D_FF + c0:D_FF + c0 + FF_CHUNK], preferred_element_type=F32)
        act = (_silu(gate) * up).astype(BF16)
        acc = acc + jnp.dot(act, w2_ref[c0:c0 + FF_CHUNK, :], preferred_element_type=F32)
    return x + (HALF_STEP * g) * acc


def _ffn_kernel(x_ref, sh_ref, sc_ref, g_ref, nw_ref, w13_ref, w2_ref, o_ref):
    o_ref[...] = _ffn_half_step(x_ref[...], nw_ref[...], sc_ref[0], sh_ref[0], g_ref[0], w13_ref, w2_ref)


def _ffn(x, mod, rows, pieces, nw, w13, w2, name):
    sh, sc, g = pieces
    return pl.pallas_call(
        _ffn_kernel,
        grid=rows.grid,
        in_specs=[rows.row_spec(D_MODEL), rows.mod_spec(sh), rows.mod_spec(sc), rows.mod_spec(g),
                  _resident((1, D_MODEL)), _resident(w13.shape), _resident(w2.shape)],
        out_specs=rows.row_spec(D_MODEL),
        out_shape=jax.ShapeDtypeStruct((rows.n_rows, D_MODEL), F32),
        compiler_params=_cparams("parallel"),
        name=name,
    )(x, mod, mod, mod, nw, w13, w2)


IN_A_PIECES = (CONV_DIM, DN_V_W)
IN_B_PIECES = (SWA_Q_W, 2 * SWA_KV_W, 2 * D_MODEL)
IN_OUT_DTYPES = (F32, BF16, BF16, F32, BF16, F32)


def _split_w_in_kernel(w_ref, wa_ref, wb_ref, wab_ref):
    n_a = sum(IN_A_PIECES)
    n_ab = 2 * DN_HEADS
    wa_ref[...] = w_ref[:, :n_a].astype(BF16)
    wb_ref[...] = w_ref[:, n_a + n_ab:].astype(BF16)
    lane = lax.broadcasted_iota(jnp.int32, wab_ref.shape, 1)
    wab_ref[...] = jnp.where(lane < n_ab, w_ref[:, n_a:n_a + LANES], 0.0)


def _split_w_in(w_in):
    d, n = w_in.shape
    n_a, n_b = sum(IN_A_PIECES), sum(IN_B_PIECES)
    assert n == n_a + 2 * DN_HEADS + n_b
    tr = LANES
    return pl.pallas_call(
        _split_w_in_kernel,
        grid=(d // tr,),
        in_specs=[pl.BlockSpec((tr, n), lambda i: (i, 0))],
        out_specs=[pl.BlockSpec((tr, n_a), lambda i: (i, 0)), pl.BlockSpec((tr, n_b), lambda i: (i, 0)),
                   pl.BlockSpec((tr, LANES), lambda i: (i, 0))],
        out_shape=[jax.ShapeDtypeStruct((d, n_a), BF16), jax.ShapeDtypeStruct((d, n_b), BF16),
                   jax.ShapeDtypeStruct((d, LANES), F32)],
        compiler_params=_cparams("parallel"),
        name="split_w_in",
    )(w_in)


def _inproj_kernel(x_ref, sh_ref, sc_ref, nw_ref, wa_ref, wb_ref, wab_ref,
                   qkv_ref, z_ref, qsw_ref, kv_ref, gates_ref, ab_ref):
    h = _rms_mod(x_ref[...], nw_ref[...], sc_ref[0], sh_ref[0])
    hb = h.astype(BF16)

    def project(w_ref, off, width, ref):
        for c0 in range(0, width, IN_COL_CHUNK):
            cw = min(IN_COL_CHUNK, width - c0)
            val = jnp.dot(hb, w_ref[:, off + c0:off + c0 + cw], preferred_element_type=F32)
            ref[:, c0:c0 + cw] = val.astype(ref.dtype)

    project(wa_ref, 0, CONV_DIM, qkv_ref)
    project(wa_ref, CONV_DIM, DN_V_W, z_ref)
    off = 0
    for ref, width in zip((qsw_ref, kv_ref, gates_ref), IN_B_PIECES):
        project(wb_ref, off, width, ref)
        off += width
    ab_ref[...] = _dot3(h, wab_ref[...])


def _inproj(x, mod, rows, pieces, nw, w_a, w_b, w_ab, name):
    sh, sc = pieces
    widths = IN_A_PIECES + IN_B_PIECES + (LANES,)
    return pl.pallas_call(
        _inproj_kernel,
        grid=rows.grid,
        in_specs=[rows.row_spec(D_MODEL), rows.mod_spec(sh), rows.mod_spec(sc),
                  _resident((1, D_MODEL)), _resident(w_a.shape), _resident(w_b.shape),
                  _resident(w_ab.shape)],
        out_specs=[rows.row_spec(w) for w in widths],
        out_shape=[jax.ShapeDtypeStruct((rows.n_rows, w), dt) for w, dt in zip(widths, IN_OUT_DTYPES)],
        compiler_params=_cparams("parallel"),
        name=name,
    )(x, mod, mod, nw, w_a, w_b, w_ab)


def _unit_lower_inverse(mats, row, col):
    n = mats[0].shape[0]
    eye = jnp.where(row == col, 1.0, 0.0).astype(F32)
    in_block = (row >> 4) == (col >> 4)
    p = [jnp.where(in_block, -a, 0.0) for a in mats]
    r = [eye + pi for pi in p]
    q = [_dot(pi, pi) for pi in p]
    for _ in range(2):
        rq = [_dot(qi, jnp.concatenate([ri, qi], axis=1)) for qi, ri in zip(q, r)]
        r = [ri + rqi[:, :n] for ri, rqi in zip(r, rq)]
        q = [rqi[:, n:] for rqi in rq]
    x = [ri + _dot(qi, ri) for qi, ri in zip(q, r)]
    s = 4
    while (1 << s) < n:
        pair = ((row >> (s + 1)) == (col >> (s + 1))) & ((row >> s) > (col >> s))
        t = [_dot(jnp.where(pair, a, 0.0), xi) for a, xi in zip(mats, x)]
        x = [xi - _dot(xi, ti) for xi, ti in zip(x, t)]
        s += 1
    return x


def _gdn_prompt_kernel(qkv_ref, z_ref, ab_ref, cw_ref, alog_ref, dtb_ref, dnw_ref,
                       y_ref, s_out_ref, tail_ref, halo_ref, s_ref):
    t = pl.program_id(1)
    c = GDN_CHUNK
    rows = qkv_ref.shape[0]
    halo = SUBLANES
    heads = range(DN_HEADS)

    @pl.when(t == 0)
    def _():
        s_ref[...] = jnp.zeros(s_ref.shape, F32)
        halo_ref[...] = jnp.zeros(halo_ref.shape, F32)

    row = lax.broadcasted_iota(jnp.int32, (c, c), 0)
    col = lax.broadcasted_iota(jnp.int32, (c, c), 1)
    lower = row >= col
    strict = row > col
    tri = jnp.where(lower, 1.0, 0.0).astype(BF16)

    def l2n(a):
        return a * lax.rsqrt(jnp.sum(a * a, axis=-1, keepdims=True) + EPS)

    for ci in range(rows // c):
        r0 = ci * c

        def conv_silu(c0):
            cols = slice(c0, c0 + LANES)
            cur = qkv_ref[r0:r0 + c, cols]
            y = cur * cw_ref[CONV_W - 1:CONV_W, cols]
            if ci == 0:
                ext = jnp.concatenate([halo_ref[:, cols], cur], axis=0)
            for j in range(CONV_W - 1):
                lag = CONV_W - 1 - j
                if ci == 0:
                    shifted = ext[halo - lag:halo - lag + c]
                else:
                    shifted = qkv_ref[pl.ds(r0 - lag, c), cols]
                y = y + shifted * cw_ref[j:j + 1, cols]
            return _silu(y)

        ab = ab_ref[r0:r0 + c, :]
        g_log = -jnp.exp(alog_ref[...]) * jax.nn.softplus(ab + dtb_ref[...])
        beta_all = jax.nn.sigmoid(ab)
        gc = sum(jnp.dot(tri, piece, preferred_element_type=F32) for piece in _split3(g_log))
        gc_t = gc.T
        gc_last = gc[c - 1:c, :]

        qn = [l2n(conv_silu(h * DN_DK)) * (DN_DK ** -0.5) for h in heads]
        kn = [l2n(conv_silu(DN_QK_W + h * DN_DK)) for h in heads]
        v = [conv_silu(2 * DN_QK_W + h * DN_DV) for h in heads]
        g_col = [gc[:, h:h + 1] for h in heads]
        g_end = [gc_last[:, h:h + 1] for h in heads]
        beta = [beta_all[:, DN_HEADS + h:DN_HEADS + h + 1] for h in heads]
        decay = [jnp.where(lower, jnp.exp(jnp.where(lower, g_col[h] - gc_t[h:h + 1, :], 0.0)), 0.0)
                 for h in heads]
        kb = [kn[h] * beta[h] for h in heads]
        e_col = [jnp.exp(g_col[h]) for h in heads]

        kq = [_dot_nt(jnp.concatenate([kb[h], qn[h]], axis=0), kn[h]) for h in heads]
        a_mat = [jnp.where(strict, kq[h][:c] * decay[h], 0.0) for h in heads]
        qk = [kq[h][c:] * decay[h] for h in heads]
        x_inv = _unit_lower_inverse(a_mat, row, col)
        uw = [_dot(x_inv[h], jnp.concatenate([v[h] * beta[h], kb[h] * e_col[h]], axis=1)) for h in heads]
        s_old = [s_ref[h] for h in heads]
        ws = [_dot(jnp.concatenate([uw[h][:, DN_DV:], qn[h] * e_col[h]], axis=0), s_old[h]) for h in heads]
        v_new = [uw[h][:, :DN_DV] - ws[h][:c] for h in heads]
        o = [ws[h][c:] + _dot(qk[h], v_new[h]) for h in heads]
        for h in heads:
            k_dec = kn[h] * jnp.exp(g_end[h] - g_col[h])
            s_ref[h] = s_old[h] * jnp.exp(g_end[h]) + _dot_tn(k_dec, v_new[h])
        for h in heads:
            oh = o[h] * lax.rsqrt(jnp.mean(o[h] * o[h], axis=-1, keepdims=True) + EPS) * dnw_ref[...]
            zh = z_ref[r0:r0 + c, h * DN_DV:(h + 1) * DN_DV].astype(F32)
            y_ref[r0:r0 + c, h * DN_DV:(h + 1) * DN_DV] = (oh * _silu(zh)).astype(y_ref.dtype)

    halo_ref[...] = qkv_ref[rows - halo:rows, :]

    @pl.when(t == pl.num_programs(1) - 1)
    def _():
        s_out_ref[0] = s_ref[...]
        tail_ref[0] = qkv_ref[rows - halo:rows, :]


def _gdn_prompt(qkv, z, ab, conv_w, alog_pad, dtb_pad, dn_norm, n_batch, seq):
    rows = GDN_CHUNK * GDN_CHUNKS_PER_STEP
    assert seq % rows == 0
    nt = seq // rows
    row_spec = lambda w: pl.BlockSpec((rows, w), lambda b, t: (b * nt + t, 0))
    const = lambda shape: pl.BlockSpec(shape, lambda b, t: (0,) * len(shape))
    return pl.pallas_call(
        _gdn_prompt_kernel,
        grid=(n_batch, nt),
        in_specs=[row_spec(CONV_DIM), row_spec(DN_V_W), row_spec(LANES),
                  const((CONV_W, CONV_DIM)), const((1, LANES)), const((1, LANES)), const((1, DN_DV))],
        out_specs=[row_spec(DN_V_W),
                   pl.BlockSpec((1, DN_HEADS, DN_DK, DN_DV), lambda b, t: (b, 0, 0, 0)),
                   pl.BlockSpec((1, SUBLANES, CONV_DIM), lambda b, t: (b, 0, 0))],
        out_shape=[jax.ShapeDtypeStruct((n_batch * seq, DN_V_W), BF16),
                   jax.ShapeDtypeStruct((n_batch, DN_HEADS, DN_DK, DN_DV), F32),
                   jax.ShapeDtypeStruct((n_batch, SUBLANES, CONV_DIM), F32)],
        scratch_shapes=[pltpu.VMEM((SUBLANES, CONV_DIM), F32),
                        pltpu.VMEM((DN_HEADS, DN_DK, DN_DV), F32)],
        compiler_params=_cparams("parallel", "arbitrary"),
        name="gdn_prompt",
    )(qkv, z, ab, conv_w, alog_pad, dtb_pad, dn_norm)


def _kv_head_views(k2, v2, lo):
    k2r = pltpu.roll(k2, SWA_HD, axis=1)
    v2r = pltpu.roll(v2, SWA_HD, axis=1)
    k_lo = (jnp.where(lo, k2, 0.0), jnp.where(lo, k2r, 0.0))
    k_hi = (jnp.where(lo, 0.0, k2r), jnp.where(lo, 0.0, k2))
    v_dup = (jnp.where(lo, v2, v2r), jnp.where(lo, v2r, v2))
    return k_lo, k_hi, v_dup


def _swa_prompt_kernel(sink_ref, q_ref, kvc_ref, kvp_ref, o_ref):
    n = pl.program_id(1)
    blk = SWA_BLOCK
    n_sub = q_ref.shape[0] // blk
    lo = lax.broadcasted_iota(jnp.int32, (blk, LANES), 1) < SWA_HD
    own = (lax.broadcasted_iota(jnp.int32, (blk, blk), 0)
           >= lax.broadcasted_iota(jnp.int32, (blk, blk), 1))
    pairs = SWA_Q_HEADS // 2
    pairs_per_kv = pairs // SWA_KV_HEADS
    views = [_kv_head_views(kvp_ref[:, :LANES], kvp_ref[:, LANES:], lo)]
    for sub in range(n_sub):
        rs = slice(sub * blk, (sub + 1) * blk)
        views.append(_kv_head_views(kvc_ref[rs, :LANES], kvc_ref[rs, LANES:], lo))
    heads = [(sub, j, half) for sub in range(n_sub) for j in range(pairs) for half in range(2)]
    qp = {(sub, j): q_ref[sub * blk:(sub + 1) * blk, j * LANES:(j + 1) * LANES] * (SWA_HD ** -0.5)
          for sub in range(n_sub) for j in range(pairs)}
    s_own = [_dot_nt(qp[sub, j], views[1 + sub][half][j // pairs_per_kv]) for sub, j, half in heads]
    s_prev = [_dot_nt(qp[sub, j], views[sub][half][j // pairs_per_kv]) for sub, j, half in heads]
    p, den = [], []
    for i, (sub, j, half) in enumerate(heads):
        prev = jnp.where(n > 0, s_prev[i], MASK_VALUE) if sub == 0 else s_prev[i]
        s = jnp.where(own, s_own[i], prev)
        sink = sink_ref[2 * j + half]
        m = jnp.maximum(jnp.max(s, axis=-1, keepdims=True), sink)
        e = jnp.exp(s - m)
        p.append(e)
        den.append(jnp.sum(e, axis=-1, keepdims=True) + jnp.exp(sink - m))
    pv_own = [_dot(jnp.where(own, p[i], 0.0), views[1 + sub][2][j // pairs_per_kv])
              for i, (sub, j, _) in enumerate(heads)]
    pv_prev = [_dot(jnp.where(own, 0.0, p[i]), views[sub][2][j // pairs_per_kv])
               for i, (sub, j, _) in enumerate(heads)]
    out = [(pv_own[i] + pv_prev[i]) / den[i] for i in range(len(heads))]
    for i in range(0, len(heads), 2):
        sub, j, _ = heads[i]
        o_ref[sub * blk:(sub + 1) * blk, j * LANES:(j + 1) * LANES] = (
            jnp.where(lo, out[i], out[i + 1]).astype(o_ref.dtype))


def _swa_prompt(q, kv, sinks, n_batch, seq):
    blk = SWA_BLOCK
    n_sub = SWA_BLOCKS_PER_STEP
    assert seq % (blk * n_sub) == 0
    ns = seq // (blk * n_sub)
    step_rows = lambda w: pl.BlockSpec((blk * n_sub, w), lambda b, n: (b * ns + n, 0))
    return pl.pallas_call(
        _swa_prompt_kernel,
        grid=(n_batch, ns),
        in_specs=[pl.BlockSpec(memory_space=pltpu.SMEM),
                  step_rows(SWA_Q_W), step_rows(2 * SWA_KV_W),
                  pl.BlockSpec((blk, 2 * SWA_KV_W),
                               lambda b, n: ((b * ns + n) * n_sub - jnp.minimum(n, 1), 0))],
        out_specs=step_rows(SWA_Q_W),
        out_shape=jax.ShapeDtypeStruct((n_batch * seq, SWA_Q_W), BF16),
        compiler_params=_cparams("parallel", "arbitrary"),
        name="swa_prompt",
    )(sinks, q, kv, kv)


def _mix_ffn_kernel(x_ref, ydn_ref, ysw_ref, gates_ref, g2_ref, sh_ref, sc_ref, g3_ref, nw_ref,
                    wbd_ref, wbs_ref, wout_ref, w13_ref, w2_ref, fw_ref, o_ref, *, final):
    a = _dot(ydn_ref[...], wbd_ref[...])
    b = _dot(ysw_ref[...], wbs_ref[...])
    merged = (jax.nn.sigmoid(gates_ref[:, :D_MODEL].astype(F32)) * a
              + jax.nn.sigmoid(gates_ref[:, D_MODEL:].astype(F32)) * b)
    x = x_ref[...] + g2_ref[0] * _dot(merged, wout_ref[...])
    y = _ffn_half_step(x, nw_ref[...], sc_ref[0], sh_ref[0], g3_ref[0], w13_ref, w2_ref)
    if final:
        y = y * lax.rsqrt(jnp.mean(y * y, axis=-1, keepdims=True) + EPS) * fw_ref[...]
    o_ref[...] = y


def _mix_ffn(x, y_dn, y_sw, gates, mod, rows, w, final_w, final, name):
    weights = (w['w_br_dn'], w['w_br_swa'], w['w_out'], w['w13_ffn2'], w['w2_ffn2'])
    return pl.pallas_call(
        functools.partial(_mix_ffn_kernel, final=final),
        grid=rows.grid,
        in_specs=[rows.row_spec(D_MODEL), rows.row_spec(DN_V_W), rows.row_spec(SWA_Q_W),
                  rows.row_spec(2 * D_MODEL), rows.mod_spec(5), rows.mod_spec(6), rows.mod_spec(7),
                  rows.mod_spec(8), _resident((1, D_MODEL))]
                 + [_resident(a.shape) for a in weights] + [_resident((1, D_MODEL))],
        out_specs=rows.row_spec(D_MODEL),
        out_shape=jax.ShapeDtypeStruct((rows.n_rows, D_MODEL), F32),
        compiler_params=_cparams("parallel"),
        name=name,
    )(x, y_dn, y_sw, gates, mod, mod, mod, mod, w['norm_ffn2'], *weights, final_w)


def _gdn_step_prep_kernel(qkv_ref, cs_ref, ab_ref, cw_ref, alog_ref, dtb_ref,
                          cs_out_ref, qt_ref, kt_ref, v_ref, dec_ref, beta_ref):
    nb = qkv_ref.shape[0]
    for j in range(CONV_W - 2):
        cs_out_ref[:, j, :] = cs_ref[:, j + 1, :]
    cs_out_ref[:, CONV_W - 2, :] = qkv_ref[...]
    ab = ab_ref[...]
    dec = jnp.exp(-jnp.exp(alog_ref[...]) * jax.nn.softplus(ab + dtb_ref[...]))
    beta = jax.nn.sigmoid(ab)

    def conv_silu(c0):
        cols = slice(c0, c0 + LANES)
        y = qkv_ref[:, cols] * cw_ref[CONV_W - 1:CONV_W, cols]
        for j in range(CONV_W - 1):
            y = y + cs_ref[:, j, cols] * cw_ref[j:j + 1, cols]
        return _silu(y)

    for h in range(DN_HEADS):
        q = conv_silu(h * DN_DK)
        k = conv_silu(DN_QK_W + h * DN_DK)
        qn = q * lax.rsqrt(jnp.sum(q * q, axis=-1, keepdims=True) + EPS) * (DN_DK ** -0.5)
        kn = k * lax.rsqrt(jnp.sum(k * k, axis=-1, keepdims=True) + EPS)
        qt_ref[h] = qn.T
        kt_ref[h] = kn.T
        sl = slice(h * DN_DV, (h + 1) * DN_DV)
        v_ref[:, sl] = conv_silu(2 * DN_QK_W + h * DN_DV)
        dec_ref[:, sl] = jnp.broadcast_to(dec[:, h:h + 1], (nb, DN_DV))
        beta_ref[:, sl] = jnp.broadcast_to(beta[:, DN_HEADS + h:DN_HEADS + h + 1], (nb, DN_DV))


def _gdn_step_prep(qkv, conv_state, ab, conv_w, alog_pad, dtb_pad):
    nb = qkv.shape[0]
    full = _resident
    return pl.pallas_call(
        _gdn_step_prep_kernel,
        grid=(1,),
        in_specs=[full((nb, CONV_DIM)), full(conv_state.shape), full((nb, LANES)),
                  full((CONV_W, CONV_DIM)), full((1, LANES)), full((1, LANES))],
        out_specs=[full(conv_state.shape), full((DN_HEADS, DN_DK, nb)), full((DN_HEADS, DN_DK, nb)),
                   full((nb, DN_V_W)), full((nb, DN_V_W)), full((nb, DN_V_W))],
        out_shape=[jax.ShapeDtypeStruct(conv_state.shape, F32),
                   jax.ShapeDtypeStruct((DN_HEADS, DN_DK, nb), F32),
                   jax.ShapeDtypeStruct((DN_HEADS, DN_DK, nb), F32),
                   jax.ShapeDtypeStruct((nb, DN_V_W), F32),
                   jax.ShapeDtypeStruct((nb, DN_V_W), F32),
                   jax.ShapeDtypeStruct((nb, DN_V_W), F32)],
        compiler_params=_cparams("arbitrary"),
        name="gdn_step_prep",
    )(qkv, conv_state, ab, conv_w, alog_pad, dtb_pad)


def _gdn_step_kernel(s_ref, qt_ref, kt_ref, v_ref, dec_ref, beta_ref, z_ref, dnw_ref,
                     s_out_ref, y_ref, o_scr):
    nb = s_ref.shape[0]
    qt = qt_ref[0]
    kt = kt_ref[0]
    for b in range(nb):
        k_col = kt[:, b:b + 1]
        q_col = qt[:, b:b + 1]
        s1 = s_ref[b, 0] * dec_ref[b:b + 1, :]
        kv = jnp.sum(s1 * k_col, axis=0, keepdims=True)
        delta = (v_ref[b:b + 1, :] - kv) * beta_ref[b:b + 1, :]
        s2 = s1 + k_col * delta
        s_out_ref[b, 0] = s2
        o_scr[b:b + 1, :] = jnp.sum(s2 * q_col, axis=0, keepdims=True)
    o = o_scr[...]
    o = o * lax.rsqrt(jnp.mean(o * o, axis=-1, keepdims=True) + EPS) * dnw_ref[...]
    y_ref[...] = (o * _silu(z_ref[...].astype(F32))).astype(y_ref.dtype)


def _gdn_step(state, qt, kt, v, dec, beta, z, dn_norm):
    nb = state.shape[0]
    head_cols = pl.BlockSpec((nb, DN_DV), lambda h: (0, h))
    head_t = pl.BlockSpec((1, DN_DK, nb), lambda h: (h, 0, 0))
    s_spec = pl.BlockSpec((nb, 1, DN_DK, DN_DV), lambda h: (0, h, 0, 0))
    return pl.pallas_call(
        _gdn_step_kernel,
        grid=(DN_HEADS,),
        in_specs=[s_spec, head_t, head_t, head_cols, head_cols, head_cols, head_cols,
                  pl.BlockSpec((1, DN_DV), lambda h: (0, 0))],
        out_specs=[s_spec, head_cols],
        out_shape=[jax.ShapeDtypeStruct(state.shape, F32),
                   jax.ShapeDtypeStruct((nb, DN_V_W), BF16)],
        scratch_shapes=[pltpu.VMEM((nb, DN_DV), F32)],
        compiler_params=_cparams("parallel"),
        name="gdn_step",
    )(state, qt, kt, v, dec, beta, z, dn_norm)


SWA_STEP_BATCH = 8


def _swa_step_kernel(q_ref, kvn_ref, ck_ref, cv_ref, slo_ref, shi_ref, o_ref, ck_out_ref, cv_out_ref):
    length = ck_ref.shape[1]
    last = lax.broadcasted_iota(jnp.int32, (length, LANES), 0) == length - 1
    lo_k = lax.broadcasted_iota(jnp.int32, (length, LANES), 1) < SWA_HD
    pairs = SWA_Q_HEADS // 2
    first_kv = lax.broadcasted_iota(jnp.int32, (pairs, LANES), 0) < pairs // SWA_KV_HEADS
    lo_o = lax.broadcasted_iota(jnp.int32, (pairs, LANES), 1) < SWA_HD
    scale = SWA_HD ** -0.5
    for b in range(q_ref.shape[0]):
        k2 = jnp.where(last, kvn_ref[b:b + 1, 0:LANES], pltpu.roll(ck_ref[b], length - 1, axis=0))
        v2 = jnp.where(last, kvn_ref[b:b + 1, LANES:2 * LANES], pltpu.roll(cv_ref[b], length - 1, axis=0))
        ck_out_ref[b] = k2
        cv_out_ref[b] = v2
        k_lo, k_hi, v_dup = _kv_head_views(k2, v2, lo_k)
        q8 = q_ref[b]
        halves = []
        for keys, sink_ref in ((k_lo, slo_ref), (k_hi, shi_ref)):
            s = jnp.where(first_kv, _dot_nt(q8, keys[0]), _dot_nt(q8, keys[1])) * scale
            sink = sink_ref[...]
            m = jnp.maximum(jnp.max(s, axis=-1, keepdims=True), sink)
            p = jnp.exp(s - m)
            den = jnp.sum(p, axis=-1, keepdims=True) + jnp.exp(sink - m)
            halves.append(jnp.where(first_kv, _dot(p, v_dup[0]), _dot(p, v_dup[1])) / den)
        o_ref[b] = jnp.where(lo_o, halves[0], halves[1]).astype(o_ref.dtype)


def _swa_step(q3, kv_new, cache_k, cache_v, sink_lo, sink_hi):
    nb, length, _ = cache_k.shape
    tb = SWA_STEP_BATCH
    pairs = SWA_Q_HEADS // 2
    q_spec = pl.BlockSpec((tb, pairs, LANES), lambda i: (i, 0, 0))
    c_spec = pl.BlockSpec((tb, length, LANES), lambda i: (i, 0, 0))
    sink_spec = pl.BlockSpec((pairs, LANES), lambda i: (0, 0))
    return pl.pallas_call(
        _swa_step_kernel,
        grid=(nb // tb,),
        in_specs=[q_spec, pl.BlockSpec((tb, 2 * SWA_KV_W), lambda i: (i, 0)), c_spec, c_spec,
                  sink_spec, sink_spec],
        out_specs=[q_spec, c_spec, c_spec],
        out_shape=[jax.ShapeDtypeStruct(q3.shape, BF16),
                   jax.ShapeDtypeStruct(cache_k.shape, F32),
                   jax.ShapeDtypeStruct(cache_v.shape, F32)],
        compiler_params=_cparams("parallel"),
        name="swa_step",
    )(q3, kv_new, cache_k, cache_v, sink_lo, sink_hi)


def _prep_layer_weights(lp):
    w_a, w_b, w_ab = _split_w_in(lp['w_in'])
    pad_heads = lambda a: jnp.pad(a.astype(F32), (0, LANES - DN_HEADS)).reshape(1, LANES)
    sinks = lp['sinks'].astype(F32)
    pairs = SWA_Q_HEADS // 2
    return dict(
        w13_ffn1=lp['w13_ffn1'].astype(BF16), w2_ffn1=lp['w2_ffn1'].astype(BF16),
        w13_ffn2=lp['w13_ffn2'].astype(BF16), w2_ffn2=lp['w2_ffn2'].astype(BF16),
        w_a=w_a, w_b=w_b, w_ab=w_ab,
        w_br_dn=lp['w_br_dn'].astype(BF16), w_br_swa=lp['w_br_swa'].astype(BF16),
        w_out=lp['w_out'].astype(BF16),
        norm_ffn1=lp['norm_ffn1'].reshape(1, D_MODEL), norm_mix=lp['norm_mix'].reshape(1, D_MODEL),
        norm_ffn2=lp['norm_ffn2'].reshape(1, D_MODEL),
        conv_w=lp['conv_w'], alog_pad=pad_heads(lp['a_log']), dtb_pad=pad_heads(lp['dt_bias']),
        dn_norm=lp['dn_norm'].reshape(1, DN_DV), sinks=sinks,
        sink_lo=jnp.broadcast_to(sinks.reshape(pairs, 2)[:, 0:1], (pairs, LANES)),
        sink_hi=jnp.broadcast_to(sinks.reshape(pairs, 2)[:, 1:2], (pairs, LANES)),
    )


def _layer(x, mod, rows, w, final_w, final, past, n_batch, seq, tag):
    x = _ffn(x, mod, rows, (0, 1, 2), w['norm_ffn1'], w['w13_ffn1'], w['w2_ffn1'], "ffn1_" + tag)
    if past is None:
        in_rows = _Rows(n_batch, seq, INPROJ_ROW_TILE)
        qkv, z, q_sw, kv, gates, ab = _inproj(
            x, mod, in_rows, (3, 4), w['norm_mix'], w['w_a'], w['w_b'], w['w_ab'], "inproj_" + tag)
        y_dn, s_new, tail = _gdn_prompt(qkv, z, ab, w['conv_w'], w['alog_pad'], w['dtb_pad'],
                                        w['dn_norm'], n_batch, seq)
        y_sw = _swa_prompt(q_sw, kv, w['sinks'], n_batch, seq)
        conv_new = tail[:, SUBLANES - (CONV_W - 1):]
        keep = min(WINDOW, seq)
        kv3 = kv.reshape(n_batch, seq, 2 * SWA_KV_W)[:, seq - keep:]
        k_buf = kv3[:, :, :SWA_KV_W].reshape(n_batch, keep, SWA_KV_HEADS, SWA_HD)
        v_buf = kv3[:, :, SWA_KV_W:].reshape(n_batch, keep, SWA_KV_HEADS, SWA_HD)
    else:
        s0, conv_buf, k_old, v_old = past
        length = k_old.shape[1]
        qkv, z, q_sw, kv, gates, ab = _inproj(
            x, mod, rows, (3, 4), w['norm_mix'], w['w_a'], w['w_b'], w['w_ab'], "inproj_" + tag)
        conv_new, qt, kt, v, dec, beta = _gdn_step_prep(qkv, conv_buf, ab, w['conv_w'],
                                                        w['alog_pad'], w['dtb_pad'])
        s_new, y_dn = _gdn_step(s0, qt, kt, v, dec, beta, z, w['dn_norm'])
        o3, k_buf, v_buf = _swa_step(q_sw.reshape(n_batch, SWA_Q_HEADS // 2, LANES), kv,
                                     k_old.reshape(n_batch, length, SWA_KV_W),
                                     v_old.reshape(n_batch, length, SWA_KV_W),
                                     w['sink_lo'], w['sink_hi'])
        y_sw = o3.reshape(n_batch, SWA_Q_W)
        k_buf = k_buf.reshape(n_batch, length, SWA_KV_HEADS, SWA_HD)
        v_buf = v_buf.reshape(n_batch, length, SWA_KV_HEADS, SWA_HD)
    x = _mix_ffn(x, y_dn, y_sw, gates, mod, rows, w, final_w, final, "mix_ffn2_" + tag)
    return x, (s_new, conv_new, k_buf, v_buf)


def kernel(x_prompt, x_sample, state_dn, state_conv, cache_swa_k, cache_swa_v, c_prompt, c_sample,
           w_ada, b_ada, norm_ffn1, w13_ffn1, w2_ffn1, norm_mix, w_in, conv_w, a_log, dt_bias,
           dn_norm, sinks, w_br_dn, w_br_swa, w_out, norm_ffn2, w13_ffn2, w2_ffn2, final_norm):
    n_p, seq_p, d = x_prompt.shape
    n_s, seq_s, _ = x_sample.shape
    depth = w_ada.shape[0]
    assert d == D_MODEL and seq_s == 1 and seq_p % ROW_TILE == 0 and seq_p % GDN_CHUNK == 0
    assert w_in.shape[2] == sum(IN_SPLITS) and w13_ffn1.shape[2] == 2 * D_FF
    assert cache_swa_k.shape[2] == WINDOW and n_s % SWA_STEP_BATCH == 0 and n_p % SUBLANES == 0
    rows_p = _Rows(n_p, seq_p, ROW_TILE)
    rows_s = _Rows(n_s, seq_s, ROW_TILE)
    c_all = jnp.concatenate([c_prompt, c_sample], axis=0)
    final_w = final_norm.reshape(1, D_MODEL)
    y_p = x_prompt.reshape(n_p * seq_p, d)
    y_s = x_sample.reshape(n_s * seq_s, d)
    st_p, st_s = [], []
    for l in range(depth):
        lp = dict(w_ada=w_ada[l], b_ada=b_ada[l], norm_ffn1=norm_ffn1[l], w13_ffn1=w13_ffn1[l],
                  w2_ffn1=w2_ffn1[l], norm_mix=norm_mix[l], w_in=w_in[l], conv_w=conv_w[l],
                  a_log=a_log[l], dt_bias=dt_bias[l], dn_norm=dn_norm[l], sinks=sinks[l],
                  w_br_dn=w_br_dn[l], w_br_swa=w_br_swa[l], w_out=w_out[l], norm_ffn2=norm_ffn2[l],
                  w13_ffn2=w13_ffn2[l], w2_ffn2=w2_ffn2[l])
        w = _prep_layer_weights(lp)
        mod_p, mod_s = _ada(c_all, lp['w_ada'], lp['b_ada'], n_p)
        final = l == depth - 1
        y_p, sp = _layer(y_p, mod_p.reshape(rows_p.mod_shape), rows_p, w, final_w, final, None,
                         n_p, seq_p, "prompt")
        y_s, ss = _layer(y_s, mod_s.reshape(rows_s.mod_shape), rows_s, w, final_w, final,
                         (state_dn[l], state_conv[l], cache_swa_k[l], cache_swa_v[l]),
                         n_s, seq_s, "sample")
        st_p.append(sp)
        st_s.append(ss)
    stack = lambda sts, i: sts[0][i][None] if depth == 1 else jnp.stack([s[i] for s in sts])
    return (y_p.reshape(n_p, seq_p, d), y_s.reshape(n_s, seq_s, d),
            stack(st_p, 0), stack(st_s, 0), stack(st_p, 1), stack(st_s, 1),
            stack(st_p, 2), stack(st_s, 2), stack(st_p, 3), stack(st_s, 3))
```

```python
import functools

import jax
import jax.numpy as jnp
from jax import lax
from jax.experimental import pallas as pl
from jax.experimental.pallas import tpu as pltpu

F32 = jnp.float32
BF16 = jnp.bfloat16

D_MODEL = 1024
DN_HEADS = 8
DN_DK = 128
DN_DV = 128
DN_QK_W = DN_HEADS * DN_DK
DN_V_W = DN_HEADS * DN_DV
CONV_W = 4
CONV_DIM = 2 * DN_QK_W + DN_V_W
SWA_Q_HEADS = 16
SWA_KV_HEADS = 2
SWA_HD = 64
SWA_Q_W = SWA_Q_HEADS * SWA_HD
SWA_KV_W = SWA_KV_HEADS * SWA_HD
WINDOW = 128
D_FF = 2816
HALF_STEP = 0.5
N_MOD = 9
EPS = 1e-6
MASK_VALUE = -1e30
IN_SPLITS = (CONV_DIM, DN_V_W, DN_HEADS, DN_HEADS, SWA_Q_W, SWA_KV_W, SWA_KV_W, D_MODEL, D_MODEL)

LANES = 128
SUBLANES = 8
VMEM_LIMIT_BYTES = 56 * 1024 * 1024

GDN_CHUNK = 128
GDN_CHUNKS_PER_STEP = 4
SWA_BLOCK = 128
SWA_BLOCKS_PER_STEP = 2
FF_CHUNK = 256
ROW_TILE = 512
INPROJ_ROW_TILE = 512
IN_COL_CHUNK = 512


def _cparams(*sem):
    return pltpu.CompilerParams(dimension_semantics=sem, vmem_limit_bytes=VMEM_LIMIT_BYTES)


def _resident(shape):
    nd = len(shape)
    return pl.BlockSpec(shape, lambda *_: (0,) * nd, pipeline_mode=pl.Buffered(1))


def _dot(a, b):
    return jnp.dot(a.astype(BF16), b.astype(BF16), preferred_element_type=F32)


def _dot_nt(a, b):
    return lax.dot_general(a.astype(BF16), b.astype(BF16), (((1,), (1,)), ((), ())),
                           preferred_element_type=F32)


def _dot_tn(a, b):
    return lax.dot_general(a.astype(BF16), b.astype(BF16), (((0,), (0,)), ((), ())),
                           preferred_element_type=F32)


def _split3(a):
    hi = a.astype(BF16)
    r = a - hi.astype(F32)
    mid = r.astype(BF16)
    lo = (r - mid.astype(F32)).astype(BF16)
    return hi, mid, lo


def _dot3(a, b):
    a_hi = a.astype(BF16)
    a_lo = (a - a_hi.astype(F32)).astype(BF16)
    b_hi = b.astype(BF16)
    b_lo = (b - b_hi.astype(F32)).astype(BF16)
    d = functools.partial(jnp.dot, preferred_element_type=F32)
    return d(a_hi, b_hi) + (d(a_lo, b_hi) + d(a_hi, b_lo))


def _silu(x):
    return x * jax.nn.sigmoid(x)


def _rms_mod(x, nw, sc, sh):
    y = x * lax.rsqrt(jnp.mean(x * x, axis=-1, keepdims=True) + EPS)
    return (y * nw) * (1.0 + sc) + sh


def _ada_kernel(c_ref, w_ref, b_ref, op_ref, os_ref, *, n_prompt):
    m = _dot3(_silu(c_ref[...]), w_ref[...]) + b_ref[...]
    op_ref[...] = m[:n_prompt]
    os_ref[...] = m[n_prompt:]


def _ada(c_all, w_ada, b_ada, n_prompt):
    n_all, d = c_all.shape
    n_out = w_ada.shape[1]
    tn = D_MODEL
    return pl.pallas_call(
        functools.partial(_ada_kernel, n_prompt=n_prompt),
        grid=(n_out // tn,),
        in_specs=[pl.BlockSpec((n_all, d), lambda j: (0, 0)),
                  pl.BlockSpec((d, tn), lambda j: (0, j)),
                  pl.BlockSpec((1, tn), lambda j: (0, j))],
        out_specs=[pl.BlockSpec((n_prompt, tn), lambda j: (0, j)),
                   pl.BlockSpec((n_all - n_prompt, tn), lambda j: (0, j))],
        out_shape=[jax.ShapeDtypeStruct((n_prompt, n_out), F32),
                   jax.ShapeDtypeStruct((n_all - n_prompt, n_out), F32)],
        compiler_params=_cparams("arbitrary"),
        name="ada",
    )(c_all, w_ada, b_ada.reshape(1, n_out))


class _Rows:
    def __init__(self, n_batch, seq, row_tile):
        self.n_batch = n_batch
        self.n_rows = n_batch * seq
        if seq == 1:
            self.tm = n_batch
            self.mod_shape = (1, n_batch, N_MOD * D_MODEL)
            self.mod_block = (1, n_batch, D_MODEL)
            self.tiles_per_batch = None
        else:
            self.tm = min(row_tile, seq)
            assert seq % self.tm == 0
            self.mod_shape = (n_batch, 1, N_MOD * D_MODEL)
            self.mod_block = (1, 1, D_MODEL)
            self.tiles_per_batch = seq // self.tm
        self.grid = (self.n_rows // self.tm,)

    def mod_spec(self, piece):
        if self.tiles_per_batch is None:
            return pl.BlockSpec(self.mod_block, lambda i: (0, 0, piece))
        tpb = self.tiles_per_batch
        return pl.BlockSpec(self.mod_block, lambda i: (i // tpb, 0, piece))

    def row_spec(self, width):
        return pl.BlockSpec((self.tm, width), lambda i: (i, 0))


def _ffn_half_step(x, nw, sc, sh, g, w13_ref, w2_ref):
    h = _rms_mod(x, nw, sc, sh).astype(BF16)
    acc = jnp.zeros(x.shape, F32)
    for c0 in range(0, D_FF, FF_CHUNK):
        gate = jnp.dot(h, w13_ref[:, c0:c0 + FF_CHUNK], preferred_element_type=F32)
        up = jnp.dot(h, w13_ref[:, D_FF + c0:D_FF + c0 + FF_CHUNK], preferred_element_type=F32)
        act = (_silu(gate) * up).astype(BF16)
        acc = acc + jnp.dot(act, w2_ref[c0:c0 + FF_CHUNK, :], preferred_element_type=F32)
    return x + (HALF_STEP * g) * acc


def _ffn_kernel(x_ref, sh_ref, sc_ref, g_ref, nw_ref, w13_ref, w2_ref, o_ref):
    o_ref[...] = _ffn_half_step(x_ref[...], nw_ref[...], sc_ref[0], sh_ref[0], g_ref[0], w13_ref, w2_ref)


def _ffn(x, mod, rows, pieces, nw, w13, w2, name):
    sh, sc, g = pieces
    return pl.pallas_call(
        _ffn_kernel,
        grid=rows.grid,
        in_specs=[rows.row_spec(D_MODEL), rows.mod_spec(sh), rows.mod_spec(sc), rows.mod_spec(g),
                  _resident((1, D_MODEL)), _resident(w13.shape), _resident(w2.shape)],
        out_specs=rows.row_spec(D_MODEL),
        out_shape=jax.ShapeDtypeStruct((rows.n_rows, D_MODEL), F32),
        compiler_params=_cparams("parallel"),
        name=name,
    )(x, mod, mod, mod, nw, w13, w2)


IN_A_PIECES = (CONV_DIM, DN_V_W)
IN_B_PIECES = (SWA_Q_W, 2 * SWA_KV_W, 2 * D_MODEL)
IN_OUT_DTYPES = (F32, BF16, BF16, F32, BF16, F32)


def _split_w_in_kernel(w_ref, wa_ref, wb_ref, wab_ref):
    n_a = sum(IN_A_PIECES)
    n_ab = 2 * DN_HEADS
    wa_ref[...] = w_ref[:, :n_a].astype(BF16)
    wb_ref[...] = w_ref[:, n_a + n_ab:].astype(BF16)
    lane = lax.broadcasted_iota(jnp.int32, wab_ref.shape, 1)
    wab_ref[...] = jnp.where(lane < n_ab, w_ref[:, n_a:n_a + LANES], 0.0)


def _split_w_in(w_in):
    d, n = w_in.shape
    n_a, n_b = sum(IN_A_PIECES), sum(IN_B_PIECES)
    assert n == n_a + 2 * DN_HEADS + n_b
    tr = LANES
    return pl.pallas_call(
        _split_w_in_kernel,
        grid=(d // tr,),
        in_specs=[pl.BlockSpec((tr, n), lambda i: (i, 0))],
        out_specs=[pl.BlockSpec((tr, n_a), lambda i: (i, 0)), pl.BlockSpec((tr, n_b), lambda i: (i, 0)),
                   pl.BlockSpec((tr, LANES), lambda i: (i, 0))],
        out_shape=[jax.ShapeDtypeStruct((d, n_a), BF16), jax.ShapeDtypeStruct((d, n_b), BF16),
                   jax.ShapeDtypeStruct((d, LANES), F32)],
        compiler_params=_cparams("parallel"),
        name="split_w_in",
    )(w_in)


def _inproj_kernel(x_ref, sh_ref, sc_ref, nw_ref, wa_ref, wb_ref, wab_ref,
                   qkv_ref, z_ref, qsw_ref, kv_ref, gates_ref, ab_ref):
    h = _rms_mod(x_ref[...], nw_ref[...], sc_ref[0], sh_ref[0])
    hb = h.astype(BF16)

    def project(w_ref, off, width, ref):
        for c0 in range(0, width, IN_COL_CHUNK):
            cw = min(IN_COL_CHUNK, width - c0)
            val = jnp.dot(hb, w_ref[:, off + c0:off + c0 + cw], preferred_element_type=F32)
            ref[:, c0:c0 + cw] = val.astype(ref.dtype)

    project(wa_ref, 0, CONV_DIM, qkv_ref)
    project(wa_ref, CONV_DIM, DN_V_W, z_ref)
    off = 0
    for ref, width in zip((qsw_ref, kv_ref, gates_ref), IN_B_PIECES):
        project(wb_ref, off, width, ref)
        off += width
    ab_ref[...] = _dot3(h, wab_ref[...])


def _inproj(x, mod, rows, pieces, nw, w_a, w_b, w_ab, name):
    sh, sc = pieces
    widths = IN_A_PIECES + IN_B_PIECES + (LANES,)
    return pl.pallas_call(
        _inproj_kernel,
        grid=rows.grid,
        in_specs=[rows.row_spec(D_MODEL), rows.mod_spec(sh), rows.mod_spec(sc),
                  _resident((1, D_MODEL)), _resident(w_a.shape), _resident(w_b.shape),
                  _resident(w_ab.shape)],
        out_specs=[rows.row_spec(w) for w in widths],
        out_shape=[jax.ShapeDtypeStruct((rows.n_rows, w), dt) for w, dt in zip(widths, IN_OUT_DTYPES)],
        compiler_params=_cparams("parallel"),
        name=name,
    )(x, mod, mod, nw, w_a, w_b, w_ab)


def _unit_lower_inverse(mats, row, col):
    n = mats[0].shape[0]
    eye = jnp.where(row == col, 1.0, 0.0).astype(F32)
    in_block = (row >> 4) == (col >> 4)
    p = [jnp.where(in_block, -a, 0.0) for a in mats]
    r = [eye + pi for pi in p]
    q = [_dot(pi, pi) for pi in p]
    for _ in range(2):
        rq = [_dot(qi, jnp.concatenate([ri, qi], axis=1)) for qi, ri in zip(q, r)]
        r = [ri + rqi[:, :n] for ri, rqi in zip(r, rq)]
        q = [rqi[:, n:] for rqi in rq]
    x = [ri + _dot(qi, ri) for qi, ri in zip(q, r)]
    s = 4
    while (1 << s) < n:
        pair = ((row >> (s + 1)) == (col >> (s + 1))) & ((row >> s) > (col >> s))
        t = [_dot(jnp.where(pair, a, 0.0), xi) for a, xi in zip(mats, x)]
        x = [xi - _dot(xi, ti) for xi, ti in zip(x, t)]
        s += 1
    return x


def _gdn_prompt_kernel(qkv_ref, z_ref, ab_ref, cw_ref, alog_ref, dtb_ref, dnw_ref,
                       y_ref, s_out_ref, tail_ref, halo_ref, s_ref):
    t = pl.program_id(1)
    c = GDN_CHUNK
    rows = qkv_ref.shape[0]
    halo = SUBLANES
    heads = range(DN_HEADS)

    @pl.when(t == 0)
    def _():
        s_ref[...] = jnp.zeros(s_ref.shape, F32)
        halo_ref[...] = jnp.zeros(halo_ref.shape, F32)

    row = lax.broadcasted_iota(jnp.int32, (c, c), 0)
    col = lax.broadcasted_iota(jnp.int32, (c, c), 1)
    lower = row >= col
    strict = row > col
    tri = jnp.where(lower, 1.0, 0.0).astype(BF16)

    def l2n(a, scale=1.0):
        return a * (lax.rsqrt(jnp.sum(a * a, axis=-1, keepdims=True) + EPS) * scale)

    for ci in range(rows // c):
        r0 = ci * c

        def conv_silu(c0):
            cols = slice(c0, c0 + LANES)
            cur = qkv_ref[r0:r0 + c, cols]
            y = cur * cw_ref[CONV_W - 1:CONV_W, cols]
            if ci == 0:
                ext = jnp.concatenate([halo_ref[:, cols], cur], axis=0)
            for j in range(CONV_W - 1):
                lag = CONV_W - 1 - j
                if ci == 0:
                    shifted = ext[halo - lag:halo - lag + c]
                else:
                    shifted = qkv_ref[pl.ds(r0 - lag, c), cols]
                y = y + shifted * cw_ref[j:j + 1, cols]
            return _silu(y)

        ab = ab_ref[r0:r0 + c, :]
        g_log = -jnp.exp(alog_ref[...]) * jax.nn.softplus(ab + dtb_ref[...])
        beta_all = jax.nn.sigmoid(ab)
        gc = sum(jnp.dot(tri, piece, preferred_element_type=F32) for piece in _split3(g_log))
        gc_t = gc.T
        gc_last = gc[c - 1:c, :]

        qn = [l2n(conv_silu(h * DN_DK), DN_DK ** -0.5) for h in heads]
        kn = [l2n(conv_silu(DN_QK_W + h * DN_DK)) for h in heads]
        v = [conv_silu(2 * DN_QK_W + h * DN_DV) for h in heads]
        g_col = [gc[:, h:h + 1] for h in heads]
        g_end = [gc_last[:, h:h + 1] for h in heads]
        beta = [beta_all[:, DN_HEADS + h:DN_HEADS + h + 1] for h in heads]
        decay = [jnp.where(lower, jnp.exp(g_col[h] - gc_t[h:h + 1, :]), 0.0) for h in heads]
        kb = [kn[h] * beta[h] for h in heads]
        e_col = [jnp.exp(g_col[h]) for h in heads]

        kq = [_dot_nt(jnp.concatenate([kb[h], qn[h]], axis=0), kn[h]) for h in heads]
        a_mat = [jnp.where(strict, kq[h][:c] * decay[h], 0.0) for h in heads]
        qk = [kq[h][c:] * decay[h] for h in heads]
        x_inv = _unit_lower_inverse(a_mat, row, col)
        uw = [_dot(x_inv[h], jnp.concatenate([v[h] * beta[h], kb[h] * e_col[h]], axis=1)) for h in heads]
        s_old = [s_ref[h] for h in heads]
        ws = [_dot(jnp.concatenate([uw[h][:, DN_DV:], qn[h] * e_col[h]], axis=0), s_old[h]) for h in heads]
        v_new = [uw[h][:, :DN_DV] - ws[h][:c] for h in heads]
        o = [ws[h][c:] + _dot(qk[h], v_new[h]) for h in heads]
        for h in heads:
            k_dec = kn[h] * jnp.exp(g_end[h] - g_col[h])
            s_ref[h] = s_old[h] * jnp.exp(g_end[h]) + _dot_tn(k_dec, v_new[h])
        for h in heads:
            oh = o[h] * lax.rsqrt(jnp.mean(o[h] * o[h], axis=-1, keepdims=True) + EPS) * dnw_ref[...]
            zh = z_ref[r0:r0 + c, h * DN_DV:(h + 1) * DN_DV].astype(F32)
            y_ref[r0:r0 + c, h * DN_DV:(h + 1) * DN_DV] = (oh * _silu(zh)).astype(y_ref.dtype)

    halo_ref[...] = qkv_ref[rows - halo:rows, :]

    @pl.when(t == pl.num_programs(1) - 1)
    def _():
        s_out_ref[0] = s_ref[...]
        tail_ref[0] = qkv_ref[rows - halo:rows, :]


def _gdn_prompt(qkv, z, ab, conv_w, alog_pad, dtb_pad, dn_norm, n_batch, seq):
    rows = GDN_CHUNK * GDN_CHUNKS_PER_STEP
    assert seq % rows == 0
    nt = seq // rows
    row_spec = lambda w: pl.BlockSpec((rows, w), lambda b, t: (b * nt + t, 0))
    const = lambda shape: pl.BlockSpec(shape, lambda b, t: (0,) * len(shape))
    return pl.pallas_call(
        _gdn_prompt_kernel,
        grid=(n_batch, nt),
        in_specs=[row_spec(CONV_DIM), row_spec(DN_V_W), row_spec(LANES),
                  const((CONV_W, CONV_DIM)), const((1, LANES)), const((1, LANES)), const((1, DN_DV))],
        out_specs=[row_spec(DN_V_W),
                   pl.BlockSpec((1, DN_HEADS, DN_DK, DN_DV), lambda b, t: (b, 0, 0, 0)),
                   pl.BlockSpec((1, SUBLANES, CONV_DIM), lambda b, t: (b, 0, 0))],
        out_shape=[jax.ShapeDtypeStruct((n_batch * seq, DN_V_W), BF16),
                   jax.ShapeDtypeStruct((n_batch, DN_HEADS, DN_DK, DN_DV), F32),
                   jax.ShapeDtypeStruct((n_batch, SUBLANES, CONV_DIM), F32)],
        scratch_shapes=[pltpu.VMEM((SUBLANES, CONV_DIM), F32),
                        pltpu.VMEM((DN_HEADS, DN_DK, DN_DV), F32)],
        compiler_params=_cparams("parallel", "arbitrary"),
        name="gdn_prompt",
    )(qkv, z, ab, conv_w, alog_pad, dtb_pad, dn_norm)


def _kv_head_views(k2, v2, lo):
    k2r = pltpu.roll(k2, SWA_HD, axis=1)
    v2r = pltpu.roll(v2, SWA_HD, axis=1)
    k_lo = (jnp.where(lo, k2, 0.0), jnp.where(lo, k2r, 0.0))
    k_hi = (jnp.where(lo, 0.0, k2r), jnp.where(lo, 0.0, k2))
    v_dup = (jnp.where(lo, v2, v2r), jnp.where(lo, v2r, v2))
    return k_lo, k_hi, v_dup


def _swa_prompt_kernel(sink_ref, q_ref, kvc_ref, kvp_ref, o_ref):
    n = pl.program_id(1)
    blk = SWA_BLOCK
    n_sub = q_ref.shape[0] // blk
    lo = lax.broadcasted_iota(jnp.int32, (blk, LANES), 1) < SWA_HD
    own = (lax.broadcasted_iota(jnp.int32, (blk, blk), 0)
           >= lax.broadcasted_iota(jnp.int32, (blk, blk), 1))
    pairs = SWA_Q_HEADS // 2
    pairs_per_kv = pairs // SWA_KV_HEADS
    views = [_kv_head_views(kvp_ref[:, :LANES], kvp_ref[:, LANES:], lo)]
    for sub in range(n_sub):
        rs = slice(sub * blk, (sub + 1) * blk)
        views.append(_kv_head_views(kvc_ref[rs, :LANES], kvc_ref[rs, LANES:], lo))
    k_cat = {(sub, c): jnp.concatenate([views[1 + sub][0][c], views[1 + sub][1][c],
                                        views[sub][0][c], views[sub][1][c]], axis=0).astype(BF16)
             for sub in range(n_sub) for c in range(SWA_KV_HEADS)}
    v_cat = {(sub, c): jnp.concatenate([views[1 + sub][2][c], views[sub][2][c]], axis=0).astype(BF16)
             for sub in range(n_sub) for c in range(SWA_KV_HEADS)}
    groups = [(sub, c) for sub in range(n_sub) for c in range(SWA_KV_HEADS)]

    def score_stage(sub, c):
        return [_dot_nt(q_ref[sub * blk:(sub + 1) * blk, j * LANES:(j + 1) * LANES]
                        * (SWA_HD ** -0.5),
                        k_cat[sub, c])
                for j in range(c * pairs_per_kv, (c + 1) * pairs_per_kv)]

    def finish(sub, c, scores):
        heads = [(jj, half) for jj in range(pairs_per_kv) for half in range(2)]
        p, den = [], []
        for jj, half in heads:
            s_own = scores[jj][:, half * blk:(half + 1) * blk]
            s_prev = scores[jj][:, (2 + half) * blk:(3 + half) * blk]
            if sub == 0:
                s_prev = jnp.where(n > 0, s_prev, MASK_VALUE)
            s = jnp.where(own, s_own, s_prev)
            sink = sink_ref[2 * (c * pairs_per_kv + jj) + half]
            m = jnp.maximum(jnp.max(s, axis=-1, keepdims=True), sink)
            e = jnp.exp(s - m)
            p.append(jnp.concatenate([jnp.where(own, e, 0.0), jnp.where(own, 0.0, e)], axis=1))
            den.append(jnp.sum(e, axis=-1, keepdims=True) + jnp.exp(sink - m))
        out = [_dot(p[i], v_cat[sub, c]) / den[i] for i in range(len(heads))]
        for jj in range(pairs_per_kv):
            j = c * pairs_per_kv + jj
            o_ref[sub * blk:(sub + 1) * blk, j * LANES:(j + 1) * LANES] = (
                jnp.where(lo, out[2 * jj], out[2 * jj + 1]).astype(o_ref.dtype))

    pending = score_stage(*groups[0])
    for g, (sub, c) in enumerate(groups):
        nxt = score_stage(*groups[g + 1]) if g + 1 < len(groups) else None
        finish(sub, c, pending)
        pending = nxt


def _swa_prompt(q, kv, sinks, n_batch, seq):
    blk = SWA_BLOCK
    n_sub = SWA_BLOCKS_PER_STEP
    assert seq % (blk * n_sub) == 0
    ns = seq // (blk * n_sub)
    step_rows = lambda w: pl.BlockSpec((blk * n_sub, w), lambda b, n: (b * ns + n, 0))
    return pl.pallas_call(
        _swa_prompt_kernel,
        grid=(n_batch, ns),
        in_specs=[pl.BlockSpec(memory_space=pltpu.SMEM),
                  step_rows(SWA_Q_W), step_rows(2 * SWA_KV_W),
                  pl.BlockSpec((blk, 2 * SWA_KV_W),
                               lambda b, n: ((b * ns + n) * n_sub - jnp.minimum(n, 1), 0))],
        out_specs=step_rows(SWA_Q_W),
        out_shape=jax.ShapeDtypeStruct((n_batch * seq, SWA_Q_W), BF16),
        compiler_params=_cparams("parallel", "arbitrary"),
        name="swa_prompt",
    )(sinks, q, kv, kv)


def _mix_ffn_kernel(x_ref, ydn_ref, ysw_ref, gates_ref, g2_ref, sh_ref, sc_ref, g3_ref, nw_ref,
                    wbd_ref, wbs_ref, wout_ref, w13_ref, w2_ref, fw_ref, o_ref, *, final):
    a = _dot(ydn_ref[...], wbd_ref[...])
    b = _dot(ysw_ref[...], wbs_ref[...])
    merged = (jax.nn.sigmoid(gates_ref[:, :D_MODEL].astype(F32)) * a
              + jax.nn.sigmoid(gates_ref[:, D_MODEL:].astype(F32)) * b)
    x = x_ref[...] + g2_ref[0] * _dot(merged, wout_ref[...])
    y = _ffn_half_step(x, nw_ref[...], sc_ref[0], sh_ref[0], g3_ref[0], w13_ref, w2_ref)
    if final:
        y = y * lax.rsqrt(jnp.mean(y * y, axis=-1, keepdims=True) + EPS) * fw_ref[...]
    o_ref[...] = y


def _mix_ffn(x, y_dn, y_sw, gates, mod, rows, w, final_w, final, name):
    weights = (w['w_br_dn'], w['w_br_swa'], w['w_out'], w['w13_ffn2'], w['w2_ffn2'])
    return pl.pallas_call(
        functools.partial(_mix_ffn_kernel, final=final),
        grid=rows.grid,
        in_specs=[rows.row_spec(D_MODEL), rows.row_spec(DN_V_W), rows.row_spec(SWA_Q_W),
                  rows.row_spec(2 * D_MODEL), rows.mod_spec(5), rows.mod_spec(6), rows.mod_spec(7),
                  rows.mod_spec(8), _resident((1, D_MODEL))]
                 + [_resident(a.shape) for a in weights] + [_resident((1, D_MODEL))],
        out_specs=rows.row_spec(D_MODEL),
        out_shape=jax.ShapeDtypeStruct((rows.n_rows, D_MODEL), F32),
        compiler_params=_cparams("parallel"),
        name=name,
    )(x, y_dn, y_sw, gates, mod, mod, mod, mod, w['norm_ffn2'], *weights, final_w)


def _gdn_step_prep_kernel(qkv_ref, cs_ref, ab_ref, cw_ref, alog_ref, dtb_ref,
                          cs_out_ref, qt_ref, kt_ref, v_ref, dec_ref, beta_ref):
    nb = qkv_ref.shape[0]
    for j in range(CONV_W - 2):
        cs_out_ref[:, j, :] = cs_ref[:, j + 1, :]
    cs_out_ref[:, CONV_W - 2, :] = qkv_ref[...]
    ab = ab_ref[...]
    dec = jnp.exp(-jnp.exp(alog_ref[...]) * jax.nn.softplus(ab + dtb_ref[...]))
    beta = jax.nn.sigmoid(ab)

    def conv_silu(c0):
        cols = slice(c0, c0 + LANES)
        y = qkv_ref[:, cols] * cw_ref[CONV_W - 1:CONV_W, cols]
        for j in range(CONV_W - 1):
            y = y + cs_ref[:, j, cols] * cw_ref[j:j + 1, cols]
        return _silu(y)

    for h in range(DN_HEADS):
        q = conv_silu(h * DN_DK)
        k = conv_silu(DN_QK_W + h * DN_DK)
        qn = q * lax.rsqrt(jnp.sum(q * q, axis=-1, keepdims=True) + EPS) * (DN_DK ** -0.5)
        kn = k * lax.rsqrt(jnp.sum(k * k, axis=-1, keepdims=True) + EPS)
        qt_ref[h] = qn.T
        kt_ref[h] = kn.T
        sl = slice(h * DN_DV, (h + 1) * DN_DV)
        v_ref[:, sl] = conv_silu(2 * DN_QK_W + h * DN_DV)
        dec_ref[:, sl] = jnp.broadcast_to(dec[:, h:h + 1], (nb, DN_DV))
        beta_ref[:, sl] = jnp.broadcast_to(beta[:, DN_HEADS + h:DN_HEADS + h + 1], (nb, DN_DV))


def _gdn_step_prep(qkv, conv_state, ab, conv_w, alog_pad, dtb_pad):
    nb = qkv.shape[0]
    full = _resident
    return pl.pallas_call(
        _gdn_step_prep_kernel,
        grid=(1,),
        in_specs=[full((nb, CONV_DIM)), full(conv_state.shape), full((nb, LANES)),
                  full((CONV_W, CONV_DIM)), full((1, LANES)), full((1, LANES))],
        out_specs=[full(conv_state.shape), full((DN_HEADS, DN_DK, nb)), full((DN_HEADS, DN_DK, nb)),
                   full((nb, DN_V_W)), full((nb, DN_V_W)), full((nb, DN_V_W))],
        out_shape=[jax.ShapeDtypeStruct(conv_state.shape, F32),
                   jax.ShapeDtypeStruct((DN_HEADS, DN_DK, nb), F32),
                   jax.ShapeDtypeStruct((DN_HEADS, DN_DK, nb), F32),
                   jax.ShapeDtypeStruct((nb, DN_V_W), F32),
                   jax.ShapeDtypeStruct((nb, DN_V_W), F32),
                   jax.ShapeDtypeStruct((nb, DN_V_W), F32)],
        compiler_params=_cparams("arbitrary"),
        name="gdn_step_prep",
    )(qkv, conv_state, ab, conv_w, alog_pad, dtb_pad)


def _gdn_step_kernel(s_ref, qt_ref, kt_ref, v_ref, dec_ref, beta_ref, z_ref, dnw_ref,
                     s_out_ref, y_ref, o_scr):
    nb = s_ref.shape[0]
    qt = qt_ref[0]
    kt = kt_ref[0]
    for b in range(nb):
        k_col = kt[:, b:b + 1]
        q_col = qt[:, b:b + 1]
        s1 = s_ref[b, 0] * dec_ref[b:b + 1, :]
        kv = jnp.sum(s1 * k_col, axis=0, keepdims=True)
        delta = (v_ref[b:b + 1, :] - kv) * beta_ref[b:b + 1, :]
        s2 = s1 + k_col * delta
        s_out_ref[b, 0] = s2
        o_scr[b:b + 1, :] = jnp.sum(s2 * q_col, axis=0, keepdims=True)
    o = o_scr[...]
    o = o * lax.rsqrt(jnp.mean(o * o, axis=-1, keepdims=True) + EPS) * dnw_ref[...]
    y_ref[...] = (o * _silu(z_ref[...].astype(F32))).astype(y_ref.dtype)


def _gdn_step(state, qt, kt, v, dec, beta, z, dn_norm):
    nb = state.shape[0]
    head_cols = pl.BlockSpec((nb, DN_DV), lambda h: (0, h))
    head_t = pl.BlockSpec((1, DN_DK, nb), lambda h: (h, 0, 0))
    s_spec = pl.BlockSpec((nb, 1, DN_DK, DN_DV), lambda h: (0, h, 0, 0))
    return pl.pallas_call(
        _gdn_step_kernel,
        grid=(DN_HEADS,),
        in_specs=[s_spec, head_t, head_t, head_cols, head_cols, head_cols, head_cols,
                  pl.BlockSpec((1, DN_DV), lambda h: (0, 0))],
        out_specs=[s_spec, head_cols],
        out_shape=[jax.ShapeDtypeStruct(state.shape, F32),
                   jax.ShapeDtypeStruct((nb, DN_V_W), BF16)],
        scratch_shapes=[pltpu.VMEM((nb, DN_DV), F32)],
        compiler_params=_cparams("parallel"),
        name="gdn_step",
    )(state, qt, kt, v, dec, beta, z, dn_norm)


SWA_STEP_BATCH = 16


def _swa_step_kernel(q_ref, kvn_ref, ck_ref, cv_ref, slo_ref, shi_ref, o_ref, ck_out_ref, cv_out_ref):
    length = ck_ref.shape[1]
    last = lax.broadcasted_iota(jnp.int32, (length, LANES), 0) == length - 1
    lo_k = lax.broadcasted_iota(jnp.int32, (length, LANES), 1) < SWA_HD
    pairs = SWA_Q_HEADS // 2
    first_kv = lax.broadcasted_iota(jnp.int32, (pairs, LANES), 0) < pairs // SWA_KV_HEADS
    lo_o = lax.broadcasted_iota(jnp.int32, (pairs, LANES), 1) < SWA_HD
    samples = range(q_ref.shape[0])
    k2 = [jnp.where(last, kvn_ref[b:b + 1, 0:LANES], pltpu.roll(ck_ref[b], length - 1, axis=0))
          for b in samples]
    v2 = [jnp.where(last, kvn_ref[b:b + 1, LANES:2 * LANES], pltpu.roll(cv_ref[b], length - 1, axis=0))
          for b in samples]
    for b in samples:
        ck_out_ref[b] = k2[b]
        cv_out_ref[b] = v2[b]
    views = [_kv_head_views(k2[b], v2[b], lo_k) for b in samples]
    k_cat = [jnp.concatenate([views[b][0][0], views[b][0][1], views[b][1][0], views[b][1][1]], axis=0)
             for b in samples]
    v_cat = [jnp.concatenate([views[b][2][0], views[b][2][1]], axis=1) for b in samples]
    scores = [_dot_nt(q_ref[b] * (SWA_HD ** -0.5), k_cat[b]) for b in samples]
    p, den = [], []
    for b in samples:
        halves_p, halves_den = [], []
        for half, sink_ref in enumerate((slo_ref, shi_ref)):
            s = jnp.where(first_kv, scores[b][:, 2 * half * length:(2 * half + 1) * length],
                          scores[b][:, (2 * half + 1) * length:(2 * half + 2) * length])
            sink = sink_ref[...]
            m = jnp.maximum(jnp.max(s, axis=-1, keepdims=True), sink)
            e = jnp.exp(s - m)
            halves_p.append(e)
            halves_den.append(jnp.sum(e, axis=-1, keepdims=True) + jnp.exp(sink - m))
        p.append(jnp.concatenate(halves_p, axis=0))
        den.append(halves_den)
    pv = [_dot(p[b], v_cat[b]) for b in samples]
    for b in samples:
        halves = [jnp.where(first_kv, pv[b][half * pairs:(half + 1) * pairs, :LANES],
                            pv[b][half * pairs:(half + 1) * pairs, LANES:]) / den[b][half]
                  for half in range(2)]
        o_ref[b] = jnp.where(lo_o, halves[0], halves[1]).astype(o_ref.dtype)


def _swa_step(q3, kv_new, cache_k, cache_v, sink_lo, sink_hi):
    nb, length, _ = cache_k.shape
    tb = SWA_STEP_BATCH
    pairs = SWA_Q_HEADS // 2
    q_spec = pl.BlockSpec((tb, pairs, LANES), lambda i: (i, 0, 0))
    c_spec = pl.BlockSpec((tb, length, LANES), lambda i: (i, 0, 0))
    sink_spec = pl.BlockSpec((pairs, LANES), lambda i: (0, 0))
    return pl.pallas_call(
        _swa_step_kernel,
        grid=(nb // tb,),
        in_specs=[q_spec, pl.BlockSpec((tb, 2 * SWA_KV_W), lambda i: (i, 0)), c_spec, c_spec,
                  sink_spec, sink_spec],
        out_specs=[q_spec, c_spec, c_spec],
        out_shape=[jax.ShapeDtypeStruct(q3.shape, BF16),
                   jax.ShapeDtypeStruct(cache_k.shape, F32),
                   jax.ShapeDtypeStruct(cache_v.shape, F32)],
        compiler_params=_cparams("parallel"),
        name="swa_step",
    )(q3, kv_new, cache_k, cache_v, sink_lo, sink_hi)


def _prep_layer_weights(lp):
    w_a, w_b, w_ab = _split_w_in(lp['w_in'])
    pad_heads = lambda a: jnp.pad(a.astype(F32), (0, LANES - DN_HEADS)).reshape(1, LANES)
    sinks = lp['sinks'].astype(F32)
    pairs = SWA_Q_HEADS // 2
    return dict(
        w13_ffn1=lp['w13_ffn1'].astype(BF16), w2_ffn1=lp['w2_ffn1'].astype(BF16),
        w13_ffn2=lp['w13_ffn2'].astype(BF16), w2_ffn2=lp['w2_ffn2'].astype(BF16),
        w_a=w_a, w_b=w_b, w_ab=w_ab,
        w_br_dn=lp['w_br_dn'].astype(BF16), w_br_swa=lp['w_br_swa'].astype(BF16),
        w_out=lp['w_out'].astype(BF16),
        norm_ffn1=lp['norm_ffn1'].reshape(1, D_MODEL), norm_mix=lp['norm_mix'].reshape(1, D_MODEL),
        norm_ffn2=lp['norm_ffn2'].reshape(1, D_MODEL),
        conv_w=lp['conv_w'], alog_pad=pad_heads(lp['a_log']), dtb_pad=pad_heads(lp['dt_bias']),
        dn_norm=lp['dn_norm'].reshape(1, DN_DV), sinks=sinks,
        sink_lo=jnp.broadcast_to(sinks.reshape(pairs, 2)[:, 0:1], (pairs, LANES)),
        sink_hi=jnp.broadcast_to(sinks.reshape(pairs, 2)[:, 1:2], (pairs, LANES)),
    )


def _layer(x, mod, rows, w, final_w, final, past, n_batch, seq, tag):
    x = _ffn(x, mod, rows, (0, 1, 2), w['norm_ffn1'], w['w13_ffn1'], w['w2_ffn1'], "ffn1_" + tag)
    if past is None:
        in_rows = _Rows(n_batch, seq, INPROJ_ROW_TILE)
        qkv, z, q_sw, kv, gates, ab = _inproj(
            x, mod, in_rows, (3, 4), w['norm_mix'], w['w_a'], w['w_b'], w['w_ab'], "inproj_" + tag)
        y_dn, s_new, tail = _gdn_prompt(qkv, z, ab, w['conv_w'], w['alog_pad'], w['dtb_pad'],
                                        w['dn_norm'], n_batch, seq)
        y_sw = _swa_prompt(q_sw, kv, w['sinks'], n_batch, seq)
        conv_new = tail[:, SUBLANES - (CONV_W - 1):]
        keep = min(WINDOW, seq)
        kv3 = kv.reshape(n_batch, seq, 2 * SWA_KV_W)[:, seq - keep:]
        k_buf = kv3[:, :, :SWA_KV_W].reshape(n_batch, keep, SWA_KV_HEADS, SWA_HD)
        v_buf = kv3[:, :, SWA_KV_W:].reshape(n_batch, keep, SWA_KV_HEADS, SWA_HD)
    else:
        s0, conv_buf, k_old, v_old = past
        length = k_old.shape[1]
        qkv, z, q_sw, kv, gates, ab = _inproj(
            x, mod, rows, (3, 4), w['norm_mix'], w['w_a'], w['w_b'], w['w_ab'], "inproj_" + tag)
        conv_new, qt, kt, v, dec, beta = _gdn_step_prep(qkv, conv_buf, ab, w['conv_w'],
                                                        w['alog_pad'], w['dtb_pad'])
        s_new, y_dn = _gdn_step(s0, qt, kt, v, dec, beta, z, w['dn_norm'])
        o3, k_buf, v_buf = _swa_step(q_sw.reshape(n_batch, SWA_Q_HEADS // 2, LANES), kv,
                                     k_old.reshape(n_batch, length, SWA_KV_W),
                                     v_old.reshape(n_batch, length, SWA_KV_W),
                                     w['sink_lo'], w['sink_hi'])
        y_sw = o3.reshape(n_batch, SWA_Q_W)
        k_buf = k_buf.reshape(n_batch, length, SWA_KV_HEADS, SWA_HD)
        v_buf = v_buf.reshape(n_batch, length, SWA_KV_HEADS, SWA_HD)
    x = _mix_ffn(x, y_dn, y_sw, gates, mod, rows, w, final_w, final, "mix_ffn2_" + tag)
    return x, (s_new, conv_new, k_buf, v_buf)


def kernel(x_prompt, x_sample, state_dn, state_conv, cache_swa_k, cache_swa_v, c_prompt, c_sample,
           w_ada, b_ada, norm_ffn1, w13_ffn1, w2_ffn1, norm_mix, w_in, conv_w, a_log, dt_bias,
           dn_norm, sinks, w_br_dn, w_br_swa, w_out, norm_ffn2, w13_ffn2, w2_ffn2, final_norm):
    n_p, seq_p, d = x_prompt.shape
    n_s, seq_s, _ = x_sample.shape
    depth = w_ada.shape[0]
    assert d == D_MODEL and seq_s == 1 and seq_p % ROW_TILE == 0 and seq_p % GDN_CHUNK == 0
    assert w_in.shape[2] == sum(IN_SPLITS) and w13_ffn1.shape[2] == 2 * D_FF
    assert cache_swa_k.shape[2] == WINDOW and n_s % SWA_STEP_BATCH == 0 and n_p % SUBLANES == 0
    rows_p = _Rows(n_p, seq_p, ROW_TILE)
    rows_s = _Rows(n_s, seq_s, ROW_TILE)
    c_all = jnp.concatenate([c_prompt, c_sample], axis=0)
    final_w = final_norm.reshape(1, D_MODEL)
    y_p = x_prompt.reshape(n_p * seq_p, d)
    y_s = x_sample.reshape(n_s * seq_s, d)
    st_p, st_s = [], []
    for l in range(depth):
        lp = dict(w_ada=w_ada[l], b_ada=b_ada[l], norm_ffn1=norm_ffn1[l], w13_ffn1=w13_ffn1[l],
                  w2_ffn1=w2_ffn1[l], norm_mix=norm_mix[l], w_in=w_in[l], conv_w=conv_w[l],
                  a_log=a_log[l], dt_bias=dt_bias[l], dn_norm=dn_norm[l], sinks=sinks[l],
                  w_br_dn=w_br_dn[l], w_br_swa=w_br_swa[l], w_out=w_out[l], norm_ffn2=norm_ffn2[l],
                  w13_ffn2=w13_ffn2[l], w2_ffn2=w2_ffn2[l])
        w = _prep_layer_weights(lp)
        mod_p, mod_s = _ada(c_all, lp['w_ada'], lp['b_ada'], n_p)
        final = l == depth - 1
        y_p, sp = _layer(y_p, mod_p.reshape(rows_p.mod_shape), rows_p, w, final_w, final, None,
                         n_p, seq_p, "prompt")
        y_s, ss = _layer(y_s, mod_s.reshape(rows_s.mod_shape), rows_s, w, final_w, final,
                         (state_dn[l], state_conv[l], cache_swa_k[l], cache_swa_v[l]),
                         n_s, seq_s, "sample")
        st_p.append(sp)
        st_s.append(ss)
    stack = lambda sts, i: sts[0][i][None] if depth == 1 else jnp.stack([s[i] for s in sts])
    return (y_p.reshape(n_p, seq_p, d), y_s.reshape(n_s, seq_s, d),
            stack(st_p, 0), stack(st_s, 0), stack(st_p, 1), stack(st_s, 1),
            stack(st_p, 2), stack(st_s, 2), stack(st_p, 3), stack(st_s, 3))
```

```python
import functools

import jax
import jax.numpy as jnp
from jax import lax
from jax.experimental import pallas as pl
from jax.experimental.pallas import tpu as pltpu

F32 = jnp.float32
BF16 = jnp.bfloat16

D_MODEL = 1024
DN_HEADS = 8
DN_DK = 128
DN_DV = 128
DN_QK_W = DN_HEADS * DN_DK
DN_V_W = DN_HEADS * DN_DV
CONV_W = 4
CONV_DIM = 2 * DN_QK_W + DN_V_W
SWA_Q_HEADS = 16
SWA_KV_HEADS = 2
SWA_HD = 64
SWA_Q_W = SWA_Q_HEADS * SWA_HD
SWA_KV_W = SWA_KV_HEADS * SWA_HD
WINDOW = 128
D_FF = 2816
HALF_STEP = 0.5
N_MOD = 9
EPS = 1e-6
MASK_VALUE = -1e30
IN_SPLITS = (CONV_DIM, DN_V_W, DN_HEADS, DN_HEADS, SWA_Q_W, SWA_KV_W, SWA_KV_W, D_MODEL, D_MODEL)

LANES = 128
SUBLANES = 8
VMEM_LIMIT_BYTES = 56 * 1024 * 1024

GDN_CHUNK = 128
GDN_CHUNKS_PER_STEP = 4
SWA_BLOCK = 128
SWA_BLOCKS_PER_STEP = 2
FF_CHUNK = 256
ROW_TILE = 512
INPROJ_ROW_TILE = 512
IN_COL_CHUNK = 512


def _cparams(*sem):
    return pltpu.CompilerParams(dimension_semantics=sem, vmem_limit_bytes=VMEM_LIMIT_BYTES)


def _resident(shape):
    nd = len(shape)
    return pl.BlockSpec(shape, lambda *_: (0,) * nd, pipeline_mode=pl.Buffered(1))


def _dot(a, b):
    return jnp.dot(a.astype(BF16), b.astype(BF16), preferred_element_type=F32)


def _dot_nt(a, b):
    return lax.dot_general(a.astype(BF16), b.astype(BF16), (((1,), (1,)), ((), ())),
                           preferred_element_type=F32)


def _dot_tn(a, b):
    return lax.dot_general(a.astype(BF16), b.astype(BF16), (((0,), (0,)), ((), ())),
                           preferred_element_type=F32)


def _split3(a):
    hi = a.astype(BF16)
    r = a - hi.astype(F32)
    mid = r.astype(BF16)
    lo = (r - mid.astype(F32)).astype(BF16)
    return hi, mid, lo


def _dot3(a, b):
    a_hi = a.astype(BF16)
    a_lo = (a - a_hi.astype(F32)).astype(BF16)
    b_hi = b.astype(BF16)
    b_lo = (b - b_hi.astype(F32)).astype(BF16)
    d = functools.partial(jnp.dot, preferred_element_type=F32)
    return d(a_hi, b_hi) + (d(a_lo, b_hi) + d(a_hi, b_lo))


def _silu(x):
    return x * jax.nn.sigmoid(x)


def _rms_mod(x, nw, sc, sh):
    y = x * lax.rsqrt(jnp.mean(x * x, axis=-1, keepdims=True) + EPS)
    return (y * nw) * (1.0 + sc) + sh


def _ada_kernel(c_ref, w_ref, b_ref, op_ref, os_ref, *, n_prompt):
    m = _dot(_silu(c_ref[...]), w_ref[...]) + b_ref[...]
    op_ref[...] = m[:n_prompt]
    os_ref[...] = m[n_prompt:]


def _ada(c_all, w_ada, b_ada, n_prompt):
    n_all, d = c_all.shape
    n_out = w_ada.shape[1]
    tn = D_MODEL
    return pl.pallas_call(
        functools.partial(_ada_kernel, n_prompt=n_prompt),
        grid=(n_out // tn,),
        in_specs=[pl.BlockSpec((n_all, d), lambda j: (0, 0)),
                  pl.BlockSpec((d, tn), lambda j: (0, j)),
                  pl.BlockSpec((1, tn), lambda j: (0, j))],
        out_specs=[pl.BlockSpec((n_prompt, tn), lambda j: (0, j)),
                   pl.BlockSpec((n_all - n_prompt, tn), lambda j: (0, j))],
        out_shape=[jax.ShapeDtypeStruct((n_prompt, n_out), F32),
                   jax.ShapeDtypeStruct((n_all - n_prompt, n_out), F32)],
        compiler_params=_cparams("arbitrary"),
        name="ada",
    )(c_all, w_ada, b_ada.reshape(1, n_out))


class _Rows:
    def __init__(self, n_batch, seq, row_tile):
        self.n_batch = n_batch
        self.n_rows = n_batch * seq
        if seq == 1:
            self.tm = n_batch
            self.mod_shape = (1, n_batch, N_MOD * D_MODEL)
            self.mod_block = (1, n_batch, D_MODEL)
            self.tiles_per_batch = None
        else:
            self.tm = min(row_tile, seq)
            assert seq % self.tm == 0
            self.mod_shape = (n_batch, 1, N_MOD * D_MODEL)
            self.mod_block = (1, 1, D_MODEL)
            self.tiles_per_batch = seq // self.tm
        self.grid = (self.n_rows // self.tm,)

    def mod_spec(self, piece):
        if self.tiles_per_batch is None:
            return pl.BlockSpec(self.mod_block, lambda i: (0, 0, piece))
        tpb = self.tiles_per_batch
        return pl.BlockSpec(self.mod_block, lambda i: (i // tpb, 0, piece))

    def row_spec(self, width):
        return pl.BlockSpec((self.tm, width), lambda i: (i, 0))


def _ffn_half_step(x, nw, sc, sh, g, wg_ref, wu_ref, w2_ref):
    h = _rms_mod(x, nw, sc, sh).astype(BF16)
    acc = jnp.zeros(x.shape, F32)
    for c0 in range(0, D_FF, FF_CHUNK):
        gate = jnp.dot(h, wg_ref[:, c0:c0 + FF_CHUNK], preferred_element_type=F32)
        up = jnp.dot(h, wu_ref[:, c0:c0 + FF_CHUNK], preferred_element_type=F32)
        act = (_silu(gate) * up).astype(BF16)
        acc = acc + jnp.dot(act, w2_ref[c0:c0 + FF_CHUNK, :], preferred_element_type=F32)
    return x + (HALF_STEP * g) * acc


def _ffn_kernel(x_ref, sh_ref, sc_ref, g_ref, nw_ref, wg_ref, wu_ref, w2_ref, o_ref):
    o_ref[...] = _ffn_half_step(x_ref[...], nw_ref[...], sc_ref[0], sh_ref[0], g_ref[0],
                                wg_ref, wu_ref, w2_ref)


def _ffn(x, mod, rows, pieces, nw, ffn_w, name):
    sh, sc, g = pieces
    return pl.pallas_call(
        _ffn_kernel,
        grid=rows.grid,
        in_specs=[rows.row_spec(D_MODEL), rows.mod_spec(sh), rows.mod_spec(sc), rows.mod_spec(g),
                  _resident((1, D_MODEL))] + [_resident(a.shape) for a in ffn_w],
        out_specs=rows.row_spec(D_MODEL),
        out_shape=jax.ShapeDtypeStruct((rows.n_rows, D_MODEL), F32),
        compiler_params=_cparams("parallel"),
        name=name,
    )(x, mod, mod, mod, nw, *ffn_w)


def _ffn_stream_step(h_scr, acc_scr, wg_ref, wu_ref, w2_ref, wgb_ref, wub_ref, w2b_ref):
    wg = wg_ref[...].astype(BF16)
    wu = wu_ref[...].astype(BF16)
    w2 = w2_ref[...].astype(BF16)
    wgb_ref[...] = wg
    wub_ref[...] = wu
    w2b_ref[...] = w2
    h = h_scr[...]
    gate = jnp.dot(h, wg, preferred_element_type=F32)
    up = jnp.dot(h, wu, preferred_element_type=F32)
    act = (_silu(gate) * up).astype(BF16)
    acc_scr[...] += jnp.dot(act, w2, preferred_element_type=F32)


def _ffn_stream_specs(w13, w2):
    d = w13.shape[0]
    n_chunks = D_FF // FF_CHUNK
    ins = [pl.BlockSpec((d, FF_CHUNK), lambda j: (0, j)),
           pl.BlockSpec((d, FF_CHUNK), lambda j: (0, n_chunks + j)),
           pl.BlockSpec((FF_CHUNK, d), lambda j: (j, 0))]
    outs = [pl.BlockSpec((d, FF_CHUNK), lambda j: (0, j)),
            pl.BlockSpec((d, FF_CHUNK), lambda j: (0, j)),
            pl.BlockSpec((FF_CHUNK, d), lambda j: (j, 0))]
    shapes = [jax.ShapeDtypeStruct((d, D_FF), BF16), jax.ShapeDtypeStruct((d, D_FF), BF16),
              jax.ShapeDtypeStruct((D_FF, d), BF16)]
    return ins, outs, shapes


def _ffn_stream_kernel(x_ref, sh_ref, sc_ref, g_ref, nw_ref, wg_ref, wu_ref, w2_ref,
                       o_ref, wgb_ref, wub_ref, w2b_ref, h_scr, acc_scr):
    j = pl.program_id(0)

    @pl.when(j == 0)
    def _():
        h_scr[...] = _rms_mod(x_ref[...], nw_ref[...], sc_ref[0], sh_ref[0]).astype(BF16)
        acc_scr[...] = jnp.zeros(acc_scr.shape, F32)

    _ffn_stream_step(h_scr, acc_scr, wg_ref, wu_ref, w2_ref, wgb_ref, wub_ref, w2b_ref)

    @pl.when(j == pl.num_programs(0) - 1)
    def _():
        o_ref[...] = x_ref[...] + (HALF_STEP * g_ref[0]) * acc_scr[...]


def _ffn_stream(x, mod, rows, pieces, nw, w13, w2, name):
    sh, sc, g = pieces
    assert rows.grid == (1,)
    const = lambda shape: pl.BlockSpec(shape, lambda j: (0,) * len(shape))
    w_ins, w_outs, w_shapes = _ffn_stream_specs(w13, w2)
    mod_spec = lambda p: pl.BlockSpec(rows.mod_block, lambda j: (0, 0, p))
    return pl.pallas_call(
        _ffn_stream_kernel,
        grid=(D_FF // FF_CHUNK,),
        in_specs=[const((rows.tm, D_MODEL)), mod_spec(sh), mod_spec(sc), mod_spec(g),
                  const((1, D_MODEL))] + w_ins,
        out_specs=[const((rows.tm, D_MODEL))] + w_outs,
        out_shape=[jax.ShapeDtypeStruct((rows.n_rows, D_MODEL), F32)] + w_shapes,
        scratch_shapes=[pltpu.VMEM((rows.tm, D_MODEL), BF16), pltpu.VMEM((rows.tm, D_MODEL), F32)],
        compiler_params=_cparams("arbitrary"),
        name=name,
    )(x, mod, mod, mod, nw, w13, w13, w2)


IN_A_PIECES = (CONV_DIM, DN_V_W)
IN_B_PIECES = (SWA_Q_W, 2 * SWA_KV_W, 2 * D_MODEL)
IN_OUT_DTYPES = (F32, BF16, BF16, F32, BF16, F32)


def _split_w_in_kernel(w_ref, wa_ref, wb_ref, wab_ref):
    n_a = sum(IN_A_PIECES)
    n_ab = 2 * DN_HEADS
    wa_ref[...] = w_ref[:, :n_a].astype(BF16)
    wb_ref[...] = w_ref[:, n_a + n_ab:].astype(BF16)
    lane = lax.broadcasted_iota(jnp.int32, wab_ref.shape, 1)
    wab_ref[...] = jnp.where(lane < n_ab, w_ref[:, n_a:n_a + LANES], 0.0)


def _split_w_in(w_in):
    d, n = w_in.shape
    n_a, n_b = sum(IN_A_PIECES), sum(IN_B_PIECES)
    assert n == n_a + 2 * DN_HEADS + n_b
    tr = LANES
    return pl.pallas_call(
        _split_w_in_kernel,
        grid=(d // tr,),
        in_specs=[pl.BlockSpec((tr, n), lambda i: (i, 0))],
        out_specs=[pl.BlockSpec((tr, n_a), lambda i: (i, 0)), pl.BlockSpec((tr, n_b), lambda i: (i, 0)),
                   pl.BlockSpec((tr, LANES), lambda i: (i, 0))],
        out_shape=[jax.ShapeDtypeStruct((d, n_a), BF16), jax.ShapeDtypeStruct((d, n_b), BF16),
                   jax.ShapeDtypeStruct((d, LANES), F32)],
        compiler_params=_cparams("parallel"),
        name="split_w_in",
    )(w_in)


def _inproj_kernel(x_ref, sh_ref, sc_ref, nw_ref, wa_ref, wb_ref, wab_ref,
                   qkv_ref, z_ref, qsw_ref, kv_ref, gates_ref, ab_ref):
    h = _rms_mod(x_ref[...], nw_ref[...], sc_ref[0], sh_ref[0])
    hb = h.astype(BF16)

    def project(w_ref, off, width, ref, act=None):
        for c0 in range(0, width, IN_COL_CHUNK):
            cw = min(IN_COL_CHUNK, width - c0)
            val = jnp.dot(hb, w_ref[:, off + c0:off + c0 + cw], preferred_element_type=F32)
            ref[:, c0:c0 + cw] = (val if act is None else act(val)).astype(ref.dtype)

    project(wa_ref, 0, CONV_DIM, qkv_ref)
    project(wa_ref, CONV_DIM, DN_V_W, z_ref, _silu)
    off = 0
    for ref, width in zip((qsw_ref, kv_ref, gates_ref), IN_B_PIECES):
        project(wb_ref, off, width, ref)
        off += width
    ab_ref[...] = _dot3(h, wab_ref[...])


def _inproj(x, mod, rows, pieces, nw, w_a, w_b, w_ab, name):
    sh, sc = pieces
    widths = IN_A_PIECES + IN_B_PIECES + (LANES,)
    return pl.pallas_call(
        _inproj_kernel,
        grid=rows.grid,
        in_specs=[rows.row_spec(D_MODEL), rows.mod_spec(sh), rows.mod_spec(sc),
                  _resident((1, D_MODEL)), _resident(w_a.shape), _resident(w_b.shape),
                  _resident(w_ab.shape)],
        out_specs=[rows.row_spec(w) for w in widths],
        out_shape=[jax.ShapeDtypeStruct((rows.n_rows, w), dt) for w, dt in zip(widths, IN_OUT_DTYPES)],
        compiler_params=_cparams("parallel"),
        name=name,
    )(x, mod, mod, nw, w_a, w_b, w_ab)


def _unit_lower_inverse(mats, row, col):
    n = mats[0].shape[0]
    eye = jnp.where(row == col, 1.0, 0.0).astype(F32)
    in_block = (row >> 4) == (col >> 4)
    p = [jnp.where(in_block, -a, 0.0) for a in mats]
    r = [eye + pi for pi in p]
    q = [_dot(pi, pi) for pi in p]
    for _ in range(2):
        rq = [_dot(qi, jnp.concatenate([ri, qi], axis=1)) for qi, ri in zip(q, r)]
        r = [ri + rqi[:, :n] for ri, rqi in zip(r, rq)]
        q = [rqi[:, n:] for rqi in rq]
    x = [ri + _dot(qi, ri) for qi, ri in zip(q, r)]
    s = 4
    while (1 << s) < n:
        pair = ((row >> (s + 1)) == (col >> (s + 1))) & ((row >> s) > (col >> s))
        t = [_dot(jnp.where(pair, a, 0.0), xi) for a, xi in zip(mats, x)]
        x = [xi - _dot(xi, ti) for xi, ti in zip(x, t)]
        s += 1
    return x


def _gdn_prompt_kernel(qkv_ref, z_ref, ab_ref, cw_ref, alog_ref, dtb_ref, dnw_ref,
                       y_ref, s_out_ref, tail_ref, halo_ref, s_ref):
    t = pl.program_id(1)
    c = GDN_CHUNK
    rows = qkv_ref.shape[0]
    halo = SUBLANES
    heads = range(DN_HEADS)

    @pl.when(t == 0)
    def _():
        s_ref[...] = jnp.zeros(s_ref.shape, F32)
        halo_ref[...] = jnp.zeros(halo_ref.shape, F32)

    row = lax.broadcasted_iota(jnp.int32, (c, c), 0)
    col = lax.broadcasted_iota(jnp.int32, (c, c), 1)
    lower = row >= col
    strict = row > col
    tri = jnp.where(lower, 1.0, 0.0).astype(BF16)

    def l2n(a, scale=1.0):
        return a * (lax.rsqrt(jnp.sum(a * a, axis=-1, keepdims=True) + EPS) * scale)

    for ci in range(rows // c):
        r0 = ci * c

        def conv_silu(c0):
            cols = slice(c0, c0 + LANES)
            cur = qkv_ref[r0:r0 + c, cols]
            y = cur * cw_ref[CONV_W - 1:CONV_W, cols]
            if ci == 0:
                ext = jnp.concatenate([halo_ref[:, cols], cur], axis=0)
            for j in range(CONV_W - 1):
                lag = CONV_W - 1 - j
                if ci == 0:
                    shifted = ext[halo - lag:halo - lag + c]
                else:
                    shifted = qkv_ref[pl.ds(r0 - lag, c), cols]
                y = y + shifted * cw_ref[j:j + 1, cols]
            return _silu(y)

        ab = ab_ref[r0:r0 + c, :]
        g_log = -jnp.exp(alog_ref[...]) * jax.nn.softplus(ab + dtb_ref[...])
        beta_all = jax.nn.sigmoid(ab)
        gc = sum(jnp.dot(tri, piece, preferred_element_type=F32) for piece in _split3(g_log))
        gc_t = gc.T
        gc_last = gc[c - 1:c, :]

        qn = [l2n(conv_silu(h * DN_DK), DN_DK ** -0.5) for h in heads]
        kn = [l2n(conv_silu(DN_QK_W + h * DN_DK)) for h in heads]
        v = [conv_silu(2 * DN_QK_W + h * DN_DV) for h in heads]
        g_col = [gc[:, h:h + 1] for h in heads]
        g_end = [gc_last[:, h:h + 1] for h in heads]
        beta = [beta_all[:, DN_HEADS + h:DN_HEADS + h + 1] for h in heads]
        decay = [jnp.where(lower, jnp.exp(g_col[h] - gc_t[h:h + 1, :]), 0.0) for h in heads]
        kb = [kn[h] * beta[h] for h in heads]
        e_col = [jnp.exp(g_col[h]) for h in heads]

        kq = [_dot_nt(jnp.concatenate([kb[h], qn[h]], axis=0), kn[h]) for h in heads]
        a_mat = [jnp.where(strict, kq[h][:c] * decay[h], 0.0) for h in heads]
        qk = [kq[h][c:] * decay[h] for h in heads]
        x_inv = _unit_lower_inverse(a_mat, row, col)
        uw = [_dot(x_inv[h], jnp.concatenate([v[h] * beta[h], kb[h] * e_col[h]], axis=1)) for h in heads]
        s_old = [s_ref[h] for h in heads]
        ws = [_dot(jnp.concatenate([uw[h][:, DN_DV:], qn[h] * e_col[h]], axis=0), s_old[h]) for h in heads]
        v_new = [uw[h][:, :DN_DV] - ws[h][:c] for h in heads]
        o = [ws[h][c:] + _dot(qk[h], v_new[h]) for h in heads]
        for h in heads:
            k_dec = kn[h] * jnp.exp(g_end[h] - g_col[h])
            s_ref[h] = s_old[h] * jnp.exp(g_end[h]) + _dot_tn(k_dec, v_new[h])
        for h in heads:
            oh = o[h] * lax.rsqrt(jnp.mean(o[h] * o[h], axis=-1, keepdims=True) + EPS) * dnw_ref[...]
            z_act = z_ref[r0:r0 + c, h * DN_DV:(h + 1) * DN_DV].astype(F32)
            y_ref[r0:r0 + c, h * DN_DV:(h + 1) * DN_DV] = (oh * z_act).astype(y_ref.dtype)

    halo_ref[...] = qkv_ref[rows - halo:rows, :]

    @pl.when(t == pl.num_programs(1) - 1)
    def _():
        s_out_ref[0] = s_ref[...]
        tail_ref[0] = qkv_ref[rows - halo:rows, :]


def _gdn_prompt(qkv, z, ab, conv_w, alog_pad, dtb_pad, dn_norm, n_batch, seq):
    rows = GDN_CHUNK * GDN_CHUNKS_PER_STEP
    assert seq % rows == 0
    nt = seq // rows
    row_spec = lambda w: pl.BlockSpec((rows, w), lambda b, t: (b * nt + t, 0))
    const = lambda shape: pl.BlockSpec(shape, lambda b, t: (0,) * len(shape))
    return pl.pallas_call(
        _gdn_prompt_kernel,
        grid=(n_batch, nt),
        in_specs=[row_spec(CONV_DIM), row_spec(DN_V_W), row_spec(LANES),
                  const((CONV_W, CONV_DIM)), const((1, LANES)), const((1, LANES)), const((1, DN_DV))],
        out_specs=[row_spec(DN_V_W),
                   pl.BlockSpec((1, DN_HEADS, DN_DK, DN_DV), lambda b, t: (b, 0, 0, 0)),
                   pl.BlockSpec((1, SUBLANES, CONV_DIM), lambda b, t: (b, 0, 0))],
        out_shape=[jax.ShapeDtypeStruct((n_batch * seq, DN_V_W), BF16),
                   jax.ShapeDtypeStruct((n_batch, DN_HEADS, DN_DK, DN_DV), F32),
                   jax.ShapeDtypeStruct((n_batch, SUBLANES, CONV_DIM), F32)],
        scratch_shapes=[pltpu.VMEM((SUBLANES, CONV_DIM), F32),
                        pltpu.VMEM((DN_HEADS, DN_DK, DN_DV), F32)],
        compiler_params=_cparams("parallel", "arbitrary"),
        name="gdn_prompt",
    )(qkv, z, ab, conv_w, alog_pad, dtb_pad, dn_norm)


def _kv_head_views(k2, v2, lo):
    k2r = pltpu.roll(k2, SWA_HD, axis=1)
    v2r = pltpu.roll(v2, SWA_HD, axis=1)
    k_lo = (jnp.where(lo, k2, 0.0), jnp.where(lo, k2r, 0.0))
    k_hi = (jnp.where(lo, 0.0, k2r), jnp.where(lo, 0.0, k2))
    v_dup = (jnp.where(lo, v2, v2r), jnp.where(lo, v2r, v2))
    return k_lo, k_hi, v_dup


def _swa_prompt_kernel(sink_ref, q_ref, kvc_ref, kvp_ref, o_ref):
    n = pl.program_id(1)
    blk = SWA_BLOCK
    n_sub = q_ref.shape[0] // blk
    lo = lax.broadcasted_iota(jnp.int32, (blk, LANES), 1) < SWA_HD
    own = (lax.broadcasted_iota(jnp.int32, (blk, blk), 0)
           >= lax.broadcasted_iota(jnp.int32, (blk, blk), 1))
    pairs = SWA_Q_HEADS // 2
    pairs_per_kv = pairs // SWA_KV_HEADS
    views = [_kv_head_views(kvp_ref[:, :LANES], kvp_ref[:, LANES:], lo)]
    for sub in range(n_sub):
        rs = slice(sub * blk, (sub + 1) * blk)
        views.append(_kv_head_views(kvc_ref[rs, :LANES], kvc_ref[rs, LANES:], lo))
    k_cat = {(sub, c): jnp.concatenate([views[1 + sub][0][c], views[1 + sub][1][c],
                                        views[sub][0][c], views[sub][1][c]], axis=0).astype(BF16)
             for sub in range(n_sub) for c in range(SWA_KV_HEADS)}
    v_cat = {(sub, c): jnp.concatenate([views[1 + sub][2][c], views[sub][2][c]], axis=0).astype(BF16)
             for sub in range(n_sub) for c in range(SWA_KV_HEADS)}
    groups = [(sub, c) for sub in range(n_sub) for c in range(SWA_KV_HEADS)]

    def score_stage(sub, c):
        return [_dot_nt(q_ref[sub * blk:(sub + 1) * blk, j * LANES:(j + 1) * LANES]
                        * (SWA_HD ** -0.5),
                        k_cat[sub, c])
                for j in range(c * pairs_per_kv, (c + 1) * pairs_per_kv)]

    def finish(sub, c, scores):
        heads = [(jj, half) for jj in range(pairs_per_kv) for half in range(2)]
        p, den = [], []
        for jj, half in heads:
            s_own = scores[jj][:, half * blk:(half + 1) * blk]
            s_prev = scores[jj][:, (2 + half) * blk:(3 + half) * blk]
            if sub == 0:
                s_prev = jnp.where(n > 0, s_prev, MASK_VALUE)
            s = jnp.where(own, s_own, s_prev)
            sink = sink_ref[2 * (c * pairs_per_kv + jj) + half]
            m = jnp.maximum(jnp.max(s, axis=-1, keepdims=True), sink)
            e = jnp.exp(s - m)
            p.append(jnp.concatenate([jnp.where(own, e, 0.0), jnp.where(own, 0.0, e)], axis=1))
            den.append(jnp.sum(e, axis=-1, keepdims=True) + jnp.exp(sink - m))
        out = [_dot(p[i], v_cat[sub, c]) / den[i] for i in range(len(heads))]
        for jj in range(pairs_per_kv):
            j = c * pairs_per_kv + jj
            o_ref[sub * blk:(sub + 1) * blk, j * LANES:(j + 1) * LANES] = (
                jnp.where(lo, out[2 * jj], out[2 * jj + 1]).astype(o_ref.dtype))

    pending = score_stage(*groups[0])
    for g, (sub, c) in enumerate(groups):
        nxt = score_stage(*groups[g + 1]) if g + 1 < len(groups) else None
        finish(sub, c, pending)
        pending = nxt


def _swa_prompt(q, kv, sinks, n_batch, seq):
    blk = SWA_BLOCK
    n_sub = SWA_BLOCKS_PER_STEP
    assert seq % (blk * n_sub) == 0
    ns = seq // (blk * n_sub)
    step_rows = lambda w: pl.BlockSpec((blk * n_sub, w), lambda b, n: (b * ns + n, 0))
    return pl.pallas_call(
        _swa_prompt_kernel,
        grid=(n_batch, ns),
        in_specs=[pl.BlockSpec(memory_space=pltpu.SMEM),
                  step_rows(SWA_Q_W), step_rows(2 * SWA_KV_W),
                  pl.BlockSpec((blk, 2 * SWA_KV_W),
                               lambda b, n: ((b * ns + n) * n_sub - jnp.minimum(n, 1), 0))],
        out_specs=step_rows(SWA_Q_W),
        out_shape=jax.ShapeDtypeStruct((n_batch * seq, SWA_Q_W), BF16),
        compiler_params=_cparams("parallel", "arbitrary"),
        name="swa_prompt",
    )(sinks, q, kv, kv)


def _mix(x, ydn, ysw, gates, g2, wbd_ref, wbs_ref, wout_ref):
    a = _dot(ydn, wbd_ref[...])
    b = _dot(ysw, wbs_ref[...])
    merged = (jax.nn.sigmoid(gates[:, :D_MODEL].astype(F32)) * a
              + jax.nn.sigmoid(gates[:, D_MODEL:].astype(F32)) * b)
    return x + g2 * _dot(merged, wout_ref[...])


def _final_norm(y, fw):
    return y * lax.rsqrt(jnp.mean(y * y, axis=-1, keepdims=True) + EPS) * fw


def _mix_ffn_kernel(x_ref, ydn_ref, ysw_ref, gates_ref, g2_ref, sh_ref, sc_ref, g3_ref, nw_ref,
                    wbd_ref, wbs_ref, wout_ref, wg_ref, wu_ref, w2_ref, fw_ref, o_ref, *, final):
    x = _mix(x_ref[...], ydn_ref[...], ysw_ref[...], gates_ref[...], g2_ref[0], wbd_ref, wbs_ref, wout_ref)
    y = _ffn_half_step(x, nw_ref[...], sc_ref[0], sh_ref[0], g3_ref[0], wg_ref, wu_ref, w2_ref)
    o_ref[...] = _final_norm(y, fw_ref[...]) if final else y


def _mix_ffn(x, y_dn, y_sw, gates, mod, rows, w, ffn_w, final_w, final, name):
    weights = (w['w_br_dn'], w['w_br_swa'], w['w_out']) + tuple(ffn_w)
    return pl.pallas_call(
        functools.partial(_mix_ffn_kernel, final=final),
        grid=rows.grid,
        in_specs=[rows.row_spec(D_MODEL), rows.row_spec(DN_V_W), rows.row_spec(SWA_Q_W),
                  rows.row_spec(2 * D_MODEL), rows.mod_spec(5), rows.mod_spec(6), rows.mod_spec(7),
                  rows.mod_spec(8), _resident((1, D_MODEL))]
                 + [_resident(a.shape) for a in weights] + [_resident((1, D_MODEL))],
        out_specs=rows.row_spec(D_MODEL),
        out_shape=jax.ShapeDtypeStruct((rows.n_rows, D_MODEL), F32),
        compiler_params=_cparams("parallel"),
        name=name,
    )(x, y_dn, y_sw, gates, mod, mod, mod, mod, w['norm_ffn2'], *weights, final_w)


def _mix_ffn_stream_kernel(x_ref, ydn_ref, ysw_ref, gates_ref, g2_ref, sh_ref, sc_ref, g3_ref, nw_ref,
                           wbd_ref, wbs_ref, wout_ref, fw_ref, wg_ref, wu_ref, w2_ref,
                           o_ref, wgb_ref, wub_ref, w2b_ref, x_scr, h_scr, acc_scr, *, final):
    j = pl.program_id(0)

    @pl.when(j == 0)
    def _():
        x = _mix(x_ref[...], ydn_ref[...], ysw_ref[...], gates_ref[...], g2_ref[0],
                 wbd_ref, wbs_ref, wout_ref)
        x_scr[...] = x
        h_scr[...] = _rms_mod(x, nw_ref[...], sc_ref[0], sh_ref[0]).astype(BF16)
        acc_scr[...] = jnp.zeros(acc_scr.shape, F32)

    _ffn_stream_step(h_scr, acc_scr, wg_ref, wu_ref, w2_ref, wgb_ref, wub_ref, w2b_ref)

    @pl.when(j == pl.num_programs(0) - 1)
    def _():
        y = x_scr[...] + (HALF_STEP * g3_ref[0]) * acc_scr[...]
        o_ref[...] = _final_norm(y, fw_ref[...]) if final else y


def _mix_ffn_stream(x, y_dn, y_sw, gates, mod, rows, w, w13, w2, final_w, final, name):
    assert rows.grid == (1,)
    const = lambda shape: pl.BlockSpec(shape, lambda j: (0,) * len(shape))
    mod_spec = lambda p: pl.BlockSpec(rows.mod_block, lambda j: (0, 0, p))
    mix_w = (w['w_br_dn'], w['w_br_swa'], w['w_out'])
    w_ins, w_outs, w_shapes = _ffn_stream_specs(w13, w2)
    tm = rows.tm
    return pl.pallas_call(
        functools.partial(_mix_ffn_stream_kernel, final=final),
        grid=(D_FF // FF_CHUNK,),
        in_specs=[const((tm, D_MODEL)), const((tm, DN_V_W)), const((tm, SWA_Q_W)), const((tm, 2 * D_MODEL)),
                  mod_spec(5), mod_spec(6), mod_spec(7), mod_spec(8), const((1, D_MODEL))]
                 + [const(a.shape) for a in mix_w] + [const((1, D_MODEL))] + w_ins,
        out_specs=[const((tm, D_MODEL))] + w_outs,
        out_shape=[jax.ShapeDtypeStruct((rows.n_rows, D_MODEL), F32)] + w_shapes,
        scratch_shapes=[pltpu.VMEM((tm, D_MODEL), F32), pltpu.VMEM((tm, D_MODEL), BF16),
                        pltpu.VMEM((tm, D_MODEL), F32)],
        compiler_params=_cparams("arbitrary"),
        name=name,
    )(x, y_dn, y_sw, gates, mod, mod, mod, mod, w['norm_ffn2'], *mix_w, final_w, w13, w13, w2)


def _gdn_step_prep_kernel(qkv_ref, cs_ref, ab_ref, cw_ref, alog_ref, dtb_ref,
                          cs_out_ref, qt_ref, kt_ref, v_ref, dec_ref, beta_ref):
    nb = qkv_ref.shape[0]
    for j in range(CONV_W - 2):
        cs_out_ref[:, j, :] = cs_ref[:, j + 1, :]
    cs_out_ref[:, CONV_W - 2, :] = qkv_ref[...]
    ab = ab_ref[...]
    dec = jnp.exp(-jnp.exp(alog_ref[...]) * jax.nn.softplus(ab + dtb_ref[...]))
    beta = jax.nn.sigmoid(ab)

    def conv_silu(c0):
        cols = slice(c0, c0 + LANES)
        y = qkv_ref[:, cols] * cw_ref[CONV_W - 1:CONV_W, cols]
        for j in range(CONV_W - 1):
            y = y + cs_ref[:, j, cols] * cw_ref[j:j + 1, cols]
        return _silu(y)

    for h in range(DN_HEADS):
        q = conv_silu(h * DN_DK)
        k = conv_silu(DN_QK_W + h * DN_DK)
        qn = q * lax.rsqrt(jnp.sum(q * q, axis=-1, keepdims=True) + EPS) * (DN_DK ** -0.5)
        kn = k * lax.rsqrt(jnp.sum(k * k, axis=-1, keepdims=True) + EPS)
        qt_ref[h] = qn.T
        kt_ref[h] = kn.T
        sl = slice(h * DN_DV, (h + 1) * DN_DV)
        v_ref[:, sl] = conv_silu(2 * DN_QK_W + h * DN_DV)
        dec_ref[:, sl] = jnp.broadcast_to(dec[:, h:h + 1], (nb, DN_DV))
        beta_ref[:, sl] = jnp.broadcast_to(beta[:, DN_HEADS + h:DN_HEADS + h + 1], (nb, DN_DV))


def _gdn_step_prep(qkv, conv_state, ab, conv_w, alog_pad, dtb_pad):
    nb = qkv.shape[0]
    full = _resident
    return pl.pallas_call(
        _gdn_step_prep_kernel,
        grid=(1,),
        in_specs=[full((nb, CONV_DIM)), full(conv_state.shape), full((nb, LANES)),
                  full((CONV_W, CONV_DIM)), full((1, LANES)), full((1, LANES))],
        out_specs=[full(conv_state.shape), full((DN_HEADS, DN_DK, nb)), full((DN_HEADS, DN_DK, nb)),
                   full((nb, DN_V_W)), full((nb, DN_V_W)), full((nb, DN_V_W))],
        out_shape=[jax.ShapeDtypeStruct(conv_state.shape, F32),
                   jax.ShapeDtypeStruct((DN_HEADS, DN_DK, nb), F32),
                   jax.ShapeDtypeStruct((DN_HEADS, DN_DK, nb), F32),
                   jax.ShapeDtypeStruct((nb, DN_V_W), F32),
                   jax.ShapeDtypeStruct((nb, DN_V_W), F32),
                   jax.ShapeDtypeStruct((nb, DN_V_W), F32)],
        compiler_params=_cparams("arbitrary"),
        name="gdn_step_prep",
    )(qkv, conv_state, ab, conv_w, alog_pad, dtb_pad)


def _gdn_step_kernel(s_ref, qt_ref, kt_ref, v_ref, dec_ref, beta_ref, z_ref, dnw_ref,
                     s_out_ref, y_ref, o_scr):
    nb = s_ref.shape[0]
    qt = qt_ref[0]
    kt = kt_ref[0]
    for b in range(nb):
        k_col = kt[:, b:b + 1]
        q_col = qt[:, b:b + 1]
        s1 = s_ref[b, 0] * dec_ref[b:b + 1, :]
        kv = jnp.sum(s1 * k_col, axis=0, keepdims=True)
        delta = (v_ref[b:b + 1, :] - kv) * beta_ref[b:b + 1, :]
        s2 = s1 + k_col * delta
        s_out_ref[b, 0] = s2
        o_scr[b:b + 1, :] = jnp.sum(s2 * q_col, axis=0, keepdims=True)
    o = o_scr[...]
    o = o * lax.rsqrt(jnp.mean(o * o, axis=-1, keepdims=True) + EPS) * dnw_ref[...]
    y_ref[...] = (o * z_ref[...].astype(F32)).astype(y_ref.dtype)


def _gdn_step(state, qt, kt, v, dec, beta, z, dn_norm):
    nb = state.shape[0]
    head_cols = pl.BlockSpec((nb, DN_DV), lambda h: (0, h))
    head_t = pl.BlockSpec((1, DN_DK, nb), lambda h: (h, 0, 0))
    s_spec = pl.BlockSpec((nb, 1, DN_DK, DN_DV), lambda h: (0, h, 0, 0))
    return pl.pallas_call(
        _gdn_step_kernel,
        grid=(DN_HEADS,),
        in_specs=[s_spec, head_t, head_t, head_cols, head_cols, head_cols, head_cols,
                  pl.BlockSpec((1, DN_DV), lambda h: (0, 0))],
        out_specs=[s_spec, head_cols],
        out_shape=[jax.ShapeDtypeStruct(state.shape, F32),
                   jax.ShapeDtypeStruct((nb, DN_V_W), BF16)],
        scratch_shapes=[pltpu.VMEM((nb, DN_DV), F32)],
        compiler_params=_cparams("parallel"),
        name="gdn_step",
    )(state, qt, kt, v, dec, beta, z, dn_norm)


SWA_STEP_BATCH = 16


def _swa_step_kernel(q_ref, kvn_ref, ck_ref, cv_ref, slo_ref, shi_ref, o_ref, ck_out_ref, cv_out_ref):
    length = ck_ref.shape[1]
    last = lax.broadcasted_iota(jnp.int32, (length, LANES), 0) == length - 1
    lo_k = lax.broadcasted_iota(jnp.int32, (length, LANES), 1) < SWA_HD
    pairs = SWA_Q_HEADS // 2
    first_kv = lax.broadcasted_iota(jnp.int32, (pairs, LANES), 0) < pairs // SWA_KV_HEADS
    lo_o = lax.broadcasted_iota(jnp.int32, (pairs, LANES), 1) < SWA_HD
    samples = range(q_ref.shape[0])
    k2 = [jnp.where(last, kvn_ref[b:b + 1, 0:LANES], pltpu.roll(ck_ref[b], length - 1, axis=0))
          for b in samples]
    v2 = [jnp.where(last, kvn_ref[b:b + 1, LANES:2 * LANES], pltpu.roll(cv_ref[b], length - 1, axis=0))
          for b in samples]
    for b in samples:
        ck_out_ref[b] = k2[b]
        cv_out_ref[b] = v2[b]
    views = [_kv_head_views(k2[b], v2[b], lo_k) for b in samples]
    k_cat = [jnp.concatenate([views[b][0][0], views[b][0][1], views[b][1][0], views[b][1][1]], axis=0)
             for b in samples]
    v_cat = [jnp.concatenate([views[b][2][0], views[b][2][1]], axis=1) for b in samples]
    scores = [_dot_nt(q_ref[b] * (SWA_HD ** -0.5), k_cat[b]) for b in samples]
    p, den = [], []
    for b in samples:
        halves_p, halves_den = [], []
        for half, sink_ref in enumerate((slo_ref, shi_ref)):
            s = jnp.where(first_kv, scores[b][:, 2 * half * length:(2 * half + 1) * length],
                          scores[b][:, (2 * half + 1) * length:(2 * half + 2) * length])
            sink = sink_ref[...]
            m = jnp.maximum(jnp.max(s, axis=-1, keepdims=True), sink)
            e = jnp.exp(s - m)
            halves_p.append(e)
            halves_den.append(jnp.sum(e, axis=-1, keepdims=True) + jnp.exp(sink - m))
        p.append(jnp.concatenate(halves_p, axis=0))
        den.append(halves_den)
    pv = [_dot(p[b], v_cat[b]) for b in samples]
    for b in samples:
        halves = [jnp.where(first_kv, pv[b][half * pairs:(half + 1) * pairs, :LANES],
                            pv[b][half * pairs:(half + 1) * pairs, LANES:]) / den[b][half]
                  for half in range(2)]
        o_ref[b] = jnp.where(lo_o, halves[0], halves[1]).astype(o_ref.dtype)


def _swa_step(q3, kv_new, cache_k, cache_v, sink_lo, sink_hi):
    nb, length, _ = cache_k.shape
    tb = SWA_STEP_BATCH
    pairs = SWA_Q_HEADS // 2
    q_spec = pl.BlockSpec((tb, pairs, LANES), lambda i: (i, 0, 0))
    c_spec = pl.BlockSpec((tb, length, LANES), lambda i: (i, 0, 0))
    sink_spec = pl.BlockSpec((pairs, LANES), lambda i: (0, 0))
    return pl.pallas_call(
        _swa_step_kernel,
        grid=(nb // tb,),
        in_specs=[q_spec, pl.BlockSpec((tb, 2 * SWA_KV_W), lambda i: (i, 0)), c_spec, c_spec,
                  sink_spec, sink_spec],
        out_specs=[q_spec, c_spec, c_spec],
        out_shape=[jax.ShapeDtypeStruct(q3.shape, BF16),
                   jax.ShapeDtypeStruct(cache_k.shape, F32),
                   jax.ShapeDtypeStruct(cache_v.shape, F32)],
        compiler_params=_cparams("parallel"),
        name="swa_step",
    )(q3, kv_new, cache_k, cache_v, sink_lo, sink_hi)


def _prep_layer_weights(lp):
    w_a, w_b, w_ab = _split_w_in(lp['w_in'])
    pad_heads = lambda a: jnp.pad(a.astype(F32), (0, LANES - DN_HEADS)).reshape(1, LANES)
    sinks = lp['sinks'].astype(F32)
    pairs = SWA_Q_HEADS // 2
    return dict(
        w13_ffn1=lp['w13_ffn1'], w2_ffn1=lp['w2_ffn1'], w13_ffn2=lp['w13_ffn2'], w2_ffn2=lp['w2_ffn2'],
        w_a=w_a, w_b=w_b, w_ab=w_ab,
        w_br_dn=lp['w_br_dn'].astype(BF16), w_br_swa=lp['w_br_swa'].astype(BF16),
        w_out=lp['w_out'].astype(BF16),
        norm_ffn1=lp['norm_ffn1'].reshape(1, D_MODEL), norm_mix=lp['norm_mix'].reshape(1, D_MODEL),
        norm_ffn2=lp['norm_ffn2'].reshape(1, D_MODEL),
        conv_w=lp['conv_w'], alog_pad=pad_heads(lp['a_log']), dtb_pad=pad_heads(lp['dt_bias']),
        dn_norm=lp['dn_norm'].reshape(1, DN_DV), sinks=sinks,
        sink_lo=jnp.broadcast_to(sinks.reshape(pairs, 2)[:, 0:1], (pairs, LANES)),
        sink_hi=jnp.broadcast_to(sinks.reshape(pairs, 2)[:, 1:2], (pairs, LANES)),
    )


def _layer(x, mod, rows, w, final_w, final, past, ffn_bf16, n_batch, seq, tag):
    if past is None:
        x = _ffn(x, mod, rows, (0, 1, 2), w['norm_ffn1'], ffn_bf16[0], "ffn1_" + tag)
    else:
        x, *ffn1_b = _ffn_stream(x, mod, rows, (0, 1, 2), w['norm_ffn1'], w['w13_ffn1'], w['w2_ffn1'],
                                 "ffn1_" + tag)
    if past is None:
        in_rows = _Rows(n_batch, seq, INPROJ_ROW_TILE)
        qkv, z, q_sw, kv, gates, ab = _inproj(
            x, mod, in_rows, (3, 4), w['norm_mix'], w['w_a'], w['w_b'], w['w_ab'], "inproj_" + tag)
        y_dn, s_new, tail = _gdn_prompt(qkv, z, ab, w['conv_w'], w['alog_pad'], w['dtb_pad'],
                                        w['dn_norm'], n_batch, seq)
        y_sw = _swa_prompt(q_sw, kv, w['sinks'], n_batch, seq)
        conv_new = tail[:, SUBLANES - (CONV_W - 1):]
        keep = min(WINDOW, seq)
        kv3 = kv.reshape(n_batch, seq, 2 * SWA_KV_W)[:, seq - keep:]
        k_buf = kv3[:, :, :SWA_KV_W].reshape(n_batch, keep, SWA_KV_HEADS, SWA_HD)
        v_buf = kv3[:, :, SWA_KV_W:].reshape(n_batch, keep, SWA_KV_HEADS, SWA_HD)
    else:
        s0, conv_buf, k_old, v_old = past
        length = k_old.shape[1]
        qkv, z, q_sw, kv, gates, ab = _inproj(
            x, mod, rows, (3, 4), w['norm_mix'], w['w_a'], w['w_b'], w['w_ab'], "inproj_" + tag)
        conv_new, qt, kt, v, dec, beta = _gdn_step_prep(qkv, conv_buf, ab, w['conv_w'],
                                                        w['alog_pad'], w['dtb_pad'])
        s_new, y_dn = _gdn_step(s0, qt, kt, v, dec, beta, z, w['dn_norm'])
        o3, k_buf, v_buf = _swa_step(q_sw.reshape(n_batch, SWA_Q_HEADS // 2, LANES), kv,
                                     k_old.reshape(n_batch, length, SWA_KV_W),
                                     v_old.reshape(n_batch, length, SWA_KV_W),
                                     w['sink_lo'], w['sink_hi'])
        y_sw = o3.reshape(n_batch, SWA_Q_W)
        k_buf = k_buf.reshape(n_batch, length, SWA_KV_HEADS, SWA_HD)
        v_buf = v_buf.reshape(n_batch, length, SWA_KV_HEADS, SWA_HD)
    if past is None:
        x = _mix_ffn(x, y_dn, y_sw, gates, mod, rows, w, ffn_bf16[1], final_w, final, "mix_ffn2_" + tag)
        return x, (s_new, conv_new, k_buf, v_buf)
    x, *ffn2_b = _mix_ffn_stream(x, y_dn, y_sw, gates, mod, rows, w, w['w13_ffn2'], w['w2_ffn2'],
                                 final_w, final, "mix_ffn2_" + tag)
    return x, (s_new, conv_new, k_buf, v_buf), (ffn1_b, ffn2_b)


def kernel(x_prompt, x_sample, state_dn, state_conv, cache_swa_k, cache_swa_v, c_prompt, c_sample,
           w_ada, b_ada, norm_ffn1, w13_ffn1, w2_ffn1, norm_mix, w_in, conv_w, a_log, dt_bias,
           dn_norm, sinks, w_br_dn, w_br_swa, w_out, norm_ffn2, w13_ffn2, w2_ffn2, final_norm):
    n_p, seq_p, d = x_prompt.shape
    n_s, seq_s, _ = x_sample.shape
    depth = w_ada.shape[0]
    assert d == D_MODEL and seq_s == 1 and seq_p % ROW_TILE == 0 and seq_p % GDN_CHUNK == 0
    assert w_in.shape[2] == sum(IN_SPLITS) and w13_ffn1.shape[2] == 2 * D_FF
    assert cache_swa_k.shape[2] == WINDOW and n_s % SWA_STEP_BATCH == 0 and n_p % SUBLANES == 0
    rows_p = _Rows(n_p, seq_p, ROW_TILE)
    rows_s = _Rows(n_s, seq_s, ROW_TILE)
    c_all = jnp.concatenate([c_prompt, c_sample], axis=0)
    final_w = final_norm.reshape(1, D_MODEL)
    y_p = x_prompt.reshape(n_p * seq_p, d)
    y_s = x_sample.reshape(n_s * seq_s, d)
    st_p, st_s = [], []
    for l in range(depth):
        lp = dict(w_ada=w_ada[l], b_ada=b_ada[l], norm_ffn1=norm_ffn1[l], w13_ffn1=w13_ffn1[l],
                  w2_ffn1=w2_ffn1[l], norm_mix=norm_mix[l], w_in=w_in[l], conv_w=conv_w[l],
                  a_log=a_log[l], dt_bias=dt_bias[l], dn_norm=dn_norm[l], sinks=sinks[l],
                  w_br_dn=w_br_dn[l], w_br_swa=w_br_swa[l], w_out=w_out[l], norm_ffn2=norm_ffn2[l],
                  w13_ffn2=w13_ffn2[l], w2_ffn2=w2_ffn2[l])
        w = _prep_layer_weights(lp)
        mod_p, mod_s = _ada(c_all, lp['w_ada'], lp['b_ada'], n_p)
        final = l == depth - 1
        y_s, ss, ffn_bf16 = _layer(y_s, mod_s.reshape(rows_s.mod_shape), rows_s, w, final_w, final,
                                   (state_dn[l], state_conv[l], cache_swa_k[l], cache_swa_v[l]), None,
                                   n_s, seq_s, "sample")
        y_p, sp = _layer(y_p, mod_p.reshape(rows_p.mod_shape), rows_p, w, final_w, final, None, ffn_bf16,
                         n_p, seq_p, "prompt")
        st_p.append(sp)
        st_s.append(ss)
    stack = lambda sts, i: sts[0][i][None] if depth == 1 else jnp.stack([s[i] for s in sts])
    return (y_p.reshape(n_p, seq_p, d), y_s.reshape(n_s, seq_s, d),
            stack(st_p, 0), stack(st_s, 0), stack(st_p, 1), stack(st_s, 1),
            stack(st_p, 2), stack(st_s, 2), stack(st_p, 3), stack(st_s, 3))
```

```python
import functools

import jax
import jax.numpy as jnp
from jax import lax
from jax.experimental import pallas as pl
from jax.experimental.pallas import tpu as pltpu

F32 = jnp.float32
BF16 = jnp.bfloat16

D_MODEL = 1024
DN_HEADS = 8
DN_DK = 128
DN_DV = 128
DN_QK_W = DN_HEADS * DN_DK
DN_V_W = DN_HEADS * DN_DV
CONV_W = 4
CONV_DIM = 2 * DN_QK_W + DN_V_W
SWA_Q_HEADS = 16
SWA_KV_HEADS = 2
SWA_HD = 64
SWA_Q_W = SWA_Q_HEADS * SWA_HD
SWA_KV_W = SWA_KV_HEADS * SWA_HD
WINDOW = 128
D_FF = 2816
HALF_STEP = 0.5
N_MOD = 9
EPS = 1e-6
MASK_VALUE = -1e30
IN_SPLITS = (CONV_DIM, DN_V_W, DN_HEADS, DN_HEADS, SWA_Q_W, SWA_KV_W, SWA_KV_W, D_MODEL, D_MODEL)

LANES = 128
SUBLANES = 8
VMEM_LIMIT_BYTES = 56 * 1024 * 1024

GDN_CHUNK = 128
GDN_CHUNKS_PER_STEP = 4
SWA_BLOCK = 128
SWA_BLOCKS_PER_STEP = 2
FF_CHUNK = 256
ROW_TILE = 512
INPROJ_ROW_TILE = 512
IN_COL_CHUNK = 512


def _cparams(*sem):
    return pltpu.CompilerParams(dimension_semantics=sem, vmem_limit_bytes=VMEM_LIMIT_BYTES)


def _resident(shape):
    nd = len(shape)
    return pl.BlockSpec(shape, lambda *_: (0,) * nd, pipeline_mode=pl.Buffered(1))


def _dot(a, b):
    return jnp.dot(a.astype(BF16), b.astype(BF16), preferred_element_type=F32)


def _dot_nt(a, b):
    return lax.dot_general(a.astype(BF16), b.astype(BF16), (((1,), (1,)), ((), ())),
                           preferred_element_type=F32)


def _dot_tn(a, b):
    return lax.dot_general(a.astype(BF16), b.astype(BF16), (((0,), (0,)), ((), ())),
                           preferred_element_type=F32)


def _split3(a):
    hi = a.astype(BF16)
    r = a - hi.astype(F32)
    mid = r.astype(BF16)
    lo = (r - mid.astype(F32)).astype(BF16)
    return hi, mid, lo


def _dot3(a, b):
    a_hi = a.astype(BF16)
    a_lo = (a - a_hi.astype(F32)).astype(BF16)
    b_hi = b.astype(BF16)
    b_lo = (b - b_hi.astype(F32)).astype(BF16)
    d = functools.partial(jnp.dot, preferred_element_type=F32)
    return d(a_hi, b_hi) + (d(a_lo, b_hi) + d(a_hi, b_lo))


def _silu(x):
    return x * jax.nn.sigmoid(x)


def _rms_mod(x, nw, sc, sh):
    y = x * lax.rsqrt(jnp.mean(x * x, axis=-1, keepdims=True) + EPS)
    return (y * nw) * (1.0 + sc) + sh


def _ada_kernel(c_ref, w_ref, b_ref, op_ref, os_ref, *, n_prompt):
    m = _dot(_silu(c_ref[...]), w_ref[...]) + b_ref[...]
    op_ref[...] = m[:n_prompt]
    os_ref[...] = m[n_prompt:]


def _ada(c_all, w_ada, b_ada, n_prompt):
    n_all, d = c_all.shape
    n_out = w_ada.shape[1]
    tn = D_MODEL
    return pl.pallas_call(
        functools.partial(_ada_kernel, n_prompt=n_prompt),
        grid=(n_out // tn,),
        in_specs=[pl.BlockSpec((n_all, d), lambda j: (0, 0)),
                  pl.BlockSpec((d, tn), lambda j: (0, j)),
                  pl.BlockSpec((1, tn), lambda j: (0, j))],
        out_specs=[pl.BlockSpec((n_prompt, tn), lambda j: (0, j)),
                   pl.BlockSpec((n_all - n_prompt, tn), lambda j: (0, j))],
        out_shape=[jax.ShapeDtypeStruct((n_prompt, n_out), F32),
                   jax.ShapeDtypeStruct((n_all - n_prompt, n_out), F32)],
        compiler_params=_cparams("arbitrary"),
        name="ada",
    )(c_all, w_ada, b_ada.reshape(1, n_out))


class _Rows:
    def __init__(self, n_batch, seq, row_tile):
        self.n_batch = n_batch
        self.n_rows = n_batch * seq
        if seq == 1:
            self.tm = n_batch
            self.mod_shape = (1, n_batch, N_MOD * D_MODEL)
            self.mod_block = (1, n_batch, D_MODEL)
            self.tiles_per_batch = None
        else:
            self.tm = min(row_tile, seq)
            assert seq % self.tm == 0
            self.mod_shape = (n_batch, 1, N_MOD * D_MODEL)
            self.mod_block = (1, 1, D_MODEL)
            self.tiles_per_batch = seq // self.tm
        self.grid = (self.n_rows // self.tm,)

    def mod_spec(self, piece):
        if self.tiles_per_batch is None:
            return pl.BlockSpec(self.mod_block, lambda i: (0, 0, piece))
        tpb = self.tiles_per_batch
        return pl.BlockSpec(self.mod_block, lambda i: (i // tpb, 0, piece))

    def row_spec(self, width):
        return pl.BlockSpec((self.tm, width), lambda i: (i, 0))


def _ffn_half_step(x, nw, sc, sh, g, wg_ref, wu_ref, w2_ref):
    h = _rms_mod(x, nw, sc, sh).astype(BF16)
    acc = jnp.zeros(x.shape, F32)
    for c0 in range(0, D_FF, FF_CHUNK):
        gate = jnp.dot(h, wg_ref[:, c0:c0 + FF_CHUNK], preferred_element_type=F32)
        up = jnp.dot(h, wu_ref[:, c0:c0 + FF_CHUNK], preferred_element_type=F32)
        act = (_silu(gate) * up).astype(BF16)
        acc = acc + jnp.dot(act, w2_ref[c0:c0 + FF_CHUNK, :], preferred_element_type=F32)
    return x + (HALF_STEP * g) * acc


def _ffn_kernel(x_ref, sh_ref, sc_ref, g_ref, nw_ref, wg_ref, wu_ref, w2_ref, o_ref):
    o_ref[...] = _ffn_half_step(x_ref[...], nw_ref[...], sc_ref[0], sh_ref[0], g_ref[0],
                                wg_ref, wu_ref, w2_ref)


def _ffn(x, mod, rows, pieces, nw, ffn_w, name):
    sh, sc, g = pieces
    return pl.pallas_call(
        _ffn_kernel,
        grid=rows.grid,
        in_specs=[rows.row_spec(D_MODEL), rows.mod_spec(sh), rows.mod_spec(sc), rows.mod_spec(g),
                  _resident((1, D_MODEL))] + [_resident(a.shape) for a in ffn_w],
        out_specs=rows.row_spec(D_MODEL),
        out_shape=jax.ShapeDtypeStruct((rows.n_rows, D_MODEL), F32),
        compiler_params=_cparams("parallel"),
        name=name,
    )(x, mod, mod, mod, nw, *ffn_w)


def _ffn_stream_step(h_scr, acc_scr, wg_ref, wu_ref, w2_ref, wgb_ref, wub_ref, w2b_ref):
    wg = wg_ref[...].astype(BF16)
    wu = wu_ref[...].astype(BF16)
    w2 = w2_ref[...].astype(BF16)
    wgb_ref[...] = wg
    wub_ref[...] = wu
    w2b_ref[...] = w2
    h = h_scr[...]
    gate = jnp.dot(h, wg, preferred_element_type=F32)
    up = jnp.dot(h, wu, preferred_element_type=F32)
    act = (_silu(gate) * up).astype(BF16)
    acc_scr[...] += jnp.dot(act, w2, preferred_element_type=F32)


def _ffn_stream_specs(w13, w2):
    d = w13.shape[0]
    n_chunks = D_FF // FF_CHUNK
    ins = [pl.BlockSpec((d, FF_CHUNK), lambda j: (0, j)),
           pl.BlockSpec((d, FF_CHUNK), lambda j: (0, n_chunks + j)),
           pl.BlockSpec((FF_CHUNK, d), lambda j: (j, 0))]
    outs = [pl.BlockSpec((d, FF_CHUNK), lambda j: (0, j)),
            pl.BlockSpec((d, FF_CHUNK), lambda j: (0, j)),
            pl.BlockSpec((FF_CHUNK, d), lambda j: (j, 0))]
    shapes = [jax.ShapeDtypeStruct((d, D_FF), BF16), jax.ShapeDtypeStruct((d, D_FF), BF16),
              jax.ShapeDtypeStruct((D_FF, d), BF16)]
    return ins, outs, shapes


def _ffn_stream_kernel(x_ref, sh_ref, sc_ref, g_ref, nw_ref, wg_ref, wu_ref, w2_ref,
                       o_ref, wgb_ref, wub_ref, w2b_ref, h_scr, acc_scr):
    j = pl.program_id(0)

    @pl.when(j == 0)
    def _():
        h_scr[...] = _rms_mod(x_ref[...], nw_ref[...], sc_ref[0], sh_ref[0]).astype(BF16)
        acc_scr[...] = jnp.zeros(acc_scr.shape, F32)

    _ffn_stream_step(h_scr, acc_scr, wg_ref, wu_ref, w2_ref, wgb_ref, wub_ref, w2b_ref)

    @pl.when(j == pl.num_programs(0) - 1)
    def _():
        o_ref[...] = x_ref[...] + (HALF_STEP * g_ref[0]) * acc_scr[...]


def _ffn_stream(x, mod, rows, pieces, nw, w13, w2, name):
    sh, sc, g = pieces
    assert rows.grid == (1,)
    const = lambda shape: pl.BlockSpec(shape, lambda j: (0,) * len(shape))
    w_ins, w_outs, w_shapes = _ffn_stream_specs(w13, w2)
    mod_spec = lambda p: pl.BlockSpec(rows.mod_block, lambda j: (0, 0, p))
    return pl.pallas_call(
        _ffn_stream_kernel,
        grid=(D_FF // FF_CHUNK,),
        in_specs=[const((rows.tm, D_MODEL)), mod_spec(sh), mod_spec(sc), mod_spec(g),
                  const((1, D_MODEL))] + w_ins,
        out_specs=[const((rows.tm, D_MODEL))] + w_outs,
        out_shape=[jax.ShapeDtypeStruct((rows.n_rows, D_MODEL), F32)] + w_shapes,
        scratch_shapes=[pltpu.VMEM((rows.tm, D_MODEL), BF16), pltpu.VMEM((rows.tm, D_MODEL), F32)],
        compiler_params=_cparams("arbitrary"),
        name=name,
    )(x, mod, mod, mod, nw, w13, w13, w2)


IN_A_PIECES = (CONV_DIM, DN_V_W)
IN_B_PIECES = (SWA_Q_W, 2 * SWA_KV_W, 2 * D_MODEL)
IN_OUT_DTYPES = (F32, BF16, BF16, F32, BF16, F32)


N_AB = 2 * DN_HEADS


def _split_w_in_kernel(wa_src, wb_src0, wb_src1, wab_src, wa_ref, wb_ref, wab_ref):
    wa_ref[...] = wa_src[...].astype(BF16)
    wb_ref[...] = jnp.concatenate([wb_src0[N_AB:, :], wb_src1[:N_AB, :]], axis=0).astype(BF16)
    row = lax.broadcasted_iota(jnp.int32, wab_ref.shape, 0)
    wab_ref[...] = jnp.where(row < N_AB, wab_src[...], 0.0)


def _split_w_in(w_in_t):
    n, d = w_in_t.shape
    n_a, n_b = sum(IN_A_PIECES), sum(IN_B_PIECES)
    tr = LANES
    assert n == n_a + N_AB + n_b and n_a % tr == 0 and n_b % tr == 0 and N_AB % SUBLANES == 0
    steps_a, steps_b = n_a // tr, n_b // tr
    b_step = lambda i: jnp.minimum(i, steps_b - 1)
    return pl.pallas_call(
        _split_w_in_kernel,
        grid=(max(steps_a, steps_b),),
        in_specs=[pl.BlockSpec((tr, d), lambda i: (jnp.minimum(i, steps_a - 1), 0)),
                  pl.BlockSpec((tr, d), lambda i: (steps_a + b_step(i), 0)),
                  pl.BlockSpec((tr, d), lambda i: (steps_a + b_step(i) + 1, 0)),
                  pl.BlockSpec((tr, d), lambda i: (steps_a, 0))],
        out_specs=[pl.BlockSpec((tr, d), lambda i: (jnp.minimum(i, steps_a - 1), 0)),
                   pl.BlockSpec((tr, d), lambda i: (b_step(i), 0)),
                   pl.BlockSpec((tr, d), lambda i: (0, 0))],
        out_shape=[jax.ShapeDtypeStruct((n_a, d), BF16), jax.ShapeDtypeStruct((n_b, d), BF16),
                   jax.ShapeDtypeStruct((tr, d), F32)],
        compiler_params=_cparams("arbitrary"),
        name="split_w_in",
    )(w_in_t, w_in_t, w_in_t, w_in_t)


def _dot3_nt(a, b):
    a_hi = a.astype(BF16)
    a_lo = (a - a_hi.astype(F32)).astype(BF16)
    b_hi = b.astype(BF16)
    b_lo = (b - b_hi.astype(F32)).astype(BF16)
    return _dot_nt(a_hi, b_hi) + (_dot_nt(a_lo, b_hi) + _dot_nt(a_hi, b_lo))


def _inproj_kernel(x_ref, sh_ref, sc_ref, nw_ref, wa_ref, wb_ref, wab_ref,
                   qkv_ref, z_ref, qsw_ref, kv_ref, gates_ref, ab_ref):
    h = _rms_mod(x_ref[...], nw_ref[...], sc_ref[0], sh_ref[0])
    hb = h.astype(BF16)

    def project(w_ref, off, width, ref, act=None):
        for c0 in range(0, width, IN_COL_CHUNK):
            cw = min(IN_COL_CHUNK, width - c0)
            val = _dot_nt(hb, w_ref[off + c0:off + c0 + cw, :])
            ref[:, c0:c0 + cw] = (val if act is None else act(val)).astype(ref.dtype)

    project(wa_ref, 0, CONV_DIM, qkv_ref)
    project(wa_ref, CONV_DIM, DN_V_W, z_ref, _silu)
    off = 0
    for ref, width in zip((qsw_ref, kv_ref, gates_ref), IN_B_PIECES):
        project(wb_ref, off, width, ref)
        off += width
    ab_ref[...] = _dot3_nt(h, wab_ref[...])


def _inproj(x, mod, rows, pieces, nw, w_a, w_b, w_ab, name):
    sh, sc = pieces
    widths = IN_A_PIECES + IN_B_PIECES + (LANES,)
    return pl.pallas_call(
        _inproj_kernel,
        grid=rows.grid,
        in_specs=[rows.row_spec(D_MODEL), rows.mod_spec(sh), rows.mod_spec(sc),
                  _resident((1, D_MODEL)), _resident(w_a.shape), _resident(w_b.shape),
                  _resident(w_ab.shape)],
        out_specs=[rows.row_spec(w) for w in widths],
        out_shape=[jax.ShapeDtypeStruct((rows.n_rows, w), dt) for w, dt in zip(widths, IN_OUT_DTYPES)],
        compiler_params=_cparams("parallel"),
        name=name,
    )(x, mod, mod, nw, w_a, w_b, w_ab)


def _unit_lower_inverse(mats, row, col):
    n = mats[0].shape[0]
    eye = jnp.where(row == col, 1.0, 0.0).astype(F32)
    in_block = (row >> 4) == (col >> 4)
    p = [jnp.where(in_block, -a, 0.0) for a in mats]
    r = [eye + pi for pi in p]
    q = [_dot(pi, pi) for pi in p]
    for _ in range(2):
        rq = [_dot(qi, jnp.concatenate([ri, qi], axis=1)) for qi, ri in zip(q, r)]
        r = [ri + rqi[:, :n] for ri, rqi in zip(r, rq)]
        q = [rqi[:, n:] for rqi in rq]
    x = [ri + _dot(qi, ri) for qi, ri in zip(q, r)]
    s = 4
    while (1 << s) < n:
        pair = ((row >> (s + 1)) == (col >> (s + 1))) & ((row >> s) > (col >> s))
        t = [_dot(jnp.where(pair, a, 0.0), xi) for a, xi in zip(mats, x)]
        x = [xi - _dot(xi, ti) for xi, ti in zip(x, t)]
        s += 1
    return x


def _gdn_prompt_kernel(qkv_ref, z_ref, ab_ref, cw_ref, alog_ref, dtb_ref, dnw_ref,
                       y_ref, s_out_ref, tail_ref, halo_ref, s_ref):
    t = pl.program_id(1)
    c = GDN_CHUNK
    rows = qkv_ref.shape[0]
    halo = SUBLANES
    heads = range(DN_HEADS)

    @pl.when(t == 0)
    def _():
        s_ref[...] = jnp.zeros(s_ref.shape, F32)
        halo_ref[...] = jnp.zeros(halo_ref.shape, F32)

    row = lax.broadcasted_iota(jnp.int32, (c, c), 0)
    col = lax.broadcasted_iota(jnp.int32, (c, c), 1)
    lower = row >= col
    strict = row > col
    tri = jnp.where(lower, 1.0, 0.0).astype(BF16)

    def l2n(a, scale=1.0):
        return a * (lax.rsqrt(jnp.sum(a * a, axis=-1, keepdims=True) + EPS) * scale)

    for ci in range(rows // c):
        r0 = ci * c

        def conv_silu(c0):
            cols = slice(c0, c0 + LANES)
            cur = qkv_ref[r0:r0 + c, cols]
            y = cur * cw_ref[CONV_W - 1:CONV_W, cols]
            if ci == 0:
                ext = jnp.concatenate([halo_ref[:, cols], cur], axis=0)
            for j in range(CONV_W - 1):
                lag = CONV_W - 1 - j
                if ci == 0:
                    shifted = ext[halo - lag:halo - lag + c]
                else:
                    shifted = qkv_ref[pl.ds(r0 - lag, c), cols]
                y = y + shifted * cw_ref[j:j + 1, cols]
            return _silu(y)

        ab = ab_ref[r0:r0 + c, :]
        g_log = -jnp.exp(alog_ref[...]) * jax.nn.softplus(ab + dtb_ref[...])
        beta_all = jax.nn.sigmoid(ab)
        gc = sum(jnp.dot(tri, piece, preferred_element_type=F32) for piece in _split3(g_log))
        gc_t = gc.T
        gc_last = gc[c - 1:c, :]

        qn = [l2n(conv_silu(h * DN_DK), DN_DK ** -0.5) for h in heads]
        kn = [l2n(conv_silu(DN_QK_W + h * DN_DK)) for h in heads]
        v = [conv_silu(2 * DN_QK_W + h * DN_DV) for h in heads]
        g_col = [gc[:, h:h + 1] for h in heads]
        g_end = [gc_last[:, h:h + 1] for h in heads]
        beta = [beta_all[:, DN_HEADS + h:DN_HEADS + h + 1] for h in heads]
        decay = [jnp.where(lower, jnp.exp(g_col[h] - gc_t[h:h + 1, :]), 0.0) for h in heads]
        kb = [kn[h] * beta[h] for h in heads]
        e_col = [jnp.exp(g_col[h]) for h in heads]

        kq = [_dot_nt(jnp.concatenate([kb[h], qn[h]], axis=0), kn[h]) for h in heads]
        a_mat = [jnp.where(strict, kq[h][:c] * decay[h], 0.0) for h in heads]
        qk = [kq[h][c:] * decay[h] for h in heads]
        x_inv = _unit_lower_inverse(a_mat, row, col)
        uw = [_dot(x_inv[h], jnp.concatenate([v[h] * beta[h], kb[h] * e_col[h]], axis=1)) for h in heads]
        s_old = [s_ref[h] for h in heads]
        ws = [_dot(jnp.concatenate([uw[h][:, DN_DV:], qn[h] * e_col[h]], axis=0), s_old[h]) for h in heads]
        v_new = [uw[h][:, :DN_DV] - ws[h][:c] for h in heads]
        o = [ws[h][c:] + _dot(qk[h], v_new[h]) for h in heads]
        for h in heads:
            k_dec = kn[h] * jnp.exp(g_end[h] - g_col[h])
            s_ref[h] = s_old[h] * jnp.exp(g_end[h]) + _dot_tn(k_dec, v_new[h])
        for h in heads:
            oh = o[h] * lax.rsqrt(jnp.mean(o[h] * o[h], axis=-1, keepdims=True) + EPS) * dnw_ref[...]
            z_act = z_ref[r0:r0 + c, h * DN_DV:(h + 1) * DN_DV].astype(F32)
            y_ref[r0:r0 + c, h * DN_DV:(h + 1) * DN_DV] = (oh * z_act).astype(y_ref.dtype)

    halo_ref[...] = qkv_ref[rows - halo:rows, :]

    @pl.when(t == pl.num_programs(1) - 1)
    def _():
        s_out_ref[0] = s_ref[...]
        tail_ref[0] = qkv_ref[rows - halo:rows, :]


def _gdn_prompt(qkv, z, ab, conv_w, alog_pad, dtb_pad, dn_norm, n_batch, seq):
    rows = GDN_CHUNK * GDN_CHUNKS_PER_STEP
    assert seq % rows == 0
    nt = seq // rows
    row_spec = lambda w: pl.BlockSpec((rows, w), lambda b, t: (b * nt + t, 0))
    const = lambda shape: pl.BlockSpec(shape, lambda b, t: (0,) * len(shape))
    return pl.pallas_call(
        _gdn_prompt_kernel,
        grid=(n_batch, nt),
        in_specs=[row_spec(CONV_DIM), row_spec(DN_V_W), row_spec(LANES),
                  const((CONV_W, CONV_DIM)), const((1, LANES)), const((1, LANES)), const((1, DN_DV))],
        out_specs=[row_spec(DN_V_W),
                   pl.BlockSpec((1, DN_HEADS, DN_DK, DN_DV), lambda b, t: (b, 0, 0, 0)),
                   pl.BlockSpec((1, SUBLANES, CONV_DIM), lambda b, t: (b, 0, 0))],
        out_shape=[jax.ShapeDtypeStruct((n_batch * seq, DN_V_W), BF16),
                   jax.ShapeDtypeStruct((n_batch, DN_HEADS, DN_DK, DN_DV), F32),
                   jax.ShapeDtypeStruct((n_batch, SUBLANES, CONV_DIM), F32)],
        scratch_shapes=[pltpu.VMEM((SUBLANES, CONV_DIM), F32),
                        pltpu.VMEM((DN_HEADS, DN_DK, DN_DV), F32)],
        compiler_params=_cparams("parallel", "arbitrary"),
        name="gdn_prompt",
    )(qkv, z, ab, conv_w, alog_pad, dtb_pad, dn_norm)


def _kv_head_views(k2, v2, lo):
    k2r = pltpu.roll(k2, SWA_HD, axis=1)
    v2r = pltpu.roll(v2, SWA_HD, axis=1)
    k_lo = (jnp.where(lo, k2, 0.0), jnp.where(lo, k2r, 0.0))
    k_hi = (jnp.where(lo, 0.0, k2r), jnp.where(lo, 0.0, k2))
    v_dup = (jnp.where(lo, v2, v2r), jnp.where(lo, v2r, v2))
    return k_lo, k_hi, v_dup


def _swa_prompt_kernel(sink_ref, q_ref, kvc_ref, kvp_ref, o_ref):
    n = pl.program_id(1)
    blk = SWA_BLOCK
    n_sub = q_ref.shape[0] // blk
    lo = lax.broadcasted_iota(jnp.int32, (blk, LANES), 1) < SWA_HD
    own = (lax.broadcasted_iota(jnp.int32, (blk, blk), 0)
           >= lax.broadcasted_iota(jnp.int32, (blk, blk), 1))
    pairs = SWA_Q_HEADS // 2
    pairs_per_kv = pairs // SWA_KV_HEADS
    views = [_kv_head_views(kvp_ref[:, :LANES], kvp_ref[:, LANES:], lo)]
    for sub in range(n_sub):
        rs = slice(sub * blk, (sub + 1) * blk)
        views.append(_kv_head_views(kvc_ref[rs, :LANES], kvc_ref[rs, LANES:], lo))
    k_cat = {(sub, c): jnp.concatenate([views[1 + sub][0][c], views[1 + sub][1][c],
                                        views[sub][0][c], views[sub][1][c]], axis=0).astype(BF16)
             for sub in range(n_sub) for c in range(SWA_KV_HEADS)}
    v_cat = {(sub, c): jnp.concatenate([views[1 + sub][2][c], views[sub][2][c]], axis=0).astype(BF16)
             for sub in range(n_sub) for c in range(SWA_KV_HEADS)}
    groups = [(sub, c) for sub in range(n_sub) for c in range(SWA_KV_HEADS)]

    def score_stage(sub, c):
        return [_dot_nt(q_ref[sub * blk:(sub + 1) * blk, j * LANES:(j + 1) * LANES]
                        * (SWA_HD ** -0.5),
                        k_cat[sub, c])
                for j in range(c * pairs_per_kv, (c + 1) * pairs_per_kv)]

    def finish(sub, c, scores):
        heads = [(jj, half) for jj in range(pairs_per_kv) for half in range(2)]
        p, den = [], []
        for jj, half in heads:
            s_own = scores[jj][:, half * blk:(half + 1) * blk]
            s_prev = scores[jj][:, (2 + half) * blk:(3 + half) * blk]
            if sub == 0:
                s_prev = jnp.where(n > 0, s_prev, MASK_VALUE)
            s = jnp.where(own, s_own, s_prev)
            sink = sink_ref[2 * (c * pairs_per_kv + jj) + half]
            m = jnp.maximum(jnp.max(s, axis=-1, keepdims=True), sink)
            e = jnp.exp(s - m)
            p.append(jnp.concatenate([jnp.where(own, e, 0.0), jnp.where(own, 0.0, e)], axis=1))
            den.append(jnp.sum(e, axis=-1, keepdims=True) + jnp.exp(sink - m))
        out = [_dot(p[i], v_cat[sub, c]) / den[i] for i in range(len(heads))]
        for jj in range(pairs_per_kv):
            j = c * pairs_per_kv + jj
            o_ref[sub * blk:(sub + 1) * blk, j * LANES:(j + 1) * LANES] = (
                jnp.where(lo, out[2 * jj], out[2 * jj + 1]).astype(o_ref.dtype))

    pending = score_stage(*groups[0])
    for g, (sub, c) in enumerate(groups):
        nxt = score_stage(*groups[g + 1]) if g + 1 < len(groups) else None
        finish(sub, c, pending)
        pending = nxt


def _swa_prompt(q, kv, sinks, n_batch, seq):
    blk = SWA_BLOCK
    n_sub = SWA_BLOCKS_PER_STEP
    assert seq % (blk * n_sub) == 0
    ns = seq // (blk * n_sub)
    step_rows = lambda w: pl.BlockSpec((blk * n_sub, w), lambda b, n: (b * ns + n, 0))
    return pl.pallas_call(
        _swa_prompt_kernel,
        grid=(n_batch, ns),
        in_specs=[pl.BlockSpec(memory_space=pltpu.SMEM),
                  step_rows(SWA_Q_W), step_rows(2 * SWA_KV_W),
                  pl.BlockSpec((blk, 2 * SWA_KV_W),
                               lambda b, n: ((b * ns + n) * n_sub - jnp.minimum(n, 1), 0))],
        out_specs=step_rows(SWA_Q_W),
        out_shape=jax.ShapeDtypeStruct((n_batch * seq, SWA_Q_W), BF16),
        compiler_params=_cparams("parallel", "arbitrary"),
        name="swa_prompt",
    )(sinks, q, kv, kv)


def _mix(x, ydn, ysw, gates, g2, wbd_ref, wbs_ref, wout_ref):
    a = _dot(ydn, wbd_ref[...])
    b = _dot(ysw, wbs_ref[...])
    merged = (jax.nn.sigmoid(gates[:, :D_MODEL].astype(F32)) * a
              + jax.nn.sigmoid(gates[:, D_MODEL:].astype(F32)) * b)
    return x + g2 * _dot(merged, wout_ref[...])


def _final_norm(y, fw):
    return y * lax.rsqrt(jnp.mean(y * y, axis=-1, keepdims=True) + EPS) * fw


def _mix_ffn_kernel(x_ref, ydn_ref, ysw_ref, gates_ref, g2_ref, sh_ref, sc_ref, g3_ref, nw_ref,
                    wbd_ref, wbs_ref, wout_ref, wg_ref, wu_ref, w2_ref, fw_ref, o_ref, *, final):
    x = _mix(x_ref[...], ydn_ref[...], ysw_ref[...], gates_ref[...], g2_ref[0], wbd_ref, wbs_ref, wout_ref)
    y = _ffn_half_step(x, nw_ref[...], sc_ref[0], sh_ref[0], g3_ref[0], wg_ref, wu_ref, w2_ref)
    o_ref[...] = _final_norm(y, fw_ref[...]) if final else y


def _mix_ffn(x, y_dn, y_sw, gates, mod, rows, w, ffn_w, final_w, final, name):
    weights = (w['w_br_dn'], w['w_br_swa'], w['w_out']) + tuple(ffn_w)
    return pl.pallas_call(
        functools.partial(_mix_ffn_kernel, final=final),
        grid=rows.grid,
        in_specs=[rows.row_spec(D_MODEL), rows.row_spec(DN_V_W), rows.row_spec(SWA_Q_W),
                  rows.row_spec(2 * D_MODEL), rows.mod_spec(5), rows.mod_spec(6), rows.mod_spec(7),
                  rows.mod_spec(8), _resident((1, D_MODEL))]
                 + [_resident(a.shape) for a in weights] + [_resident((1, D_MODEL))],
        out_specs=rows.row_spec(D_MODEL),
        out_shape=jax.ShapeDtypeStruct((rows.n_rows, D_MODEL), F32),
        compiler_params=_cparams("parallel"),
        name=name,
    )(x, y_dn, y_sw, gates, mod, mod, mod, mod, w['norm_ffn2'], *weights, final_w)


def _mix_ffn_stream_kernel(x_ref, ydn_ref, ysw_ref, gates_ref, g2_ref, sh_ref, sc_ref, g3_ref, nw_ref,
                           wbd_ref, wbs_ref, wout_ref, fw_ref, wg_ref, wu_ref, w2_ref,
                           o_ref, wgb_ref, wub_ref, w2b_ref, x_scr, h_scr, acc_scr, *, final):
    j = pl.program_id(0)

    @pl.when(j == 0)
    def _():
        x = _mix(x_ref[...], ydn_ref[...], ysw_ref[...], gates_ref[...], g2_ref[0],
                 wbd_ref, wbs_ref, wout_ref)
        x_scr[...] = x
        h_scr[...] = _rms_mod(x, nw_ref[...], sc_ref[0], sh_ref[0]).astype(BF16)
        acc_scr[...] = jnp.zeros(acc_scr.shape, F32)

    _ffn_stream_step(h_scr, acc_scr, wg_ref, wu_ref, w2_ref, wgb_ref, wub_ref, w2b_ref)

    @pl.when(j == pl.num_programs(0) - 1)
    def _():
        y = x_scr[...] + (HALF_STEP * g3_ref[0]) * acc_scr[...]
        o_ref[...] = _final_norm(y, fw_ref[...]) if final else y


def _mix_ffn_stream(x, y_dn, y_sw, gates, mod, rows, w, w13, w2, final_w, final, name):
    assert rows.grid == (1,)
    const = lambda shape: pl.BlockSpec(shape, lambda j: (0,) * len(shape))
    mod_spec = lambda p: pl.BlockSpec(rows.mod_block, lambda j: (0, 0, p))
    mix_w = (w['w_br_dn'], w['w_br_swa'], w['w_out'])
    w_ins, w_outs, w_shapes = _ffn_stream_specs(w13, w2)
    tm = rows.tm
    return pl.pallas_call(
        functools.partial(_mix_ffn_stream_kernel, final=final),
        grid=(D_FF // FF_CHUNK,),
        in_specs=[const((tm, D_MODEL)), const((tm, DN_V_W)), const((tm, SWA_Q_W)), const((tm, 2 * D_MODEL)),
                  mod_spec(5), mod_spec(6), mod_spec(7), mod_spec(8), const((1, D_MODEL))]
                 + [const(a.shape) for a in mix_w] + [const((1, D_MODEL))] + w_ins,
        out_specs=[const((tm, D_MODEL))] + w_outs,
        out_shape=[jax.ShapeDtypeStruct((rows.n_rows, D_MODEL), F32)] + w_shapes,
        scratch_shapes=[pltpu.VMEM((tm, D_MODEL), F32), pltpu.VMEM((tm, D_MODEL), BF16),
                        pltpu.VMEM((tm, D_MODEL), F32)],
        compiler_params=_cparams("arbitrary"),
        name=name,
    )(x, y_dn, y_sw, gates, mod, mod, mod, mod, w['norm_ffn2'], *mix_w, final_w, w13, w13, w2)


def _gdn_step_prep_kernel(qkv_ref, cs_ref, ab_ref, cw_ref, alog_ref, dtb_ref,
                          cs_out_ref, qt_ref, kt_ref, v_ref, dec_ref, beta_ref):
    nb = qkv_ref.shape[0]
    for j in range(CONV_W - 2):
        cs_out_ref[j] = cs_ref[j + 1]
    cs_out_ref[CONV_W - 2] = qkv_ref[...]
    ab = ab_ref[...]
    dec = jnp.exp(-jnp.exp(alog_ref[...]) * jax.nn.softplus(ab + dtb_ref[...]))
    beta = jax.nn.sigmoid(ab)

    def conv_silu(c0):
        cols = slice(c0, c0 + LANES)
        y = qkv_ref[:, cols] * cw_ref[CONV_W - 1:CONV_W, cols]
        for j in range(CONV_W - 1):
            y = y + cs_ref[j, :, cols] * cw_ref[j:j + 1, cols]
        return _silu(y)

    for h in range(DN_HEADS):
        q = conv_silu(h * DN_DK)
        k = conv_silu(DN_QK_W + h * DN_DK)
        qn = q * lax.rsqrt(jnp.sum(q * q, axis=-1, keepdims=True) + EPS) * (DN_DK ** -0.5)
        kn = k * lax.rsqrt(jnp.sum(k * k, axis=-1, keepdims=True) + EPS)
        qt_ref[h] = qn.T
        kt_ref[h] = kn.T
        sl = slice(h * DN_DV, (h + 1) * DN_DV)
        v_ref[:, sl] = conv_silu(2 * DN_QK_W + h * DN_DV)
        dec_ref[:, sl] = jnp.broadcast_to(dec[:, h:h + 1], (nb, DN_DV))
        beta_ref[:, sl] = jnp.broadcast_to(beta[:, DN_HEADS + h:DN_HEADS + h + 1], (nb, DN_DV))


def _gdn_step_prep(qkv, conv_state, ab, conv_w, alog_pad, dtb_pad):
    nb = qkv.shape[0]
    full = _resident
    return pl.pallas_call(
        _gdn_step_prep_kernel,
        grid=(1,),
        in_specs=[full((nb, CONV_DIM)), full(conv_state.shape), full((nb, LANES)),
                  full((CONV_W, CONV_DIM)), full((1, LANES)), full((1, LANES))],
        out_specs=[full(conv_state.shape), full((DN_HEADS, DN_DK, nb)), full((DN_HEADS, DN_DK, nb)),
                   full((nb, DN_V_W)), full((nb, DN_V_W)), full((nb, DN_V_W))],
        out_shape=[jax.ShapeDtypeStruct(conv_state.shape, F32),
                   jax.ShapeDtypeStruct((DN_HEADS, DN_DK, nb), F32),
                   jax.ShapeDtypeStruct((DN_HEADS, DN_DK, nb), F32),
                   jax.ShapeDtypeStruct((nb, DN_V_W), F32),
                   jax.ShapeDtypeStruct((nb, DN_V_W), F32),
                   jax.ShapeDtypeStruct((nb, DN_V_W), F32)],
        compiler_params=_cparams("arbitrary"),
        name="gdn_step_prep",
    )(qkv, conv_state, ab, conv_w, alog_pad, dtb_pad)


def _gdn_step_kernel(s_ref, qt_ref, kt_ref, v_ref, dec_ref, beta_ref, z_ref, dnw_ref,
                     s_out_ref, y_ref, o_scr):
    nb = s_ref.shape[0]
    qt = qt_ref[0]
    kt = kt_ref[0]
    for b in range(nb):
        k_col = kt[:, b:b + 1]
        q_col = qt[:, b:b + 1]
        s1 = s_ref[b, 0] * dec_ref[b:b + 1, :]
        kv = jnp.sum(s1 * k_col, axis=0, keepdims=True)
        delta = (v_ref[b:b + 1, :] - kv) * beta_ref[b:b + 1, :]
        s2 = s1 + k_col * delta
        s_out_ref[b, 0] = s2
        o_scr[b:b + 1, :] = jnp.sum(s2 * q_col, axis=0, keepdims=True)
    o = o_scr[...]
    o = o * lax.rsqrt(jnp.mean(o * o, axis=-1, keepdims=True) + EPS) * dnw_ref[...]
    y_ref[...] = (o * z_ref[...].astype(F32)).astype(y_ref.dtype)


def _gdn_step(state, qt, kt, v, dec, beta, z, dn_norm):
    nb = state.shape[0]
    head_cols = pl.BlockSpec((nb, DN_DV), lambda h: (0, h))
    head_t = pl.BlockSpec((1, DN_DK, nb), lambda h: (h, 0, 0))
    s_spec = pl.BlockSpec((nb, 1, DN_DK, DN_DV), lambda h: (0, h, 0, 0))
    return pl.pallas_call(
        _gdn_step_kernel,
        grid=(DN_HEADS,),
        in_specs=[s_spec, head_t, head_t, head_cols, head_cols, head_cols, head_cols,
                  pl.BlockSpec((1, DN_DV), lambda h: (0, 0))],
        out_specs=[s_spec, head_cols],
        out_shape=[jax.ShapeDtypeStruct(state.shape, F32),
                   jax.ShapeDtypeStruct((nb, DN_V_W), BF16)],
        scratch_shapes=[pltpu.VMEM((nb, DN_DV), F32)],
        compiler_params=_cparams("parallel"),
        name="gdn_step",
    )(state, qt, kt, v, dec, beta, z, dn_norm)


SWA_STEP_BATCH = 16


def _swa_step_kernel(q_ref, kvn_ref, ck_ref, cv_ref, slo_ref, shi_ref, o_ref, ck_out_ref, cv_out_ref):
    length = ck_ref.shape[1]
    last = lax.broadcasted_iota(jnp.int32, (length, LANES), 0) == length - 1
    lo_k = lax.broadcasted_iota(jnp.int32, (length, LANES), 1) < SWA_HD
    pairs = SWA_Q_HEADS // 2
    first_kv = lax.broadcasted_iota(jnp.int32, (pairs, LANES), 0) < pairs // SWA_KV_HEADS
    lo_o = lax.broadcasted_iota(jnp.int32, (pairs, LANES), 1) < SWA_HD
    samples = range(q_ref.shape[0])
    k2 = [jnp.where(last, kvn_ref[b:b + 1, 0:LANES], pltpu.roll(ck_ref[b], length - 1, axis=0))
          for b in samples]
    v2 = [jnp.where(last, kvn_ref[b:b + 1, LANES:2 * LANES], pltpu.roll(cv_ref[b], length - 1, axis=0))
          for b in samples]
    for b in samples:
        ck_out_ref[b] = k2[b]
        cv_out_ref[b] = v2[b]
    views = [_kv_head_views(k2[b], v2[b], lo_k) for b in samples]
    k_cat = [jnp.concatenate([views[b][0][0], views[b][0][1], views[b][1][0], views[b][1][1]], axis=0)
             for b in samples]
    v_cat = [jnp.concatenate([views[b][2][0], views[b][2][1]], axis=1) for b in samples]
    scores = [_dot_nt(q_ref[b] * (SWA_HD ** -0.5), k_cat[b]) for b in samples]
    p, den = [], []
    for b in samples:
        halves_p, halves_den = [], []
        for half, sink_ref in enumerate((slo_ref, shi_ref)):
            s = jnp.where(first_kv, scores[b][:, 2 * half * length:(2 * half + 1) * length],
                          scores[b][:, (2 * half + 1) * length:(2 * half + 2) * length])
            sink = sink_ref[...]
            m = jnp.maximum(jnp.max(s, axis=-1, keepdims=True), sink)
            e = jnp.exp(s - m)
            halves_p.append(e)
            halves_den.append(jnp.sum(e, axis=-1, keepdims=True) + jnp.exp(sink - m))
        p.append(jnp.concatenate(halves_p, axis=0))
        den.append(halves_den)
    pv = [_dot(p[b], v_cat[b]) for b in samples]
    for b in samples:
        halves = [jnp.where(first_kv, pv[b][half * pairs:(half + 1) * pairs, :LANES],
                            pv[b][half * pairs:(half + 1) * pairs, LANES:]) / den[b][half]
                  for half in range(2)]
        o_ref[b] = jnp.where(lo_o, halves[0], halves[1]).astype(o_ref.dtype)


def _swa_step(q3, kv_new, cache_k, cache_v, sink_lo, sink_hi):
    nb, length, _ = cache_k.shape
    tb = SWA_STEP_BATCH
    pairs = SWA_Q_HEADS // 2
    q_spec = pl.BlockSpec((tb, pairs, LANES), lambda i: (i, 0, 0))
    c_spec = pl.BlockSpec((tb, length, LANES), lambda i: (i, 0, 0))
    sink_spec = pl.BlockSpec((pairs, LANES), lambda i: (0, 0))
    return pl.pallas_call(
        _swa_step_kernel,
        grid=(nb // tb,),
        in_specs=[q_spec, pl.BlockSpec((tb, 2 * SWA_KV_W), lambda i: (i, 0)), c_spec, c_spec,
                  sink_spec, sink_spec],
        out_specs=[q_spec, c_spec, c_spec],
        out_shape=[jax.ShapeDtypeStruct(q3.shape, BF16),
                   jax.ShapeDtypeStruct(cache_k.shape, F32),
                   jax.ShapeDtypeStruct(cache_v.shape, F32)],
        compiler_params=_cparams("parallel"),
        name="swa_step",
    )(q3, kv_new, cache_k, cache_v, sink_lo, sink_hi)


def _prep_layer_weights(lp):
    w_a, w_b, w_ab = _split_w_in(jnp.swapaxes(lp['w_in'], 0, 1))
    pad_heads = lambda a: jnp.pad(a.astype(F32), (0, LANES - DN_HEADS)).reshape(1, LANES)
    sinks = lp['sinks'].astype(F32)
    pairs = SWA_Q_HEADS // 2
    return dict(
        w13_ffn1=lp['w13_ffn1'], w2_ffn1=lp['w2_ffn1'], w13_ffn2=lp['w13_ffn2'], w2_ffn2=lp['w2_ffn2'],
        w_a=w_a, w_b=w_b, w_ab=w_ab,
        w_br_dn=lp['w_br_dn'].astype(BF16), w_br_swa=lp['w_br_swa'].astype(BF16),
        w_out=lp['w_out'].astype(BF16),
        norm_ffn1=lp['norm_ffn1'].reshape(1, D_MODEL), norm_mix=lp['norm_mix'].reshape(1, D_MODEL),
        norm_ffn2=lp['norm_ffn2'].reshape(1, D_MODEL),
        conv_w=lp['conv_w'], alog_pad=pad_heads(lp['a_log']), dtb_pad=pad_heads(lp['dt_bias']),
        dn_norm=lp['dn_norm'].reshape(1, DN_DV), sinks=sinks,
        sink_lo=jnp.broadcast_to(sinks.reshape(pairs, 2)[:, 0:1], (pairs, LANES)),
        sink_hi=jnp.broadcast_to(sinks.reshape(pairs, 2)[:, 1:2], (pairs, LANES)),
    )


def _layer(x, mod, rows, w, final_w, final, past, ffn_bf16, n_batch, seq, tag):
    if past is None:
        x = _ffn(x, mod, rows, (0, 1, 2), w['norm_ffn1'], ffn_bf16[0], "ffn1_" + tag)
    else:
        x, *ffn1_b = _ffn_stream(x, mod, rows, (0, 1, 2), w['norm_ffn1'], w['w13_ffn1'], w['w2_ffn1'],
                                 "ffn1_" + tag)
    if past is None:
        in_rows = _Rows(n_batch, seq, INPROJ_ROW_TILE)
        qkv, z, q_sw, kv, gates, ab = _inproj(
            x, mod, in_rows, (3, 4), w['norm_mix'], w['w_a'], w['w_b'], w['w_ab'], "inproj_" + tag)
        y_dn, s_new, tail = _gdn_prompt(qkv, z, ab, w['conv_w'], w['alog_pad'], w['dtb_pad'],
                                        w['dn_norm'], n_batch, seq)
        y_sw = _swa_prompt(q_sw, kv, w['sinks'], n_batch, seq)
        conv_new = tail[:, SUBLANES - (CONV_W - 1):]
        keep = min(WINDOW, seq)
        kv3 = kv.reshape(n_batch, seq, 2 * SWA_KV_W)[:, seq - keep:]
        k_buf = kv3[:, :, :SWA_KV_W].reshape(n_batch, keep, SWA_KV_HEADS, SWA_HD)
        v_buf = kv3[:, :, SWA_KV_W:].reshape(n_batch, keep, SWA_KV_HEADS, SWA_HD)
    else:
        s0, conv_buf, k_old, v_old = past
        length = k_old.shape[1]
        qkv, z, q_sw, kv, gates, ab = _inproj(
            x, mod, rows, (3, 4), w['norm_mix'], w['w_a'], w['w_b'], w['w_ab'], "inproj_" + tag)
        conv_new, qt, kt, v, dec, beta = _gdn_step_prep(qkv, jnp.swapaxes(conv_buf, 0, 1), ab,
                                                        w['conv_w'], w['alog_pad'], w['dtb_pad'])
        conv_new = jnp.swapaxes(conv_new, 0, 1)
        s_new, y_dn = _gdn_step(s0, qt, kt, v, dec, beta, z, w['dn_norm'])
        o3, k_buf, v_buf = _swa_step(q_sw.reshape(n_batch, SWA_Q_HEADS // 2, LANES), kv,
                                     k_old.reshape(n_batch, length, SWA_KV_W),
                                     v_old.reshape(n_batch, length, SWA_KV_W),
                                     w['sink_lo'], w['sink_hi'])
        y_sw = o3.reshape(n_batch, SWA_Q_W)
        k_buf = k_buf.reshape(n_batch, length, SWA_KV_HEADS, SWA_HD)
        v_buf = v_buf.reshape(n_batch, length, SWA_KV_HEADS, SWA_HD)
    if past is None:
        x = _mix_ffn(x, y_dn, y_sw, gates, mod, rows, w, ffn_bf16[1], final_w, final, "mix_ffn2_" + tag)
        return x, (s_new, conv_new, k_buf, v_buf)
    x, *ffn2_b = _mix_ffn_stream(x, y_dn, y_sw, gates, mod, rows, w, w['w13_ffn2'], w['w2_ffn2'],
                                 final_w, final, "mix_ffn2_" + tag)
    return x, (s_new, conv_new, k_buf, v_buf), (ffn1_b, ffn2_b)


def kernel(x_prompt, x_sample, state_dn, state_conv, cache_swa_k, cache_swa_v, c_prompt, c_sample,
           w_ada, b_ada, norm_ffn1, w13_ffn1, w2_ffn1, norm_mix, w_in, conv_w, a_log, dt_bias,
           dn_norm, sinks, w_br_dn, w_br_swa, w_out, norm_ffn2, w13_ffn2, w2_ffn2, final_norm):
    n_p, seq_p, d = x_prompt.shape
    n_s, seq_s, _ = x_sample.shape
    depth = w_ada.shape[0]
    assert d == D_MODEL and seq_s == 1 and seq_p % ROW_TILE == 0 and seq_p % GDN_CHUNK == 0
    assert w_in.shape[2] == sum(IN_SPLITS) and w13_ffn1.shape[2] == 2 * D_FF
    assert cache_swa_k.shape[2] == WINDOW and n_s % SWA_STEP_BATCH == 0 and n_p % SUBLANES == 0
    rows_p = _Rows(n_p, seq_p, ROW_TILE)
    rows_s = _Rows(n_s, seq_s, ROW_TILE)
    c_all = jnp.concatenate([c_prompt, c_sample], axis=0)
    final_w = final_norm.reshape(1, D_MODEL)
    y_p = x_prompt.reshape(n_p * seq_p, d)
    y_s = x_sample.reshape(n_s * seq_s, d)
    st_p, st_s = [], []
    for l in range(depth):
        lp = dict(w_ada=w_ada[l], b_ada=b_ada[l], norm_ffn1=norm_ffn1[l], w13_ffn1=w13_ffn1[l],
                  w2_ffn1=w2_ffn1[l], norm_mix=norm_mix[l], w_in=w_in[l], conv_w=conv_w[l],
                  a_log=a_log[l], dt_bias=dt_bias[l], dn_norm=dn_norm[l], sinks=sinks[l],
                  w_br_dn=w_br_dn[l], w_br_swa=w_br_swa[l], w_out=w_out[l], norm_ffn2=norm_ffn2[l],
                  w13_ffn2=w13_ffn2[l], w2_ffn2=w2_ffn2[l])
        w = _prep_layer_weights(lp)
        mod_p, mod_s = _ada(c_all, lp['w_ada'], lp['b_ada'], n_p)
        final = l == depth - 1
        y_s, ss, ffn_bf16 = _layer(y_s, mod_s.reshape(rows_s.mod_shape), rows_s, w, final_w, final,
                                   (state_dn[l], state_conv[l], cache_swa_k[l], cache_swa_v[l]), None,
                                   n_s, seq_s, "sample")
        y_p, sp = _layer(y_p, mod_p.reshape(rows_p.mod_shape), rows_p, w, final_w, final, None, ffn_bf16,
                         n_p, seq_p, "prompt")
        st_p.append(sp)
        st_s.append(ss)
    stack = lambda sts, i: sts[0][i][None] if depth == 1 else jnp.stack([s[i] for s in sts])
    return (y_p.reshape(n_p, seq_p, d), y_s.reshape(n_s, seq_s, d),
            stack(st_p, 0), stack(st_s, 0), stack(st_p, 1), stack(st_s, 1),
            stack(st_p, 2), stack(st_s, 2), stack(st_p, 3), stack(st_s, 3))
```

```python
import functools

import jax
import jax.numpy as jnp
from jax import lax
from jax.experimental import pallas as pl
from jax.experimental.pallas import tpu as pltpu

F32 = jnp.float32
BF16 = jnp.bfloat16

D_MODEL = 1024
DN_HEADS = 8
DN_DK = 128
DN_DV = 128
DN_QK_W = DN_HEADS * DN_DK
DN_V_W = DN_HEADS * DN_DV
CONV_W = 4
CONV_DIM = 2 * DN_QK_W + DN_V_W
SWA_Q_HEADS = 16
SWA_KV_HEADS = 2
SWA_HD = 64
SWA_Q_W = SWA_Q_HEADS * SWA_HD
SWA_KV_W = SWA_KV_HEADS * SWA_HD
WINDOW = 128
D_FF = 2816
HALF_STEP = 0.5
N_MOD = 9
EPS = 1e-6
MASK_VALUE = -1e30
IN_SPLITS = (CONV_DIM, DN_V_W, DN_HEADS, DN_HEADS, SWA_Q_W, SWA_KV_W, SWA_KV_W, D_MODEL, D_MODEL)

LANES = 128
SUBLANES = 8
VMEM_LIMIT_BYTES = 56 * 1024 * 1024

GDN_CHUNK = 128
GDN_CHUNKS_PER_STEP = 4
SWA_BLOCK = 128
SWA_BLOCKS_PER_STEP = 2
FF_CHUNK = 256
ROW_TILE = 512
INPROJ_ROW_TILE = 512
IN_COL_CHUNK = 512
ADA_COL_TILE = 3072


def _cparams(*sem):
    return pltpu.CompilerParams(dimension_semantics=sem, vmem_limit_bytes=VMEM_LIMIT_BYTES)


def _resident(shape):
    nd = len(shape)
    return pl.BlockSpec(shape, lambda *_: (0,) * nd, pipeline_mode=pl.Buffered(1))


def _dot(a, b):
    return jnp.dot(a.astype(BF16), b.astype(BF16), preferred_element_type=F32)


def _dot_nt(a, b):
    return lax.dot_general(a.astype(BF16), b.astype(BF16), (((1,), (1,)), ((), ())),
                           preferred_element_type=F32)


def _dot_tn(a, b):
    return lax.dot_general(a.astype(BF16), b.astype(BF16), (((0,), (0,)), ((), ())),
                           preferred_element_type=F32)


def _split3(a):
    hi = a.astype(BF16)
    r = a - hi.astype(F32)
    mid = r.astype(BF16)
    lo = (r - mid.astype(F32)).astype(BF16)
    return hi, mid, lo


def _dot3(a, b):
    a_hi = a.astype(BF16)
    a_lo = (a - a_hi.astype(F32)).astype(BF16)
    b_hi = b.astype(BF16)
    b_lo = (b - b_hi.astype(F32)).astype(BF16)
    d = functools.partial(jnp.dot, preferred_element_type=F32)
    return d(a_hi, b_hi) + (d(a_lo, b_hi) + d(a_hi, b_lo))


def _silu(x):
    return x * jax.nn.sigmoid(x)


def _rms_mod(x, nw, sc, sh):
    y = x * lax.rsqrt(jnp.mean(x * x, axis=-1, keepdims=True) + EPS)
    return (y * nw) * (1.0 + sc) + sh


def _ada_kernel(c_ref, w_ref, b_ref, op_ref, os_ref, *, n_prompt):
    m = _dot(_silu(c_ref[...]), w_ref[...]) + b_ref[...]
    op_ref[...] = m[:n_prompt]
    os_ref[...] = m[n_prompt:]


def _ada(c_all, w_ada, b_ada, n_prompt):
    n_all, d = c_all.shape
    n_out = w_ada.shape[1]
    tn = ADA_COL_TILE
    assert n_out % tn == 0
    return pl.pallas_call(
        functools.partial(_ada_kernel, n_prompt=n_prompt),
        grid=(n_out // tn,),
        in_specs=[pl.BlockSpec((n_all, d), lambda j: (0, 0)),
                  pl.BlockSpec((d, tn), lambda j: (0, j)),
                  pl.BlockSpec((1, tn), lambda j: (0, j))],
        out_specs=[pl.BlockSpec((n_prompt, tn), lambda j: (0, j)),
                   pl.BlockSpec((n_all - n_prompt, tn), lambda j: (0, j))],
        out_shape=[jax.ShapeDtypeStruct((n_prompt, n_out), F32),
                   jax.ShapeDtypeStruct((n_all - n_prompt, n_out), F32)],
        compiler_params=_cparams("arbitrary"),
        name="ada",
    )(c_all, w_ada, b_ada.reshape(1, n_out))


class _Rows:
    def __init__(self, n_batch, seq, row_tile):
        self.n_batch = n_batch
        self.n_rows = n_batch * seq
        if seq == 1:
            self.tm = n_batch
            self.mod_shape = (1, n_batch, N_MOD * D_MODEL)
            self.mod_block = (1, n_batch, D_MODEL)
            self.tiles_per_batch = None
        else:
            self.tm = min(row_tile, seq)
            assert seq % self.tm == 0
            self.mod_shape = (n_batch, 1, N_MOD * D_MODEL)
            self.mod_block = (1, 1, D_MODEL)
            self.tiles_per_batch = seq // self.tm
        self.grid = (self.n_rows // self.tm,)

    def mod_spec(self, piece):
        if self.tiles_per_batch is None:
            return pl.BlockSpec(self.mod_block, lambda i: (0, 0, piece))
        tpb = self.tiles_per_batch
        return pl.BlockSpec(self.mod_block, lambda i: (i // tpb, 0, piece))

    def row_spec(self, width):
        return pl.BlockSpec((self.tm, width), lambda i: (i, 0))


def _ffn_half_step(x, nw, sc, sh, g, wg_ref, wu_ref, w2_ref):
    h = _rms_mod(x, nw, sc, sh).astype(BF16)
    acc = jnp.zeros(x.shape, F32)
    for c0 in range(0, D_FF, FF_CHUNK):
        gate = jnp.dot(h, wg_ref[:, c0:c0 + FF_CHUNK], preferred_element_type=F32)
        up = jnp.dot(h, wu_ref[:, c0:c0 + FF_CHUNK], preferred_element_type=F32)
        act = (_silu(gate) * up).astype(BF16)
        acc = acc + jnp.dot(act, w2_ref[c0:c0 + FF_CHUNK, :], preferred_element_type=F32)
    return x + (HALF_STEP * g) * acc


def _ffn_kernel(x_ref, sh_ref, sc_ref, g_ref, nw_ref, wg_ref, wu_ref, w2_ref, o_ref):
    o_ref[...] = _ffn_half_step(x_ref[...], nw_ref[...], sc_ref[0], sh_ref[0], g_ref[0],
                                wg_ref, wu_ref, w2_ref)


def _ffn(x, mod, rows, pieces, nw, ffn_w, name):
    sh, sc, g = pieces
    return pl.pallas_call(
        _ffn_kernel,
        grid=rows.grid,
        in_specs=[rows.row_spec(D_MODEL), rows.mod_spec(sh), rows.mod_spec(sc), rows.mod_spec(g),
                  _resident((1, D_MODEL))] + [_resident(a.shape) for a in ffn_w],
        out_specs=rows.row_spec(D_MODEL),
        out_shape=jax.ShapeDtypeStruct((rows.n_rows, D_MODEL), F32),
        compiler_params=_cparams("parallel"),
        name=name,
    )(x, mod, mod, mod, nw, *ffn_w)


def _ffn_stream_step(h_scr, acc_scr, wg_ref, wu_ref, w2_ref, wgb_ref, wub_ref, w2b_ref):
    wg = wg_ref[...].astype(BF16)
    wu = wu_ref[...].astype(BF16)
    w2 = w2_ref[...].astype(BF16)
    wgb_ref[...] = wg
    wub_ref[...] = wu
    w2b_ref[...] = w2
    h = h_scr[...]
    gate = jnp.dot(h, wg, preferred_element_type=F32)
    up = jnp.dot(h, wu, preferred_element_type=F32)
    act = (_silu(gate) * up).astype(BF16)
    acc_scr[...] += jnp.dot(act, w2, preferred_element_type=F32)


def _ffn_stream_specs(w13, w2):
    d = w13.shape[0]
    n_chunks = D_FF // FF_CHUNK
    ins = [pl.BlockSpec((d, FF_CHUNK), lambda j: (0, j)),
           pl.BlockSpec((d, FF_CHUNK), lambda j: (0, n_chunks + j)),
           pl.BlockSpec((FF_CHUNK, d), lambda j: (j, 0))]
    outs = [pl.BlockSpec((d, FF_CHUNK), lambda j: (0, j)),
            pl.BlockSpec((d, FF_CHUNK), lambda j: (0, j)),
            pl.BlockSpec((FF_CHUNK, d), lambda j: (j, 0))]
    shapes = [jax.ShapeDtypeStruct((d, D_FF), BF16), jax.ShapeDtypeStruct((d, D_FF), BF16),
              jax.ShapeDtypeStruct((D_FF, d), BF16)]
    return ins, outs, shapes


def _ffn_stream_kernel(x_ref, sh_ref, sc_ref, g_ref, nw_ref, wg_ref, wu_ref, w2_ref,
                       o_ref, wgb_ref, wub_ref, w2b_ref, h_scr, acc_scr):
    j = pl.program_id(0)

    @pl.when(j == 0)
    def _():
        h_scr[...] = _rms_mod(x_ref[...], nw_ref[...], sc_ref[0], sh_ref[0]).astype(BF16)
        acc_scr[...] = jnp.zeros(acc_scr.shape, F32)

    _ffn_stream_step(h_scr, acc_scr, wg_ref, wu_ref, w2_ref, wgb_ref, wub_ref, w2b_ref)

    @pl.when(j == pl.num_programs(0) - 1)
    def _():
        o_ref[...] = x_ref[...] + (HALF_STEP * g_ref[0]) * acc_scr[...]


def _ffn_stream(x, mod, rows, pieces, nw, w13, w2, name):
    sh, sc, g = pieces
    assert rows.grid == (1,)
    const = lambda shape: pl.BlockSpec(shape, lambda j: (0,) * len(shape))
    w_ins, w_outs, w_shapes = _ffn_stream_specs(w13, w2)
    mod_spec = lambda p: pl.BlockSpec(rows.mod_block, lambda j: (0, 0, p))
    return pl.pallas_call(
        _ffn_stream_kernel,
        grid=(D_FF // FF_CHUNK,),
        in_specs=[const((rows.tm, D_MODEL)), mod_spec(sh), mod_spec(sc), mod_spec(g),
                  const((1, D_MODEL))] + w_ins,
        out_specs=[const((rows.tm, D_MODEL))] + w_outs,
        out_shape=[jax.ShapeDtypeStruct((rows.n_rows, D_MODEL), F32)] + w_shapes,
        scratch_shapes=[pltpu.VMEM((rows.tm, D_MODEL), BF16), pltpu.VMEM((rows.tm, D_MODEL), F32)],
        compiler_params=_cparams("arbitrary"),
        name=name,
    )(x, mod, mod, mod, nw, w13, w13, w2)


IN_A_PIECES = (CONV_DIM, DN_V_W)
IN_B_PIECES = (SWA_Q_W, 2 * SWA_KV_W, 2 * D_MODEL)
IN_OUT_DTYPES = (F32, BF16, BF16, F32, BF16, F32)


N_AB = 2 * DN_HEADS


W_IN_CAST_ROWS = 256


def _cast_w_in_kernel(w_src, wab_src, w_ref, wab_ref):
    w_ref[...] = w_src[...].astype(BF16)
    row = lax.broadcasted_iota(jnp.int32, wab_ref.shape, 0)
    wab_ref[...] = jnp.where(row < N_AB, wab_src[...], 0.0)


def _cast_w_in(w_in_t):
    n, d = w_in_t.shape
    n_a = sum(IN_A_PIECES)
    assert n == n_a + N_AB + sum(IN_B_PIECES) and n_a % LANES == 0
    tr = W_IN_CAST_ROWS
    return pl.pallas_call(
        _cast_w_in_kernel,
        grid=(pl.cdiv(n, tr),),
        in_specs=[pl.BlockSpec((tr, d), lambda i: (i, 0)),
                  pl.BlockSpec((LANES, d), lambda i: (n_a // LANES, 0))],
        out_specs=[pl.BlockSpec((tr, d), lambda i: (i, 0)),
                   pl.BlockSpec((LANES, d), lambda i: (0, 0))],
        out_shape=[jax.ShapeDtypeStruct((n, d), BF16), jax.ShapeDtypeStruct((LANES, d), F32)],
        compiler_params=_cparams("arbitrary"),
        name="cast_w_in",
    )(w_in_t, w_in_t)


def _dot3_nt(a, b):
    a_hi = a.astype(BF16)
    a_lo = (a - a_hi.astype(F32)).astype(BF16)
    b_hi = b.astype(BF16)
    b_lo = (b - b_hi.astype(F32)).astype(BF16)
    return _dot_nt(a_hi, b_hi) + (_dot_nt(a_lo, b_hi) + _dot_nt(a_hi, b_lo))


def _inproj_kernel(x_ref, sh_ref, sc_ref, nw_ref, w_ref, wab_ref,
                   qkv_ref, z_ref, qsw_ref, kv_ref, gates_ref, ab_ref):
    h = _rms_mod(x_ref[...], nw_ref[...], sc_ref[0], sh_ref[0])
    hb = h.astype(BF16)

    def project(off, width, ref, act=None):
        for c0 in range(0, width, IN_COL_CHUNK):
            cw = min(IN_COL_CHUNK, width - c0)
            val = _dot_nt(hb, w_ref[off + c0:off + c0 + cw, :])
            ref[:, c0:c0 + cw] = (val if act is None else act(val)).astype(ref.dtype)

    project(0, CONV_DIM, qkv_ref)
    project(CONV_DIM, DN_V_W, z_ref, _silu)
    off = sum(IN_A_PIECES) + N_AB
    for ref, width in zip((qsw_ref, kv_ref, gates_ref), IN_B_PIECES):
        project(off, width, ref)
        off += width
    ab_ref[...] = _dot3_nt(h, wab_ref[...])


def _inproj(x, mod, rows, pieces, nw, w_t, w_ab, name):
    sh, sc = pieces
    widths = IN_A_PIECES + IN_B_PIECES + (LANES,)
    return pl.pallas_call(
        _inproj_kernel,
        grid=rows.grid,
        in_specs=[rows.row_spec(D_MODEL), rows.mod_spec(sh), rows.mod_spec(sc),
                  _resident((1, D_MODEL)), _resident(w_t.shape), _resident(w_ab.shape)],
        out_specs=[rows.row_spec(w) for w in widths],
        out_shape=[jax.ShapeDtypeStruct((rows.n_rows, w), dt) for w, dt in zip(widths, IN_OUT_DTYPES)],
        compiler_params=_cparams("parallel"),
        name=name,
    )(x, mod, mod, nw, w_t, w_ab)


def _unit_lower_inverse(mats, row, col):
    n = mats[0].shape[0]
    eye = jnp.where(row == col, 1.0, 0.0).astype(F32)
    in_block = (row >> 4) == (col >> 4)
    p = [jnp.where(in_block, -a, 0.0) for a in mats]
    r = [eye + pi for pi in p]
    q = [_dot(pi, pi) for pi in p]
    for _ in range(2):
        rq = [_dot(qi, jnp.concatenate([ri, qi], axis=1)) for qi, ri in zip(q, r)]
        r = [ri + rqi[:, :n] for ri, rqi in zip(r, rq)]
        q = [rqi[:, n:] for rqi in rq]
    x = [ri + _dot(qi, ri) for qi, ri in zip(q, r)]
    s = 4
    while (1 << s) < n:
        pair = ((row >> (s + 1)) == (col >> (s + 1))) & ((row >> s) > (col >> s))
        t = [_dot(jnp.where(pair, a, 0.0), xi) for a, xi in zip(mats, x)]
        x = [xi - _dot(xi, ti) for xi, ti in zip(x, t)]
        s += 1
    return x


def _gdn_prompt_kernel(qkv_ref, z_ref, ab_ref, cw_ref, alog_ref, dtb_ref, dnw_ref,
                       y_ref, s_out_ref, tail_ref, halo_ref, s_ref):
    t = pl.program_id(1)
    c = GDN_CHUNK
    rows = qkv_ref.shape[0]
    halo = SUBLANES
    heads = range(DN_HEADS)

    @pl.when(t == 0)
    def _():
        s_ref[...] = jnp.zeros(s_ref.shape, F32)
        halo_ref[...] = jnp.zeros(halo_ref.shape, F32)

    row = lax.broadcasted_iota(jnp.int32, (c, c), 0)
    col = lax.broadcasted_iota(jnp.int32, (c, c), 1)
    lower = row >= col
    strict = row > col
    tri = jnp.where(lower, 1.0, 0.0).astype(BF16)

    def l2n(a, scale=1.0):
        return a * (lax.rsqrt(jnp.sum(a * a, axis=-1, keepdims=True) + EPS) * scale)

    for ci in range(rows // c):
        r0 = ci * c

        def conv_silu(c0):
            cols = slice(c0, c0 + LANES)
            cur = qkv_ref[r0:r0 + c, cols]
            y = cur * cw_ref[CONV_W - 1:CONV_W, cols]
            if ci == 0:
                ext = jnp.concatenate([halo_ref[:, cols], cur], axis=0)
            for j in range(CONV_W - 1):
                lag = CONV_W - 1 - j
                if ci == 0:
                    shifted = ext[halo - lag:halo - lag + c]
                else:
                    shifted = qkv_ref[pl.ds(r0 - lag, c), cols]
                y = y + shifted * cw_ref[j:j + 1, cols]
            return _silu(y)

        ab = ab_ref[r0:r0 + c, :]
        g_log = -jnp.exp(alog_ref[...]) * jax.nn.softplus(ab + dtb_ref[...])
        beta_all = jax.nn.sigmoid(ab)
        gc = sum(jnp.dot(tri, piece, preferred_element_type=F32) for piece in _split3(g_log))
        gc_t = gc.T
        gc_last = gc[c - 1:c, :]

        qn = [l2n(conv_silu(h * DN_DK), DN_DK ** -0.5) for h in heads]
        kn = [l2n(conv_silu(DN_QK_W + h * DN_DK)) for h in heads]
        v = [conv_silu(2 * DN_QK_W + h * DN_DV) for h in heads]
        g_col = [gc[:, h:h + 1] for h in heads]
        g_end = [gc_last[:, h:h + 1] for h in heads]
        beta = [beta_all[:, DN_HEADS + h:DN_HEADS + h + 1] for h in heads]
        decay = [jnp.where(lower, jnp.exp(g_col[h] - gc_t[h:h + 1, :]), 0.0) for h in heads]
        kb = [kn[h] * beta[h] for h in heads]
        e_col = [jnp.exp(g_col[h]) for h in heads]

        kq = [_dot_nt(jnp.concatenate([kb[h], qn[h]], axis=0), kn[h]) for h in heads]
        a_mat = [jnp.where(strict, kq[h][:c] * decay[h], 0.0) for h in heads]
        qk = [kq[h][c:] * decay[h] for h in heads]
        x_inv = _unit_lower_inverse(a_mat, row, col)
        uw = [_dot(x_inv[h], jnp.concatenate([v[h] * beta[h], kb[h] * e_col[h]], axis=1)) for h in heads]
        s_old = [s_ref[h] for h in heads]
        ws = [_dot(jnp.concatenate([uw[h][:, DN_DV:], qn[h] * e_col[h]], axis=0), s_old[h]) for h in heads]
        v_new = [uw[h][:, :DN_DV] - ws[h][:c] for h in heads]
        o = [ws[h][c:] + _dot(qk[h], v_new[h]) for h in heads]
        for h in heads:
            k_dec = kn[h] * jnp.exp(g_end[h] - g_col[h])
            s_ref[h] = s_old[h] * jnp.exp(g_end[h]) + _dot_tn(k_dec, v_new[h])
        for h in heads:
            oh = o[h] * lax.rsqrt(jnp.mean(o[h] * o[h], axis=-1, keepdims=True) + EPS) * dnw_ref[...]
            z_act = z_ref[r0:r0 + c, h * DN_DV:(h + 1) * DN_DV].astype(F32)
            y_ref[r0:r0 + c, h * DN_DV:(h + 1) * DN_DV] = (oh * z_act).astype(y_ref.dtype)

    halo_ref[...] = qkv_ref[rows - halo:rows, :]

    @pl.when(t == pl.num_programs(1) - 1)
    def _():
        s_out_ref[0] = s_ref[...]
        tail_ref[0] = qkv_ref[rows - halo:rows, :]


def _gdn_prompt(qkv, z, ab, conv_w, alog_pad, dtb_pad, dn_norm, n_batch, seq):
    rows = GDN_CHUNK * GDN_CHUNKS_PER_STEP
    assert seq % rows == 0
    nt = seq // rows
    row_spec = lambda w: pl.BlockSpec((rows, w), lambda b, t: (b * nt + t, 0))
    const = lambda shape: pl.BlockSpec(shape, lambda b, t: (0,) * len(shape))
    return pl.pallas_call(
        _gdn_prompt_kernel,
        grid=(n_batch, nt),
        in_specs=[row_spec(CONV_DIM), row_spec(DN_V_W), row_spec(LANES),
                  const((CONV_W, CONV_DIM)), const((1, LANES)), const((1, LANES)), const((1, DN_DV))],
        out_specs=[row_spec(DN_V_W),
                   pl.BlockSpec((1, DN_HEADS, DN_DK, DN_DV), lambda b, t: (b, 0, 0, 0)),
                   pl.BlockSpec((1, SUBLANES, CONV_DIM), lambda b, t: (b, 0, 0))],
        out_shape=[jax.ShapeDtypeStruct((n_batch * seq, DN_V_W), BF16),
                   jax.ShapeDtypeStruct((n_batch, DN_HEADS, DN_DK, DN_DV), F32),
                   jax.ShapeDtypeStruct((n_batch, SUBLANES, CONV_DIM), F32)],
        scratch_shapes=[pltpu.VMEM((SUBLANES, CONV_DIM), F32),
                        pltpu.VMEM((DN_HEADS, DN_DK, DN_DV), F32)],
        compiler_params=_cparams("parallel", "arbitrary"),
        name="gdn_prompt",
    )(qkv, z, ab, conv_w, alog_pad, dtb_pad, dn_norm)


def _kv_head_views(k2, v2, lo):
    k2r = pltpu.roll(k2, SWA_HD, axis=1)
    v2r = pltpu.roll(v2, SWA_HD, axis=1)
    k_lo = (jnp.where(lo, k2, 0.0), jnp.where(lo, k2r, 0.0))
    k_hi = (jnp.where(lo, 0.0, k2r), jnp.where(lo, 0.0, k2))
    v_dup = (jnp.where(lo, v2, v2r), jnp.where(lo, v2r, v2))
    return k_lo, k_hi, v_dup


def _swa_prompt_kernel(sink_ref, q_ref, kvc_ref, kvp_ref, o_ref):
    n = pl.program_id(1)
    blk = SWA_BLOCK
    n_sub = q_ref.shape[0] // blk
    lo = lax.broadcasted_iota(jnp.int32, (blk, LANES), 1) < SWA_HD
    own = (lax.broadcasted_iota(jnp.int32, (blk, blk), 0)
           >= lax.broadcasted_iota(jnp.int32, (blk, blk), 1))
    pairs = SWA_Q_HEADS // 2
    pairs_per_kv = pairs // SWA_KV_HEADS
    views = [_kv_head_views(kvp_ref[:, :LANES], kvp_ref[:, LANES:], lo)]
    for sub in range(n_sub):
        rs = slice(sub * blk, (sub + 1) * blk)
        views.append(_kv_head_views(kvc_ref[rs, :LANES], kvc_ref[rs, LANES:], lo))
    k_cat = {(sub, c): jnp.concatenate([views[1 + sub][0][c], views[1 + sub][1][c],
                                        views[sub][0][c], views[sub][1][c]], axis=0).astype(BF16)
             for sub in range(n_sub) for c in range(SWA_KV_HEADS)}
    v_cat = {(sub, c): jnp.concatenate([views[1 + sub][2][c], views[sub][2][c]], axis=0).astype(BF16)
             for sub in range(n_sub) for c in range(SWA_KV_HEADS)}
    groups = [(sub, c) for sub in range(n_sub) for c in range(SWA_KV_HEADS)]

    def score_stage(sub, c):
        return [_dot_nt(q_ref[sub * blk:(sub + 1) * blk, j * LANES:(j + 1) * LANES]
                        * (SWA_HD ** -0.5),
                        k_cat[sub, c])
                for j in range(c * pairs_per_kv, (c + 1) * pairs_per_kv)]

    def finish(sub, c, scores):
        heads = [(jj, half) for jj in range(pairs_per_kv) for half in range(2)]
        p, den = [], []
        for jj, half in heads:
            s_own = scores[jj][:, half * blk:(half + 1) * blk]
            s_prev = scores[jj][:, (2 + half) * blk:(3 + half) * blk]
            if sub == 0:
                s_prev = jnp.where(n > 0, s_prev, MASK_VALUE)
            s = jnp.where(own, s_own, s_prev)
            sink = sink_ref[2 * (c * pairs_per_kv + jj) + half]
            m = jnp.maximum(jnp.max(s, axis=-1, keepdims=True), sink)
            e = jnp.exp(s - m)
            p.append(jnp.concatenate([jnp.where(own, e, 0.0), jnp.where(own, 0.0, e)], axis=1))
            den.append(jnp.sum(e, axis=-1, keepdims=True) + jnp.exp(sink - m))
        out = [_dot(p[i], v_cat[sub, c]) / den[i] for i in range(len(heads))]
        for jj in range(pairs_per_kv):
            j = c * pairs_per_kv + jj
            o_ref[sub * blk:(sub + 1) * blk, j * LANES:(j + 1) * LANES] = (
                jnp.where(lo, out[2 * jj], out[2 * jj + 1]).astype(o_ref.dtype))

    pending = score_stage(*groups[0])
    for g, (sub, c) in enumerate(groups):
        nxt = score_stage(*groups[g + 1]) if g + 1 < len(groups) else None
        finish(sub, c, pending)
        pending = nxt


def _swa_prompt(q, kv, sinks, n_batch, seq):
    blk = SWA_BLOCK
    n_sub = SWA_BLOCKS_PER_STEP
    assert seq % (blk * n_sub) == 0
    ns = seq // (blk * n_sub)
    step_rows = lambda w: pl.BlockSpec((blk * n_sub, w), lambda b, n: (b * ns + n, 0))
    return pl.pallas_call(
        _swa_prompt_kernel,
        grid=(n_batch, ns),
        in_specs=[pl.BlockSpec(memory_space=pltpu.SMEM),
                  step_rows(SWA_Q_W), step_rows(2 * SWA_KV_W),
                  pl.BlockSpec((blk, 2 * SWA_KV_W),
                               lambda b, n: ((b * ns + n) * n_sub - jnp.minimum(n, 1), 0))],
        out_specs=step_rows(SWA_Q_W),
        out_shape=jax.ShapeDtypeStruct((n_batch * seq, SWA_Q_W), BF16),
        compiler_params=_cparams("parallel", "arbitrary"),
        name="swa_prompt",
    )(sinks, q, kv, kv)


def _mix(x, ydn, ysw, gates, g2, wbd_ref, wbs_ref, wout_ref):
    a = _dot(ydn, wbd_ref[...])
    b = _dot(ysw, wbs_ref[...])
    merged = (jax.nn.sigmoid(gates[:, :D_MODEL].astype(F32)) * a
              + jax.nn.sigmoid(gates[:, D_MODEL:].astype(F32)) * b)
    return x + g2 * _dot(merged, wout_ref[...])


def _final_norm(y, fw):
    return y * lax.rsqrt(jnp.mean(y * y, axis=-1, keepdims=True) + EPS) * fw


def _mix_ffn_kernel(x_ref, ydn_ref, ysw_ref, gates_ref, g2_ref, sh_ref, sc_ref, g3_ref, nw_ref,
                    wbd_ref, wbs_ref, wout_ref, wg_ref, wu_ref, w2_ref, fw_ref, o_ref, *, final):
    x = _mix(x_ref[...], ydn_ref[...], ysw_ref[...], gates_ref[...], g2_ref[0], wbd_ref, wbs_ref, wout_ref)
    y = _ffn_half_step(x, nw_ref[...], sc_ref[0], sh_ref[0], g3_ref[0], wg_ref, wu_ref, w2_ref)
    o_ref[...] = _final_norm(y, fw_ref[...]) if final else y


def _mix_ffn(x, y_dn, y_sw, gates, mod, rows, w, ffn_w, final_w, final, name):
    weights = (w['w_br_dn'], w['w_br_swa'], w['w_out']) + tuple(ffn_w)
    return pl.pallas_call(
        functools.partial(_mix_ffn_kernel, final=final),
        grid=rows.grid,
        in_specs=[rows.row_spec(D_MODEL), rows.row_spec(DN_V_W), rows.row_spec(SWA_Q_W),
                  rows.row_spec(2 * D_MODEL), rows.mod_spec(5), rows.mod_spec(6), rows.mod_spec(7),
                  rows.mod_spec(8), _resident((1, D_MODEL))]
                 + [_resident(a.shape) for a in weights] + [_resident((1, D_MODEL))],
        out_specs=rows.row_spec(D_MODEL),
        out_shape=jax.ShapeDtypeStruct((rows.n_rows, D_MODEL), F32),
        compiler_params=_cparams("parallel"),
        name=name,
    )(x, y_dn, y_sw, gates, mod, mod, mod, mod, w['norm_ffn2'], *weights, final_w)


def _mix_ffn_stream_kernel(x_ref, ydn_ref, ysw_ref, gates_ref, g2_ref, sh_ref, sc_ref, g3_ref, nw_ref,
                           wbd_ref, wbs_ref, wout_ref, fw_ref, wg_ref, wu_ref, w2_ref,
                           o_ref, wgb_ref, wub_ref, w2b_ref, x_scr, h_scr, acc_scr, *, final):
    j = pl.program_id(0)

    @pl.when(j == 0)
    def _():
        x = _mix(x_ref[...], ydn_ref[...], ysw_ref[...], gates_ref[...], g2_ref[0],
                 wbd_ref, wbs_ref, wout_ref)
        x_scr[...] = x
        h_scr[...] = _rms_mod(x, nw_ref[...], sc_ref[0], sh_ref[0]).astype(BF16)
        acc_scr[...] = jnp.zeros(acc_scr.shape, F32)

    _ffn_stream_step(h_scr, acc_scr, wg_ref, wu_ref, w2_ref, wgb_ref, wub_ref, w2b_ref)

    @pl.when(j == pl.num_programs(0) - 1)
    def _():
        y = x_scr[...] + (HALF_STEP * g3_ref[0]) * acc_scr[...]
        o_ref[...] = _final_norm(y, fw_ref[...]) if final else y


def _mix_ffn_stream(x, y_dn, y_sw, gates, mod, rows, w, w13, w2, final_w, final, name):
    assert rows.grid == (1,)
    const = lambda shape: pl.BlockSpec(shape, lambda j: (0,) * len(shape))
    mod_spec = lambda p: pl.BlockSpec(rows.mod_block, lambda j: (0, 0, p))
    mix_w = (w['w_br_dn'], w['w_br_swa'], w['w_out'])
    w_ins, w_outs, w_shapes = _ffn_stream_specs(w13, w2)
    tm = rows.tm
    return pl.pallas_call(
        functools.partial(_mix_ffn_stream_kernel, final=final),
        grid=(D_FF // FF_CHUNK,),
        in_specs=[const((tm, D_MODEL)), const((tm, DN_V_W)), const((tm, SWA_Q_W)), const((tm, 2 * D_MODEL)),
                  mod_spec(5), mod_spec(6), mod_spec(7), mod_spec(8), const((1, D_MODEL))]
                 + [const(a.shape) for a in mix_w] + [const((1, D_MODEL))] + w_ins,
        out_specs=[const((tm, D_MODEL))] + w_outs,
        out_shape=[jax.ShapeDtypeStruct((rows.n_rows, D_MODEL), F32)] + w_shapes,
        scratch_shapes=[pltpu.VMEM((tm, D_MODEL), F32), pltpu.VMEM((tm, D_MODEL), BF16),
                        pltpu.VMEM((tm, D_MODEL), F32)],
        compiler_params=_cparams("arbitrary"),
        name=name,
    )(x, y_dn, y_sw, gates, mod, mod, mod, mod, w['norm_ffn2'], *mix_w, final_w, w13, w13, w2)


def _gdn_step_prep_kernel(qkv_ref, cs_ref, ab_ref, cw_ref, alog_ref, dtb_ref,
                          cs_out_ref, qt_ref, kt_ref, v_ref, dec_ref, beta_ref):
    nb = qkv_ref.shape[0]
    for j in range(CONV_W - 2):
        cs_out_ref[j] = cs_ref[j + 1]
    cs_out_ref[CONV_W - 2] = qkv_ref[...]
    ab = ab_ref[...]
    dec = jnp.exp(-jnp.exp(alog_ref[...]) * jax.nn.softplus(ab + dtb_ref[...]))
    beta = jax.nn.sigmoid(ab)

    def conv_silu(c0):
        cols = slice(c0, c0 + LANES)
        y = qkv_ref[:, cols] * cw_ref[CONV_W - 1:CONV_W, cols]
        for j in range(CONV_W - 1):
            y = y + cs_ref[j, :, cols] * cw_ref[j:j + 1, cols]
        return _silu(y)

    for h in range(DN_HEADS):
        q = conv_silu(h * DN_DK)
        k = conv_silu(DN_QK_W + h * DN_DK)
        qn = q * lax.rsqrt(jnp.sum(q * q, axis=-1, keepdims=True) + EPS) * (DN_DK ** -0.5)
        kn = k * lax.rsqrt(jnp.sum(k * k, axis=-1, keepdims=True) + EPS)
        qt_ref[h] = qn.T
        kt_ref[h] = kn.T
        sl = slice(h * DN_DV, (h + 1) * DN_DV)
        v_ref[:, sl] = conv_silu(2 * DN_QK_W + h * DN_DV)
        dec_ref[:, sl] = jnp.broadcast_to(dec[:, h:h + 1], (nb, DN_DV))
        beta_ref[:, sl] = jnp.broadcast_to(beta[:, DN_HEADS + h:DN_HEADS + h + 1], (nb, DN_DV))


def _gdn_step_prep(qkv, conv_state, ab, conv_w, alog_pad, dtb_pad):
    nb = qkv.shape[0]
    full = _resident
    return pl.pallas_call(
        _gdn_step_prep_kernel,
        grid=(1,),
        in_specs=[full((nb, CONV_DIM)), full(conv_state.shape), full((nb, LANES)),
                  full((CONV_W, CONV_DIM)), full((1, LANES)), full((1, LANES))],
        out_specs=[full(conv_state.shape), full((DN_HEADS, DN_DK, nb)), full((DN_HEADS, DN_DK, nb)),
                   full((nb, DN_V_W)), full((nb, DN_V_W)), full((nb, DN_V_W))],
        out_shape=[jax.ShapeDtypeStruct(conv_state.shape, F32),
                   jax.ShapeDtypeStruct((DN_HEADS, DN_DK, nb), F32),
                   jax.ShapeDtypeStruct((DN_HEADS, DN_DK, nb), F32),
                   jax.ShapeDtypeStruct((nb, DN_V_W), F32),
                   jax.ShapeDtypeStruct((nb, DN_V_W), F32),
                   jax.ShapeDtypeStruct((nb, DN_V_W), F32)],
        compiler_params=_cparams("arbitrary"),
        name="gdn_step_prep",
    )(qkv, conv_state, ab, conv_w, alog_pad, dtb_pad)


def _gdn_step_kernel(s_ref, qt_ref, kt_ref, v_ref, dec_ref, beta_ref, z_ref, dnw_ref,
                     s_out_ref, y_ref, o_scr):
    nb = s_ref.shape[0]
    qt = qt_ref[0]
    kt = kt_ref[0]
    for b in range(nb):
        k_col = kt[:, b:b + 1]
        q_col = qt[:, b:b + 1]
        s1 = s_ref[b, 0] * dec_ref[b:b + 1, :]
        kv = jnp.sum(s1 * k_col, axis=0, keepdims=True)
        delta = (v_ref[b:b + 1, :] - kv) * beta_ref[b:b + 1, :]
        s2 = s1 + k_col * delta
        s_out_ref[b, 0] = s2
        o_scr[b:b + 1, :] = jnp.sum(s2 * q_col, axis=0, keepdims=True)
    o = o_scr[...]
    o = o * lax.rsqrt(jnp.mean(o * o, axis=-1, keepdims=True) + EPS) * dnw_ref[...]
    y_ref[...] = (o * z_ref[...].astype(F32)).astype(y_ref.dtype)


def _gdn_step(state, qt, kt, v, dec, beta, z, dn_norm):
    nb = state.shape[0]
    head_cols = pl.BlockSpec((nb, DN_DV), lambda h: (0, h))
    head_t = pl.BlockSpec((1, DN_DK, nb), lambda h: (h, 0, 0))
    s_spec = pl.BlockSpec((nb, 1, DN_DK, DN_DV), lambda h: (0, h, 0, 0))
    return pl.pallas_call(
        _gdn_step_kernel,
        grid=(DN_HEADS,),
        in_specs=[s_spec, head_t, head_t, head_cols, head_cols, head_cols, head_cols,
                  pl.BlockSpec((1, DN_DV), lambda h: (0, 0))],
        out_specs=[s_spec, head_cols],
        out_shape=[jax.ShapeDtypeStruct(state.shape, F32),
                   jax.ShapeDtypeStruct((nb, DN_V_W), BF16)],
        scratch_shapes=[pltpu.VMEM((nb, DN_DV), F32)],
        compiler_params=_cparams("parallel"),
        name="gdn_step",
    )(state, qt, kt, v, dec, beta, z, dn_norm)


SWA_STEP_BATCH = 16


def _swa_step_kernel(q_ref, kvn_ref, ck_ref, cv_ref, slo_ref, shi_ref, o_ref, ck_out_ref, cv_out_ref):
    length = ck_ref.shape[1]
    last = lax.broadcasted_iota(jnp.int32, (length, LANES), 0) == length - 1
    lo_k = lax.broadcasted_iota(jnp.int32, (length, LANES), 1) < SWA_HD
    pairs = SWA_Q_HEADS // 2
    first_kv = lax.broadcasted_iota(jnp.int32, (pairs, LANES), 0) < pairs // SWA_KV_HEADS
    lo_o = lax.broadcasted_iota(jnp.int32, (pairs, LANES), 1) < SWA_HD
    samples = range(q_ref.shape[0])
    k2 = [jnp.where(last, kvn_ref[b:b + 1, 0:LANES], pltpu.roll(ck_ref[b], length - 1, axis=0))
          for b in samples]
    v2 = [jnp.where(last, kvn_ref[b:b + 1, LANES:2 * LANES], pltpu.roll(cv_ref[b], length - 1, axis=0))
          for b in samples]
    for b in samples:
        ck_out_ref[b] = k2[b]
        cv_out_ref[b] = v2[b]
    views = [_kv_head_views(k2[b], v2[b], lo_k) for b in samples]
    k_cat = [jnp.concatenate([views[b][0][0], views[b][0][1], views[b][1][0], views[b][1][1]], axis=0)
             for b in samples]
    v_cat = [jnp.concatenate([views[b][2][0], views[b][2][1]], axis=1) for b in samples]
    scores = [_dot_nt(q_ref[b] * (SWA_HD ** -0.5), k_cat[b]) for b in samples]
    p, den = [], []
    for b in samples:
        halves_p, halves_den = [], []
        for half, sink_ref in enumerate((slo_ref, shi_ref)):
            s = jnp.where(first_kv, scores[b][:, 2 * half * length:(2 * half + 1) * length],
                          scores[b][:, (2 * half + 1) * length:(2 * half + 2) * length])
            sink = sink_ref[...]
            m = jnp.maximum(jnp.max(s, axis=-1, keepdims=True), sink)
            e = jnp.exp(s - m)
            halves_p.append(e)
            halves_den.append(jnp.sum(e, axis=-1, keepdims=True) + jnp.exp(sink - m))
        p.append(jnp.concatenate(halves_p, axis=0))
        den.append(halves_den)
    pv = [_dot(p[b], v_cat[b]) for b in samples]
    for b in samples:
        halves = [jnp.where(first_kv, pv[b][half * pairs:(half + 1) * pairs, :LANES],
                            pv[b][half * pairs:(half + 1) * pairs, LANES:]) / den[b][half]
                  for half in range(2)]
        o_ref[b] = jnp.where(lo_o, halves[0], halves[1]).astype(o_ref.dtype)


def _swa_step(q3, kv_new, cache_k, cache_v, sink_lo, sink_hi):
    nb, length, _ = cache_k.shape
    tb = SWA_STEP_BATCH
    pairs = SWA_Q_HEADS // 2
    q_spec = pl.BlockSpec((tb, pairs, LANES), lambda i: (i, 0, 0))
    c_spec = pl.BlockSpec((tb, length, LANES), lambda i: (i, 0, 0))
    sink_spec = pl.BlockSpec((pairs, LANES), lambda i: (0, 0))
    return pl.pallas_call(
        _swa_step_kernel,
        grid=(nb // tb,),
        in_specs=[q_spec, pl.BlockSpec((tb, 2 * SWA_KV_W), lambda i: (i, 0)), c_spec, c_spec,
                  sink_spec, sink_spec],
        out_specs=[q_spec, c_spec, c_spec],
        out_shape=[jax.ShapeDtypeStruct(q3.shape, BF16),
                   jax.ShapeDtypeStruct(cache_k.shape, F32),
                   jax.ShapeDtypeStruct(cache_v.shape, F32)],
        compiler_params=_cparams("parallel"),
        name="swa_step",
    )(q3, kv_new, cache_k, cache_v, sink_lo, sink_hi)


def _prep_layer_weights(lp):
    w_t, w_ab = _cast_w_in(jnp.swapaxes(lp['w_in'], 0, 1))
    pad_heads = lambda a: jnp.pad(a.astype(F32), (0, LANES - DN_HEADS)).reshape(1, LANES)
    sinks = lp['sinks'].astype(F32)
    pairs = SWA_Q_HEADS // 2
    return dict(
        w13_ffn1=lp['w13_ffn1'], w2_ffn1=lp['w2_ffn1'], w13_ffn2=lp['w13_ffn2'], w2_ffn2=lp['w2_ffn2'],
        w_t=w_t, w_ab=w_ab,
        w_br_dn=lp['w_br_dn'].astype(BF16), w_br_swa=lp['w_br_swa'].astype(BF16),
        w_out=lp['w_out'].astype(BF16),
        norm_ffn1=lp['norm_ffn1'].reshape(1, D_MODEL), norm_mix=lp['norm_mix'].reshape(1, D_MODEL),
        norm_ffn2=lp['norm_ffn2'].reshape(1, D_MODEL),
        conv_w=lp['conv_w'], alog_pad=pad_heads(lp['a_log']), dtb_pad=pad_heads(lp['dt_bias']),
        dn_norm=lp['dn_norm'].reshape(1, DN_DV), sinks=sinks,
        sink_lo=jnp.broadcast_to(sinks.reshape(pairs, 2)[:, 0:1], (pairs, LANES)),
        sink_hi=jnp.broadcast_to(sinks.reshape(pairs, 2)[:, 1:2], (pairs, LANES)),
    )


def _layer(x, mod, rows, w, final_w, final, past, ffn_bf16, n_batch, seq, tag):
    if past is None:
        x = _ffn(x, mod, rows, (0, 1, 2), w['norm_ffn1'], ffn_bf16[0], "ffn1_" + tag)
    else:
        x, *ffn1_b = _ffn_stream(x, mod, rows, (0, 1, 2), w['norm_ffn1'], w['w13_ffn1'], w['w2_ffn1'],
                                 "ffn1_" + tag)
    if past is None:
        in_rows = _Rows(n_batch, seq, INPROJ_ROW_TILE)
        qkv, z, q_sw, kv, gates, ab = _inproj(
            x, mod, in_rows, (3, 4), w['norm_mix'], w['w_t'], w['w_ab'], "inproj_" + tag)
        y_dn, s_new, tail = _gdn_prompt(qkv, z, ab, w['conv_w'], w['alog_pad'], w['dtb_pad'],
                                        w['dn_norm'], n_batch, seq)
        y_sw = _swa_prompt(q_sw, kv, w['sinks'], n_batch, seq)
        conv_new = tail[:, SUBLANES - (CONV_W - 1):]
        keep = min(WINDOW, seq)
        kv3 = kv.reshape(n_batch, seq, 2 * SWA_KV_W)[:, seq - keep:]
        k_buf = kv3[:, :, :SWA_KV_W].reshape(n_batch, keep, SWA_KV_HEADS, SWA_HD)
        v_buf = kv3[:, :, SWA_KV_W:].reshape(n_batch, keep, SWA_KV_HEADS, SWA_HD)
    else:
        s0, conv_buf, k_old, v_old = past
        length = k_old.shape[1]
        qkv, z, q_sw, kv, gates, ab = _inproj(
            x, mod, rows, (3, 4), w['norm_mix'], w['w_t'], w['w_ab'], "inproj_" + tag)
        conv_new, qt, kt, v, dec, beta = _gdn_step_prep(qkv, jnp.swapaxes(conv_buf, 0, 1), ab,
                                                        w['conv_w'], w['alog_pad'], w['dtb_pad'])
        conv_new = jnp.swapaxes(conv_new, 0, 1)
        s_new, y_dn = _gdn_step(s0, qt, kt, v, dec, beta, z, w['dn_norm'])
        o3, k_buf, v_buf = _swa_step(q_sw.reshape(n_batch, SWA_Q_HEADS // 2, LANES), kv,
                                     k_old.reshape(n_batch, length, SWA_KV_W),
                                     v_old.reshape(n_batch, length, SWA_KV_W),
                                     w['sink_lo'], w['sink_hi'])
        y_sw = o3.reshape(n_batch, SWA_Q_W)
        k_buf = k_buf.reshape(n_batch, length, SWA_KV_HEADS, SWA_HD)
        v_buf = v_buf.reshape(n_batch, length, SWA_KV_HEADS, SWA_HD)
    if past is None:
        x = _mix_ffn(x, y_dn, y_sw, gates, mod, rows, w, ffn_bf16[1], final_w, final, "mix_ffn2_" + tag)
        return x, (s_new, conv_new, k_buf, v_buf)
    x, *ffn2_b = _mix_ffn_stream(x, y_dn, y_sw, gates, mod, rows, w, w['w13_ffn2'], w['w2_ffn2'],
                                 final_w, final, "mix_ffn2_" + tag)
    return x, (s_new, conv_new, k_buf, v_buf), (ffn1_b, ffn2_b)


def kernel(x_prompt, x_sample, state_dn, state_conv, cache_swa_k, cache_swa_v, c_prompt, c_sample,
           w_ada, b_ada, norm_ffn1, w13_ffn1, w2_ffn1, norm_mix, w_in, conv_w, a_log, dt_bias,
           dn_norm, sinks, w_br_dn, w_br_swa, w_out, norm_ffn2, w13_ffn2, w2_ffn2, final_norm):
    n_p, seq_p, d = x_prompt.shape
    n_s, seq_s, _ = x_sample.shape
    depth = w_ada.shape[0]
    assert d == D_MODEL and seq_s == 1 and seq_p % ROW_TILE == 0 and seq_p % GDN_CHUNK == 0
    assert w_in.shape[2] == sum(IN_SPLITS) and w13_ffn1.shape[2] == 2 * D_FF
    assert cache_swa_k.shape[2] == WINDOW and n_s % SWA_STEP_BATCH == 0 and n_p % SUBLANES == 0
    rows_p = _Rows(n_p, seq_p, ROW_TILE)
    rows_s = _Rows(n_s, seq_s, ROW_TILE)
    c_all = jnp.concatenate([c_prompt, c_sample], axis=0)
    final_w = final_norm.reshape(1, D_MODEL)
    y_p = x_prompt.reshape(n_p * seq_p, d)
    y_s = x_sample.reshape(n_s * seq_s, d)
    st_p, st_s = [], []
    for l in range(depth):
        lp = dict(w_ada=w_ada[l], b_ada=b_ada[l], norm_ffn1=norm_ffn1[l], w13_ffn1=w13_ffn1[l],
                  w2_ffn1=w2_ffn1[l], norm_mix=norm_mix[l], w_in=w_in[l], conv_w=conv_w[l],
                  a_log=a_log[l], dt_bias=dt_bias[l], dn_norm=dn_norm[l], sinks=sinks[l],
                  w_br_dn=w_br_dn[l], w_br_swa=w_br_swa[l], w_out=w_out[l], norm_ffn2=norm_ffn2[l],
                  w13_ffn2=w13_ffn2[l], w2_ffn2=w2_ffn2[l])
        w = _prep_layer_weights(lp)
        mod_p, mod_s = _ada(c_all, lp['w_ada'], lp['b_ada'], n_p)
        final = l == depth - 1
        y_s, ss, ffn_bf16 = _layer(y_s, mod_s.reshape(rows_s.mod_shape), rows_s, w, final_w, final,
                                   (state_dn[l], state_conv[l], cache_swa_k[l], cache_swa_v[l]), None,
                                   n_s, seq_s, "sample")
        y_p, sp = _layer(y_p, mod_p.reshape(rows_p.mod_shape), rows_p, w, final_w, final, None, ffn_bf16,
                         n_p, seq_p, "prompt")
        st_p.append(sp)
        st_s.append(ss)
    stack = lambda sts, i: sts[0][i][None] if depth == 1 else jnp.stack([s[i] for s in sts])
    return (y_p.reshape(n_p, seq_p, d), y_s.reshape(n_s, seq_s, d),
            stack(st_p, 0), stack(st_s, 0), stack(st_p, 1), stack(st_s, 1),
            stack(st_p, 2), stack(st_s, 2), stack(st_p, 3), stack(st_s, 3))
```

```python
import functools

import jax
import jax.numpy as jnp
from jax import lax
from jax.experimental import pallas as pl
from jax.experimental.pallas import tpu as pltpu

F32 = jnp.float32
BF16 = jnp.bfloat16

D_MODEL = 1024
DN_HEADS = 8
DN_DK = 128
DN_DV = 128
DN_QK_W = DN_HEADS * DN_DK
DN_V_W = DN_HEADS * DN_DV
CONV_W = 4
CONV_DIM = 2 * DN_QK_W + DN_V_W
SWA_Q_HEADS = 16
SWA_KV_HEADS = 2
SWA_HD = 64
SWA_Q_W = SWA_Q_HEADS * SWA_HD
SWA_KV_W = SWA_KV_HEADS * SWA_HD
WINDOW = 128
D_FF = 2816
HALF_STEP = 0.5
N_MOD = 9
EPS = 1e-6
MASK_VALUE = -1e30
IN_SPLITS = (CONV_DIM, DN_V_W, DN_HEADS, DN_HEADS, SWA_Q_W, SWA_KV_W, SWA_KV_W, D_MODEL, D_MODEL)

LANES = 128
SUBLANES = 8
VMEM_LIMIT_BYTES = 56 * 1024 * 1024

GDN_CHUNK = 128
GDN_CHUNKS_PER_STEP = 4
SWA_BLOCK = 128
SWA_BLOCKS_PER_STEP = 2
FF_CHUNK = 256
ROW_TILE = 512
INPROJ_ROW_TILE = 512
FFN1_ROW_TILE = 1024
IN_COL_CHUNK = 512
ADA_COL_TILE = 3072


def _cparams(*sem):
    return pltpu.CompilerParams(dimension_semantics=sem, vmem_limit_bytes=VMEM_LIMIT_BYTES)


def _resident(shape):
    nd = len(shape)
    return pl.BlockSpec(shape, lambda *_: (0,) * nd, pipeline_mode=pl.Buffered(1))


def _dot(a, b):
    return jnp.dot(a.astype(BF16), b.astype(BF16), preferred_element_type=F32)


def _dot_nt(a, b):
    return lax.dot_general(a.astype(BF16), b.astype(BF16), (((1,), (1,)), ((), ())),
                           preferred_element_type=F32)


def _dot_tn(a, b):
    return lax.dot_general(a.astype(BF16), b.astype(BF16), (((0,), (0,)), ((), ())),
                           preferred_element_type=F32)


def _split3(a):
    hi = a.astype(BF16)
    r = a - hi.astype(F32)
    mid = r.astype(BF16)
    lo = (r - mid.astype(F32)).astype(BF16)
    return hi, mid, lo


def _dot3(a, b):
    a_hi = a.astype(BF16)
    a_lo = (a - a_hi.astype(F32)).astype(BF16)
    b_hi = b.astype(BF16)
    b_lo = (b - b_hi.astype(F32)).astype(BF16)
    d = functools.partial(jnp.dot, preferred_element_type=F32)
    return d(a_hi, b_hi) + (d(a_lo, b_hi) + d(a_hi, b_lo))


def _silu(x):
    return x * jax.nn.sigmoid(x)


def _rms_mod(x, nw, sc, sh):
    y = x * lax.rsqrt(jnp.mean(x * x, axis=-1, keepdims=True) + EPS)
    return (y * nw) * (1.0 + sc) + sh


def _ada_kernel(c_ref, w_ref, b_ref, op_ref, os_ref, *, n_prompt):
    m = _dot(_silu(c_ref[...]), w_ref[...]) + b_ref[...]
    op_ref[...] = m[:n_prompt]
    os_ref[...] = m[n_prompt:]


def _ada(c_all, w_ada, b_ada, n_prompt):
    n_all, d = c_all.shape
    n_out = w_ada.shape[1]
    tn = ADA_COL_TILE
    assert n_out % tn == 0
    return pl.pallas_call(
        functools.partial(_ada_kernel, n_prompt=n_prompt),
        grid=(n_out // tn,),
        in_specs=[pl.BlockSpec((n_all, d), lambda j: (0, 0)),
                  pl.BlockSpec((d, tn), lambda j: (0, j)),
                  pl.BlockSpec((1, tn), lambda j: (0, j))],
        out_specs=[pl.BlockSpec((n_prompt, tn), lambda j: (0, j)),
                   pl.BlockSpec((n_all - n_prompt, tn), lambda j: (0, j))],
        out_shape=[jax.ShapeDtypeStruct((n_prompt, n_out), F32),
                   jax.ShapeDtypeStruct((n_all - n_prompt, n_out), F32)],
        compiler_params=_cparams("arbitrary"),
        name="ada",
    )(c_all, w_ada, b_ada.reshape(1, n_out))


class _Rows:
    def __init__(self, n_batch, seq, row_tile):
        self.n_batch = n_batch
        self.n_rows = n_batch * seq
        if seq == 1:
            self.tm = n_batch
            self.mod_shape = (1, n_batch, N_MOD * D_MODEL)
            self.mod_block = (1, n_batch, D_MODEL)
            self.tiles_per_batch = None
        else:
            self.tm = min(row_tile, seq)
            assert seq % self.tm == 0
            self.mod_shape = (n_batch, 1, N_MOD * D_MODEL)
            self.mod_block = (1, 1, D_MODEL)
            self.tiles_per_batch = seq // self.tm
        self.grid = (self.n_rows // self.tm,)

    def mod_spec(self, piece):
        if self.tiles_per_batch is None:
            return pl.BlockSpec(self.mod_block, lambda i: (0, 0, piece))
        tpb = self.tiles_per_batch
        return pl.BlockSpec(self.mod_block, lambda i: (i // tpb, 0, piece))

    def row_spec(self, width):
        return pl.BlockSpec((self.tm, width), lambda i: (i, 0))


def _ffn_half_step(x, nw, sc, sh, g, wg_ref, wu_ref, w2_ref):
    h = _rms_mod(x, nw, sc, sh).astype(BF16)
    acc = jnp.zeros(x.shape, F32)
    for c0 in range(0, D_FF, FF_CHUNK):
        gate = jnp.dot(h, wg_ref[:, c0:c0 + FF_CHUNK], preferred_element_type=F32)
        up = jnp.dot(h, wu_ref[:, c0:c0 + FF_CHUNK], preferred_element_type=F32)
        act = (_silu(gate) * up).astype(BF16)
        acc = acc + jnp.dot(act, w2_ref[c0:c0 + FF_CHUNK, :], preferred_element_type=F32)
    return x + (HALF_STEP * g) * acc


def _ffn_kernel(x_ref, sh_ref, sc_ref, g_ref, nw_ref, wg_ref, wu_ref, w2_ref, o_ref):
    o_ref[...] = _ffn_half_step(x_ref[...], nw_ref[...], sc_ref[0], sh_ref[0], g_ref[0],
                                wg_ref, wu_ref, w2_ref)


def _ffn(x, mod, rows, pieces, nw, ffn_w, name):
    sh, sc, g = pieces
    return pl.pallas_call(
        _ffn_kernel,
        grid=rows.grid,
        in_specs=[rows.row_spec(D_MODEL), rows.mod_spec(sh), rows.mod_spec(sc), rows.mod_spec(g),
                  _resident((1, D_MODEL))] + [_resident(a.shape) for a in ffn_w],
        out_specs=rows.row_spec(D_MODEL),
        out_shape=jax.ShapeDtypeStruct((rows.n_rows, D_MODEL), F32),
        compiler_params=_cparams("parallel"),
        name=name,
    )(x, mod, mod, mod, nw, *ffn_w)


def _ffn_stream_step(h_scr, acc_scr, wg_ref, wu_ref, w2_ref, wgb_ref, wub_ref, w2b_ref):
    wg = wg_ref[...].astype(BF16)
    wu = wu_ref[...].astype(BF16)
    w2 = w2_ref[...].astype(BF16)
    wgb_ref[...] = wg
    wub_ref[...] = wu
    w2b_ref[...] = w2
    h = h_scr[...]
    gate = jnp.dot(h, wg, preferred_element_type=F32)
    up = jnp.dot(h, wu, preferred_element_type=F32)
    act = (_silu(gate) * up).astype(BF16)
    acc_scr[...] += jnp.dot(act, w2, preferred_element_type=F32)


def _ffn_stream_specs(w13, w2):
    d = w13.shape[0]
    n_chunks = D_FF // FF_CHUNK
    ins = [pl.BlockSpec((d, FF_CHUNK), lambda j: (0, j)),
           pl.BlockSpec((d, FF_CHUNK), lambda j: (0, n_chunks + j)),
           pl.BlockSpec((FF_CHUNK, d), lambda j: (j, 0))]
    outs = [pl.BlockSpec((d, FF_CHUNK), lambda j: (0, j)),
            pl.BlockSpec((d, FF_CHUNK), lambda j: (0, j)),
            pl.BlockSpec((FF_CHUNK, d), lambda j: (j, 0))]
    shapes = [jax.ShapeDtypeStruct((d, D_FF), BF16), jax.ShapeDtypeStruct((d, D_FF), BF16),
              jax.ShapeDtypeStruct((D_FF, d), BF16)]
    return ins, outs, shapes


def _ffn_stream_kernel(x_ref, sh_ref, sc_ref, g_ref, nw_ref, wg_ref, wu_ref, w2_ref,
                       o_ref, wgb_ref, wub_ref, w2b_ref, h_scr, acc_scr):
    j = pl.program_id(0)

    @pl.when(j == 0)
    def _():
        h_scr[...] = _rms_mod(x_ref[...], nw_ref[...], sc_ref[0], sh_ref[0]).astype(BF16)
        acc_scr[...] = jnp.zeros(acc_scr.shape, F32)

    _ffn_stream_step(h_scr, acc_scr, wg_ref, wu_ref, w2_ref, wgb_ref, wub_ref, w2b_ref)

    @pl.when(j == pl.num_programs(0) - 1)
    def _():
        o_ref[...] = x_ref[...] + (HALF_STEP * g_ref[0]) * acc_scr[...]


def _ffn_stream(x, mod, rows, pieces, nw, w13, w2, name):
    sh, sc, g = pieces
    assert rows.grid == (1,)
    const = lambda shape: pl.BlockSpec(shape, lambda j: (0,) * len(shape))
    w_ins, w_outs, w_shapes = _ffn_stream_specs(w13, w2)
    mod_spec = lambda p: pl.BlockSpec(rows.mod_block, lambda j: (0, 0, p))
    return pl.pallas_call(
        _ffn_stream_kernel,
        grid=(D_FF // FF_CHUNK,),
        in_specs=[const((rows.tm, D_MODEL)), mod_spec(sh), mod_spec(sc), mod_spec(g),
                  const((1, D_MODEL))] + w_ins,
        out_specs=[const((rows.tm, D_MODEL))] + w_outs,
        out_shape=[jax.ShapeDtypeStruct((rows.n_rows, D_MODEL), F32)] + w_shapes,
        scratch_shapes=[pltpu.VMEM((rows.tm, D_MODEL), BF16), pltpu.VMEM((rows.tm, D_MODEL), F32)],
        compiler_params=_cparams("arbitrary"),
        name=name,
    )(x, mod, mod, mod, nw, w13, w13, w2)


IN_A_PIECES = (CONV_DIM, DN_V_W)
IN_B_PIECES = (SWA_Q_W, 2 * SWA_KV_W, 2 * D_MODEL)
IN_OUT_DTYPES = (F32, BF16, BF16, F32, BF16, F32)


N_AB = 2 * DN_HEADS


W_IN_CAST_ROWS = 1024


def _cast_w_in_kernel(w_src, wab_src, w_ref, wab_ref):
    w_ref[...] = w_src[...].astype(BF16)
    row = lax.broadcasted_iota(jnp.int32, wab_ref.shape, 0)
    wab_ref[...] = jnp.where(row < N_AB, wab_src[...], 0.0)


def _cast_w_in(w_in_t):
    n, d = w_in_t.shape
    n_a = sum(IN_A_PIECES)
    assert n == n_a + N_AB + sum(IN_B_PIECES) and n_a % LANES == 0
    tr = W_IN_CAST_ROWS
    return pl.pallas_call(
        _cast_w_in_kernel,
        grid=(pl.cdiv(n, tr),),
        in_specs=[pl.BlockSpec((tr, d), lambda i: (i, 0)),
                  pl.BlockSpec((LANES, d), lambda i: (n_a // LANES, 0))],
        out_specs=[pl.BlockSpec((tr, d), lambda i: (i, 0)),
                   pl.BlockSpec((LANES, d), lambda i: (0, 0))],
        out_shape=[jax.ShapeDtypeStruct((n, d), BF16), jax.ShapeDtypeStruct((LANES, d), F32)],
        compiler_params=_cparams("arbitrary"),
        name="cast_w_in",
    )(w_in_t, w_in_t)


def _dot3_nt(a, b):
    a_hi = a.astype(BF16)
    a_lo = (a - a_hi.astype(F32)).astype(BF16)
    b_hi = b.astype(BF16)
    b_lo = (b - b_hi.astype(F32)).astype(BF16)
    return _dot_nt(a_hi, b_hi) + (_dot_nt(a_lo, b_hi) + _dot_nt(a_hi, b_lo))


def _inproj_kernel(x_ref, sh_ref, sc_ref, nw_ref, w_ref, wab_ref,
                   qkv_ref, z_ref, qsw_ref, kv_ref, gates_ref, ab_ref):
    h = _rms_mod(x_ref[...], nw_ref[...], sc_ref[0], sh_ref[0])
    hb = h.astype(BF16)

    def project(off, width, ref, act=None):
        for c0 in range(0, width, IN_COL_CHUNK):
            cw = min(IN_COL_CHUNK, width - c0)
            val = _dot_nt(hb, w_ref[off + c0:off + c0 + cw, :])
            ref[:, c0:c0 + cw] = (val if act is None else act(val)).astype(ref.dtype)

    project(0, CONV_DIM, qkv_ref)
    project(CONV_DIM, DN_V_W, z_ref, _silu)
    off = sum(IN_A_PIECES) + N_AB
    for ref, width in zip((qsw_ref, kv_ref, gates_ref), IN_B_PIECES):
        project(off, width, ref)
        off += width
    ab_ref[...] = _dot3_nt(h, wab_ref[...])


def _inproj(x, mod, rows, pieces, nw, w_t, w_ab, name):
    sh, sc = pieces
    widths = IN_A_PIECES + IN_B_PIECES + (LANES,)
    return pl.pallas_call(
        _inproj_kernel,
        grid=rows.grid,
        in_specs=[rows.row_spec(D_MODEL), rows.mod_spec(sh), rows.mod_spec(sc),
                  _resident((1, D_MODEL)), _resident(w_t.shape), _resident(w_ab.shape)],
        out_specs=[rows.row_spec(w) for w in widths],
        out_shape=[jax.ShapeDtypeStruct((rows.n_rows, w), dt) for w, dt in zip(widths, IN_OUT_DTYPES)],
        compiler_params=_cparams("parallel"),
        name=name,
    )(x, mod, mod, nw, w_t, w_ab)


def _unit_lower_inverse(mats, row, col):
    n = mats[0].shape[0]
    eye = jnp.where(row == col, 1.0, 0.0).astype(F32)
    in_block = (row >> 4) == (col >> 4)
    p = [jnp.where(in_block, -a, 0.0) for a in mats]
    r = [eye + pi for pi in p]
    q = [_dot(pi, pi) for pi in p]
    for _ in range(2):
        rq = [_dot(qi, jnp.concatenate([ri, qi], axis=1)) for qi, ri in zip(q, r)]
        r = [ri + rqi[:, :n] for ri, rqi in zip(r, rq)]
        q = [rqi[:, n:] for rqi in rq]
    x = [ri + _dot(qi, ri) for qi, ri in zip(q, r)]
    s = 4
    while (1 << s) < n:
        pair = ((row >> (s + 1)) == (col >> (s + 1))) & ((row >> s) > (col >> s))
        t = [_dot(jnp.where(pair, a, 0.0), xi) for a, xi in zip(mats, x)]
        x = [xi - _dot(xi, ti) for xi, ti in zip(x, t)]
        s += 1
    return x


def _gdn_prompt_kernel(qkv_ref, z_ref, ab_ref, cw_ref, alog_ref, dtb_ref, dnw_ref,
                       y_ref, s_out_ref, tail_ref, halo_ref, s_ref):
    t = pl.program_id(1)
    c = GDN_CHUNK
    rows = qkv_ref.shape[0]
    halo = SUBLANES
    heads = range(DN_HEADS)

    @pl.when(t == 0)
    def _():
        s_ref[...] = jnp.zeros(s_ref.shape, F32)
        halo_ref[...] = jnp.zeros(halo_ref.shape, F32)

    row = lax.broadcasted_iota(jnp.int32, (c, c), 0)
    col = lax.broadcasted_iota(jnp.int32, (c, c), 1)
    lower = row >= col
    strict = row > col
    tri = jnp.where(lower, 1.0, 0.0).astype(BF16)

    def l2n(a, scale=1.0):
        return a * (lax.rsqrt(jnp.sum(a * a, axis=-1, keepdims=True) + EPS) * scale)

    for ci in range(rows // c):
        r0 = ci * c

        def conv_silu(c0):
            cols = slice(c0, c0 + LANES)
            cur = qkv_ref[r0:r0 + c, cols]
            y = cur * cw_ref[CONV_W - 1:CONV_W, cols]
            if ci == 0:
                ext = jnp.concatenate([halo_ref[:, cols], cur], axis=0)
            for j in range(CONV_W - 1):
                lag = CONV_W - 1 - j
                if ci == 0:
                    shifted = ext[halo - lag:halo - lag + c]
                else:
                    shifted = qkv_ref[pl.ds(r0 - lag, c), cols]
                y = y + shifted * cw_ref[j:j + 1, cols]
            return _silu(y)

        ab = ab_ref[r0:r0 + c, :]
        g_log = -jnp.exp(alog_ref[...]) * jax.nn.softplus(ab + dtb_ref[...])
        beta_all = jax.nn.sigmoid(ab)
        gc = sum(jnp.dot(tri, piece, preferred_element_type=F32) for piece in _split3(g_log))
        gc_t = gc.T
        gc_last = gc[c - 1:c, :]

        qn = [l2n(conv_silu(h * DN_DK), DN_DK ** -0.5) for h in heads]
        kn = [l2n(conv_silu(DN_QK_W + h * DN_DK)) for h in heads]
        v = [conv_silu(2 * DN_QK_W + h * DN_DV) for h in heads]
        g_col = [gc[:, h:h + 1] for h in heads]
        g_end = [gc_last[:, h:h + 1] for h in heads]
        beta = [beta_all[:, DN_HEADS + h:DN_HEADS + h + 1] for h in heads]
        decay = [jnp.where(lower, jnp.exp(g_col[h] - gc_t[h:h + 1, :]), 0.0) for h in heads]
        kb = [kn[h] * beta[h] for h in heads]
        e_col = [jnp.exp(g_col[h]) for h in heads]

        kq = [_dot_nt(jnp.concatenate([kb[h], qn[h]], axis=0), kn[h]) for h in heads]
        a_mat = [jnp.where(strict, kq[h][:c] * decay[h], 0.0) for h in heads]
        qk = [kq[h][c:] * decay[h] for h in heads]
        x_inv = _unit_lower_inverse(a_mat, row, col)
        uw = [_dot(x_inv[h], jnp.concatenate([v[h] * beta[h], kb[h] * e_col[h]], axis=1)) for h in heads]
        s_old = [s_ref[h] for h in heads]
        ws = [_dot(jnp.concatenate([uw[h][:, DN_DV:], qn[h] * e_col[h]], axis=0), s_old[h]) for h in heads]
        v_new = [uw[h][:, :DN_DV] - ws[h][:c] for h in heads]
        o = [ws[h][c:] + _dot(qk[h], v_new[h]) for h in heads]
        for h in heads:
            k_dec = kn[h] * jnp.exp(g_end[h] - g_col[h])
            s_ref[h] = s_old[h] * jnp.exp(g_end[h]) + _dot_tn(k_dec, v_new[h])
        for h in heads:
            oh = o[h] * lax.rsqrt(jnp.mean(o[h] * o[h], axis=-1, keepdims=True) + EPS) * dnw_ref[...]
            z_act = z_ref[r0:r0 + c, h * DN_DV:(h + 1) * DN_DV].astype(F32)
            y_ref[r0:r0 + c, h * DN_DV:(h + 1) * DN_DV] = (oh * z_act).astype(y_ref.dtype)

    halo_ref[...] = qkv_ref[rows - halo:rows, :]

    @pl.when(t == pl.num_programs(1) - 1)
    def _():
        s_out_ref[0] = s_ref[...]
        tail_ref[0] = qkv_ref[rows - halo:rows, :]


def _gdn_prompt(qkv, z, ab, conv_w, alog_pad, dtb_pad, dn_norm, n_batch, seq):
    rows = GDN_CHUNK * GDN_CHUNKS_PER_STEP
    assert seq % rows == 0
    nt = seq // rows
    row_spec = lambda w: pl.BlockSpec((rows, w), lambda b, t: (b * nt + t, 0))
    const = lambda shape: pl.BlockSpec(shape, lambda b, t: (0,) * len(shape))
    return pl.pallas_call(
        _gdn_prompt_kernel,
        grid=(n_batch, nt),
        in_specs=[row_spec(CONV_DIM), row_spec(DN_V_W), row_spec(LANES),
                  const((CONV_W, CONV_DIM)), const((1, LANES)), const((1, LANES)), const((1, DN_DV))],
        out_specs=[row_spec(DN_V_W),
                   pl.BlockSpec((1, DN_HEADS, DN_DK, DN_DV), lambda b, t: (b, 0, 0, 0)),
                   pl.BlockSpec((1, SUBLANES, CONV_DIM), lambda b, t: (b, 0, 0))],
        out_shape=[jax.ShapeDtypeStruct((n_batch * seq, DN_V_W), BF16),
                   jax.ShapeDtypeStruct((n_batch, DN_HEADS, DN_DK, DN_DV), F32),
                   jax.ShapeDtypeStruct((n_batch, SUBLANES, CONV_DIM), F32)],
        scratch_shapes=[pltpu.VMEM((SUBLANES, CONV_DIM), F32),
                        pltpu.VMEM((DN_HEADS, DN_DK, DN_DV), F32)],
        compiler_params=_cparams("parallel", "arbitrary"),
        name="gdn_prompt",
    )(qkv, z, ab, conv_w, alog_pad, dtb_pad, dn_norm)


def _kv_head_views(k2, v2, lo):
    k2r = pltpu.roll(k2, SWA_HD, axis=1)
    v2r = pltpu.roll(v2, SWA_HD, axis=1)
    k_lo = (jnp.where(lo, k2, 0.0), jnp.where(lo, k2r, 0.0))
    k_hi = (jnp.where(lo, 0.0, k2r), jnp.where(lo, 0.0, k2))
    v_dup = (jnp.where(lo, v2, v2r), jnp.where(lo, v2r, v2))
    return k_lo, k_hi, v_dup


def _swa_prompt_kernel(sink_ref, q_ref, kvc_ref, kvp_ref, o_ref):
    n = pl.program_id(1)
    blk = SWA_BLOCK
    n_sub = q_ref.shape[0] // blk
    lo = lax.broadcasted_iota(jnp.int32, (blk, LANES), 1) < SWA_HD
    own = (lax.broadcasted_iota(jnp.int32, (blk, blk), 0)
           >= lax.broadcasted_iota(jnp.int32, (blk, blk), 1))
    pairs = SWA_Q_HEADS // 2
    pairs_per_kv = pairs // SWA_KV_HEADS
    views = [_kv_head_views(kvp_ref[:, :LANES], kvp_ref[:, LANES:], lo)]
    for sub in range(n_sub):
        rs = slice(sub * blk, (sub + 1) * blk)
        views.append(_kv_head_views(kvc_ref[rs, :LANES], kvc_ref[rs, LANES:], lo))
    k_cat = {(sub, c): jnp.concatenate([views[1 + sub][0][c], views[1 + sub][1][c],
                                        views[sub][0][c], views[sub][1][c]], axis=0).astype(BF16)
             for sub in range(n_sub) for c in range(SWA_KV_HEADS)}
    v_cat = {(sub, c): jnp.concatenate([views[1 + sub][2][c], views[sub][2][c]], axis=0).astype(BF16)
             for sub in range(n_sub) for c in range(SWA_KV_HEADS)}
    groups = [(sub, c) for sub in range(n_sub) for c in range(SWA_KV_HEADS)]

    def score_stage(sub, c):
        return [_dot_nt(q_ref[sub * blk:(sub + 1) * blk, j * LANES:(j + 1) * LANES]
                        * (SWA_HD ** -0.5),
                        k_cat[sub, c])
                for j in range(c * pairs_per_kv, (c + 1) * pairs_per_kv)]

    def finish(sub, c, scores):
        heads = [(jj, half) for jj in range(pairs_per_kv) for half in range(2)]
        p, den = [], []
        for jj, half in heads:
            s_own = scores[jj][:, half * blk:(half + 1) * blk]
            s_prev = scores[jj][:, (2 + half) * blk:(3 + half) * blk]
            if sub == 0:
                s_prev = jnp.where(n > 0, s_prev, MASK_VALUE)
            s = jnp.where(own, s_own, s_prev)
            sink = sink_ref[2 * (c * pairs_per_kv + jj) + half]
            m = jnp.maximum(jnp.max(s, axis=-1, keepdims=True), sink)
            e = jnp.exp(s - m)
            p.append(jnp.concatenate([jnp.where(own, e, 0.0), jnp.where(own, 0.0, e)], axis=1))
            den.append(jnp.sum(e, axis=-1, keepdims=True) + jnp.exp(sink - m))
        out = [_dot(p[i], v_cat[sub, c]) / den[i] for i in range(len(heads))]
        for jj in range(pairs_per_kv):
            j = c * pairs_per_kv + jj
            o_ref[sub * blk:(sub + 1) * blk, j * LANES:(j + 1) * LANES] = (
                jnp.where(lo, out[2 * jj], out[2 * jj + 1]).astype(o_ref.dtype))

    pending = score_stage(*groups[0])
    for g, (sub, c) in enumerate(groups):
        nxt = score_stage(*groups[g + 1]) if g + 1 < len(groups) else None
        finish(sub, c, pending)
        pending = nxt


def _swa_prompt(q, kv, sinks, n_batch, seq):
    blk = SWA_BLOCK
    n_sub = SWA_BLOCKS_PER_STEP
    assert seq % (blk * n_sub) == 0
    ns = seq // (blk * n_sub)
    step_rows = lambda w: pl.BlockSpec((blk * n_sub, w), lambda b, n: (b * ns + n, 0))
    return pl.pallas_call(
        _swa_prompt_kernel,
        grid=(n_batch, ns),
        in_specs=[pl.BlockSpec(memory_space=pltpu.SMEM),
                  step_rows(SWA_Q_W), step_rows(2 * SWA_KV_W),
                  pl.BlockSpec((blk, 2 * SWA_KV_W),
                               lambda b, n: ((b * ns + n) * n_sub - jnp.minimum(n, 1), 0))],
        out_specs=step_rows(SWA_Q_W),
        out_shape=jax.ShapeDtypeStruct((n_batch * seq, SWA_Q_W), BF16),
        compiler_params=_cparams("parallel", "arbitrary"),
        name="swa_prompt",
    )(sinks, q, kv, kv)


def _mix(x, ydn, ysw, gates, g2, wbd_ref, wbs_ref, wout_ref):
    a = _dot(ydn, wbd_ref[...])
    b = _dot(ysw, wbs_ref[...])
    merged = (jax.nn.sigmoid(gates[:, :D_MODEL].astype(F32)) * a
              + jax.nn.sigmoid(gates[:, D_MODEL:].astype(F32)) * b)
    return x + g2 * _dot(merged, wout_ref[...])


def _final_norm(y, fw):
    return y * lax.rsqrt(jnp.mean(y * y, axis=-1, keepdims=True) + EPS) * fw


def _mix_ffn_kernel(x_ref, ydn_ref, ysw_ref, gates_ref, g2_ref, sh_ref, sc_ref, g3_ref, nw_ref,
                    wbd_ref, wbs_ref, wout_ref, wg_ref, wu_ref, w2_ref, fw_ref, o_ref, *, final):
    x = _mix(x_ref[...], ydn_ref[...], ysw_ref[...], gates_ref[...], g2_ref[0], wbd_ref, wbs_ref, wout_ref)
    y = _ffn_half_step(x, nw_ref[...], sc_ref[0], sh_ref[0], g3_ref[0], wg_ref, wu_ref, w2_ref)
    o_ref[...] = _final_norm(y, fw_ref[...]) if final else y


def _mix_ffn(x, y_dn, y_sw, gates, mod, rows, w, ffn_w, final_w, final, name):
    weights = (w['w_br_dn'], w['w_br_swa'], w['w_out']) + tuple(ffn_w)
    return pl.pallas_call(
        functools.partial(_mix_ffn_kernel, final=final),
        grid=rows.grid,
        in_specs=[rows.row_spec(D_MODEL), rows.row_spec(DN_V_W), rows.row_spec(SWA_Q_W),
                  rows.row_spec(2 * D_MODEL), rows.mod_spec(5), rows.mod_spec(6), rows.mod_spec(7),
                  rows.mod_spec(8), _resident((1, D_MODEL))]
                 + [_resident(a.shape) for a in weights] + [_resident((1, D_MODEL))],
        out_specs=rows.row_spec(D_MODEL),
        out_shape=jax.ShapeDtypeStruct((rows.n_rows, D_MODEL), F32),
        compiler_params=_cparams("parallel"),
        name=name,
    )(x, y_dn, y_sw, gates, mod, mod, mod, mod, w['norm_ffn2'], *weights, final_w)


def _mix_ffn_stream_kernel(x_ref, ydn_ref, ysw_ref, gates_ref, g2_ref, sh_ref, sc_ref, g3_ref, nw_ref,
                           wbd_ref, wbs_ref, wout_ref, fw_ref, wg_ref, wu_ref, w2_ref,
                           o_ref, wgb_ref, wub_ref, w2b_ref, x_scr, h_scr, acc_scr, *, final):
    j = pl.program_id(0)

    @pl.when(j == 0)
    def _():
        x = _mix(x_ref[...], ydn_ref[...], ysw_ref[...], gates_ref[...], g2_ref[0],
                 wbd_ref, wbs_ref, wout_ref)
        x_scr[...] = x
        h_scr[...] = _rms_mod(x, nw_ref[...], sc_ref[0], sh_ref[0]).astype(BF16)
        acc_scr[...] = jnp.zeros(acc_scr.shape, F32)

    _ffn_stream_step(h_scr, acc_scr, wg_ref, wu_ref, w2_ref, wgb_ref, wub_ref, w2b_ref)

    @pl.when(j == pl.num_programs(0) - 1)
    def _():
        y = x_scr[...] + (HALF_STEP * g3_ref[0]) * acc_scr[...]
        o_ref[...] = _final_norm(y, fw_ref[...]) if final else y


def _mix_ffn_stream(x, y_dn, y_sw, gates, mod, rows, w, w13, w2, final_w, final, name):
    assert rows.grid == (1,)
    const = lambda shape: pl.BlockSpec(shape, lambda j: (0,) * len(shape))
    mod_spec = lambda p: pl.BlockSpec(rows.mod_block, lambda j: (0, 0, p))
    mix_w = (w['w_br_dn'], w['w_br_swa'], w['w_out'])
    w_ins, w_outs, w_shapes = _ffn_stream_specs(w13, w2)
    tm = rows.tm
    return pl.pallas_call(
        functools.partial(_mix_ffn_stream_kernel, final=final),
        grid=(D_FF // FF_CHUNK,),
        in_specs=[const((tm, D_MODEL)), const((tm, DN_V_W)), const((tm, SWA_Q_W)), const((tm, 2 * D_MODEL)),
                  mod_spec(5), mod_spec(6), mod_spec(7), mod_spec(8), const((1, D_MODEL))]
                 + [const(a.shape) for a in mix_w] + [const((1, D_MODEL))] + w_ins,
        out_specs=[const((tm, D_MODEL))] + w_outs,
        out_shape=[jax.ShapeDtypeStruct((rows.n_rows, D_MODEL), F32)] + w_shapes,
        scratch_shapes=[pltpu.VMEM((tm, D_MODEL), F32), pltpu.VMEM((tm, D_MODEL), BF16),
                        pltpu.VMEM((tm, D_MODEL), F32)],
        compiler_params=_cparams("arbitrary"),
        name=name,
    )(x, y_dn, y_sw, gates, mod, mod, mod, mod, w['norm_ffn2'], *mix_w, final_w, w13, w13, w2)


def _gdn_step_prep_kernel(qkv_ref, cs_ref, ab_ref, cw_ref, alog_ref, dtb_ref,
                          cs_out_ref, qt_ref, kt_ref, v_ref, dec_ref, beta_ref):
    nb = qkv_ref.shape[0]
    for j in range(CONV_W - 2):
        cs_out_ref[j] = cs_ref[j + 1]
    cs_out_ref[CONV_W - 2] = qkv_ref[...]
    ab = ab_ref[...]
    dec = jnp.exp(-jnp.exp(alog_ref[...]) * jax.nn.softplus(ab + dtb_ref[...]))
    beta = jax.nn.sigmoid(ab)

    def conv_silu(c0):
        cols = slice(c0, c0 + LANES)
        y = qkv_ref[:, cols] * cw_ref[CONV_W - 1:CONV_W, cols]
        for j in range(CONV_W - 1):
            y = y + cs_ref[j, :, cols] * cw_ref[j:j + 1, cols]
        return _silu(y)

    for h in range(DN_HEADS):
        q = conv_silu(h * DN_DK)
        k = conv_silu(DN_QK_W + h * DN_DK)
        qn = q * lax.rsqrt(jnp.sum(q * q, axis=-1, keepdims=True) + EPS) * (DN_DK ** -0.5)
        kn = k * lax.rsqrt(jnp.sum(k * k, axis=-1, keepdims=True) + EPS)
        qt_ref[h] = qn.T
        kt_ref[h] = kn.T
        sl = slice(h * DN_DV, (h + 1) * DN_DV)
        v_ref[:, sl] = conv_silu(2 * DN_QK_W + h * DN_DV)
        dec_ref[:, sl] = jnp.broadcast_to(dec[:, h:h + 1], (nb, DN_DV))
        beta_ref[:, sl] = jnp.broadcast_to(beta[:, DN_HEADS + h:DN_HEADS + h + 1], (nb, DN_DV))


def _gdn_step_prep(qkv, conv_state, ab, conv_w, alog_pad, dtb_pad):
    nb = qkv.shape[0]
    full = _resident
    return pl.pallas_call(
        _gdn_step_prep_kernel,
        grid=(1,),
        in_specs=[full((nb, CONV_DIM)), full(conv_state.shape), full((nb, LANES)),
                  full((CONV_W, CONV_DIM)), full((1, LANES)), full((1, LANES))],
        out_specs=[full(conv_state.shape), full((DN_HEADS, DN_DK, nb)), full((DN_HEADS, DN_DK, nb)),
                   full((nb, DN_V_W)), full((nb, DN_V_W)), full((nb, DN_V_W))],
        out_shape=[jax.ShapeDtypeStruct(conv_state.shape, F32),
                   jax.ShapeDtypeStruct((DN_HEADS, DN_DK, nb), F32),
                   jax.ShapeDtypeStruct((DN_HEADS, DN_DK, nb), F32),
                   jax.ShapeDtypeStruct((nb, DN_V_W), F32),
                   jax.ShapeDtypeStruct((nb, DN_V_W), F32),
                   jax.ShapeDtypeStruct((nb, DN_V_W), F32)],
        compiler_params=_cparams("arbitrary"),
        name="gdn_step_prep",
    )(qkv, conv_state, ab, conv_w, alog_pad, dtb_pad)


def _gdn_step_kernel(s_ref, qt_ref, kt_ref, v_ref, dec_ref, beta_ref, z_ref, dnw_ref,
                     s_out_ref, y_ref, o_scr):
    nb = s_ref.shape[0]
    qt = qt_ref[0]
    kt = kt_ref[0]
    for b in range(nb):
        k_col = kt[:, b:b + 1]
        q_col = qt[:, b:b + 1]
        s1 = s_ref[b, 0] * dec_ref[b:b + 1, :]
        kv = jnp.sum(s1 * k_col, axis=0, keepdims=True)
        delta = (v_ref[b:b + 1, :] - kv) * beta_ref[b:b + 1, :]
        s2 = s1 + k_col * delta
        s_out_ref[b, 0] = s2
        o_scr[b:b + 1, :] = jnp.sum(s2 * q_col, axis=0, keepdims=True)
    o = o_scr[...]
    o = o * lax.rsqrt(jnp.mean(o * o, axis=-1, keepdims=True) + EPS) * dnw_ref[...]
    y_ref[...] = (o * z_ref[...].astype(F32)).astype(y_ref.dtype)


def _gdn_step(state, qt, kt, v, dec, beta, z, dn_norm):
    nb = state.shape[0]
    head_cols = pl.BlockSpec((nb, DN_DV), lambda h: (0, h))
    head_t = pl.BlockSpec((1, DN_DK, nb), lambda h: (h, 0, 0))
    s_spec = pl.BlockSpec((nb, 1, DN_DK, DN_DV), lambda h: (0, h, 0, 0))
    return pl.pallas_call(
        _gdn_step_kernel,
        grid=(DN_HEADS,),
        in_specs=[s_spec, head_t, head_t, head_cols, head_cols, head_cols, head_cols,
                  pl.BlockSpec((1, DN_DV), lambda h: (0, 0))],
        out_specs=[s_spec, head_cols],
        out_shape=[jax.ShapeDtypeStruct(state.shape, F32),
                   jax.ShapeDtypeStruct((nb, DN_V_W), BF16)],
        scratch_shapes=[pltpu.VMEM((nb, DN_DV), F32)],
        compiler_params=_cparams("parallel"),
        name="gdn_step",
    )(state, qt, kt, v, dec, beta, z, dn_norm)


SWA_STEP_BATCH = 16


def _swa_step_kernel(q_ref, kvn_ref, ck_ref, cv_ref, slo_ref, shi_ref, o_ref, ck_out_ref, cv_out_ref):
    length = ck_ref.shape[1]
    last = lax.broadcasted_iota(jnp.int32, (length, LANES), 0) == length - 1
    lo_k = lax.broadcasted_iota(jnp.int32, (length, LANES), 1) < SWA_HD
    pairs = SWA_Q_HEADS // 2
    first_kv = lax.broadcasted_iota(jnp.int32, (pairs, LANES), 0) < pairs // SWA_KV_HEADS
    lo_o = lax.broadcasted_iota(jnp.int32, (pairs, LANES), 1) < SWA_HD
    samples = range(q_ref.shape[0])
    k2 = [jnp.where(last, kvn_ref[b:b + 1, 0:LANES], pltpu.roll(ck_ref[b], length - 1, axis=0))
          for b in samples]
    v2 = [jnp.where(last, kvn_ref[b:b + 1, LANES:2 * LANES], pltpu.roll(cv_ref[b], length - 1, axis=0))
          for b in samples]
    for b in samples:
        ck_out_ref[b] = k2[b]
        cv_out_ref[b] = v2[b]
    views = [_kv_head_views(k2[b], v2[b], lo_k) for b in samples]
    k_cat = [jnp.concatenate([views[b][0][0], views[b][0][1], views[b][1][0], views[b][1][1]], axis=0)
             for b in samples]
    v_cat = [jnp.concatenate([views[b][2][0], views[b][2][1]], axis=1) for b in samples]
    scores = [_dot_nt(q_ref[b] * (SWA_HD ** -0.5), k_cat[b]) for b in samples]
    p, den = [], []
    for b in samples:
        halves_p, halves_den = [], []
        for half, sink_ref in enumerate((slo_ref, shi_ref)):
            s = jnp.where(first_kv, scores[b][:, 2 * half * length:(2 * half + 1) * length],
                          scores[b][:, (2 * half + 1) * length:(2 * half + 2) * length])
            sink = sink_ref[...]
            m = jnp.maximum(jnp.max(s, axis=-1, keepdims=True), sink)
            e = jnp.exp(s - m)
            halves_p.append(e)
            halves_den.append(jnp.sum(e, axis=-1, keepdims=True) + jnp.exp(sink - m))
        p.append(jnp.concatenate(halves_p, axis=0))
        den.append(halves_den)
    pv = [_dot(p[b], v_cat[b]) for b in samples]
    for b in samples:
        halves = [jnp.where(first_kv, pv[b][half * pairs:(half + 1) * pairs, :LANES],
                            pv[b][half * pairs:(half + 1) * pairs, LANES:]) / den[b][half]
                  for half in range(2)]
        o_ref[b] = jnp.where(lo_o, halves[0], halves[1]).astype(o_ref.dtype)


def _swa_step(q3, kv_new, cache_k, cache_v, sink_lo, sink_hi):
    nb, length, _ = cache_k.shape
    tb = SWA_STEP_BATCH
    pairs = SWA_Q_HEADS // 2
    q_spec = pl.BlockSpec((tb, pairs, LANES), lambda i: (i, 0, 0))
    c_spec = pl.BlockSpec((tb, length, LANES), lambda i: (i, 0, 0))
    sink_spec = pl.BlockSpec((pairs, LANES), lambda i: (0, 0))
    return pl.pallas_call(
        _swa_step_kernel,
        grid=(nb // tb,),
        in_specs=[q_spec, pl.BlockSpec((tb, 2 * SWA_KV_W), lambda i: (i, 0)), c_spec, c_spec,
                  sink_spec, sink_spec],
        out_specs=[q_spec, c_spec, c_spec],
        out_shape=[jax.ShapeDtypeStruct(q3.shape, BF16),
                   jax.ShapeDtypeStruct(cache_k.shape, F32),
                   jax.ShapeDtypeStruct(cache_v.shape, F32)],
        compiler_params=_cparams("parallel"),
        name="swa_step",
    )(q3, kv_new, cache_k, cache_v, sink_lo, sink_hi)


def _prep_layer_weights(lp):
    w_t, w_ab = _cast_w_in(jnp.swapaxes(lp['w_in'], 0, 1))
    pad_heads = lambda a: jnp.pad(a.astype(F32), (0, LANES - DN_HEADS)).reshape(1, LANES)
    sinks = lp['sinks'].astype(F32)
    pairs = SWA_Q_HEADS // 2
    return dict(
        w13_ffn1=lp['w13_ffn1'], w2_ffn1=lp['w2_ffn1'], w13_ffn2=lp['w13_ffn2'], w2_ffn2=lp['w2_ffn2'],
        w_t=w_t, w_ab=w_ab,
        w_br_dn=lp['w_br_dn'].astype(BF16), w_br_swa=lp['w_br_swa'].astype(BF16),
        w_out=lp['w_out'].astype(BF16),
        norm_ffn1=lp['norm_ffn1'].reshape(1, D_MODEL), norm_mix=lp['norm_mix'].reshape(1, D_MODEL),
        norm_ffn2=lp['norm_ffn2'].reshape(1, D_MODEL),
        conv_w=lp['conv_w'], alog_pad=pad_heads(lp['a_log']), dtb_pad=pad_heads(lp['dt_bias']),
        dn_norm=lp['dn_norm'].reshape(1, DN_DV), sinks=sinks,
        sink_lo=jnp.broadcast_to(sinks.reshape(pairs, 2)[:, 0:1], (pairs, LANES)),
        sink_hi=jnp.broadcast_to(sinks.reshape(pairs, 2)[:, 1:2], (pairs, LANES)),
    )


def _layer(x, mod, rows, w, final_w, final, past, ffn_bf16, n_batch, seq, tag):
    if past is None:
        x = _ffn(x, mod, _Rows(n_batch, seq, FFN1_ROW_TILE), (0, 1, 2), w['norm_ffn1'], ffn_bf16[0],
                 "ffn1_" + tag)
    else:
        x, *ffn1_b = _ffn_stream(x, mod, rows, (0, 1, 2), w['norm_ffn1'], w['w13_ffn1'], w['w2_ffn1'],
                                 "ffn1_" + tag)
    if past is None:
        in_rows = _Rows(n_batch, seq, INPROJ_ROW_TILE)
        qkv, z, q_sw, kv, gates, ab = _inproj(
            x, mod, in_rows, (3, 4), w['norm_mix'], w['w_t'], w['w_ab'], "inproj_" + tag)
        y_dn, s_new, tail = _gdn_prompt(qkv, z, ab, w['conv_w'], w['alog_pad'], w['dtb_pad'],
                                        w['dn_norm'], n_batch, seq)
        y_sw = _swa_prompt(q_sw, kv, w['sinks'], n_batch, seq)
        conv_new = tail[:, SUBLANES - (CONV_W - 1):]
        keep = min(WINDOW, seq)
        kv3 = kv.reshape(n_batch, seq, 2 * SWA_KV_W)[:, seq - keep:]
        k_buf = kv3[:, :, :SWA_KV_W].reshape(n_batch, keep, SWA_KV_HEADS, SWA_HD)
        v_buf = kv3[:, :, SWA_KV_W:].reshape(n_batch, keep, SWA_KV_HEADS, SWA_HD)
    else:
        s0, conv_buf, k_old, v_old = past
        length = k_old.shape[1]
        qkv, z, q_sw, kv, gates, ab = _inproj(
            x, mod, rows, (3, 4), w['norm_mix'], w['w_t'], w['w_ab'], "inproj_" + tag)
        conv_new, qt, kt, v, dec, beta = _gdn_step_prep(qkv, jnp.swapaxes(conv_buf, 0, 1), ab,
                                                        w['conv_w'], w['alog_pad'], w['dtb_pad'])
        conv_new = jnp.swapaxes(conv_new, 0, 1)
        s_new, y_dn = _gdn_step(s0, qt, kt, v, dec, beta, z, w['dn_norm'])
        o3, k_buf, v_buf = _swa_step(q_sw.reshape(n_batch, SWA_Q_HEADS // 2, LANES), kv,
                                     k_old.reshape(n_batch, length, SWA_KV_W),
                                     v_old.reshape(n_batch, length, SWA_KV_W),
                                     w['sink_lo'], w['sink_hi'])
        y_sw = o3.reshape(n_batch, SWA_Q_W)
        k_buf = k_buf.reshape(n_batch, length, SWA_KV_HEADS, SWA_HD)
        v_buf = v_buf.reshape(n_batch, length, SWA_KV_HEADS, SWA_HD)
    if past is None:
        x = _mix_ffn(x, y_dn, y_sw, gates, mod, rows, w, ffn_bf16[1], final_w, final, "mix_ffn2_" + tag)
        return x, (s_new, conv_new, k_buf, v_buf)
    x, *ffn2_b = _mix_ffn_stream(x, y_dn, y_sw, gates, mod, rows, w, w['w13_ffn2'], w['w2_ffn2'],
                                 final_w, final, "mix_ffn2_" + tag)
    return x, (s_new, conv_new, k_buf, v_buf), (ffn1_b, ffn2_b)


def kernel(x_prompt, x_sample, state_dn, state_conv, cache_swa_k, cache_swa_v, c_prompt, c_sample,
           w_ada, b_ada, norm_ffn1, w13_ffn1, w2_ffn1, norm_mix, w_in, conv_w, a_log, dt_bias,
           dn_norm, sinks, w_br_dn, w_br_swa, w_out, norm_ffn2, w13_ffn2, w2_ffn2, final_norm):
    n_p, seq_p, d = x_prompt.shape
    n_s, seq_s, _ = x_sample.shape
    depth = w_ada.shape[0]
    assert d == D_MODEL and seq_s == 1 and seq_p % ROW_TILE == 0 and seq_p % GDN_CHUNK == 0
    assert w_in.shape[2] == sum(IN_SPLITS) and w13_ffn1.shape[2] == 2 * D_FF
    assert cache_swa_k.shape[2] == WINDOW and n_s % SWA_STEP_BATCH == 0 and n_p % SUBLANES == 0
    rows_p = _Rows(n_p, seq_p, ROW_TILE)
    rows_s = _Rows(n_s, seq_s, ROW_TILE)
    c_all = jnp.concatenate([c_prompt, c_sample], axis=0)
    final_w = final_norm.reshape(1, D_MODEL)
    y_p = x_prompt.reshape(n_p * seq_p, d)
    y_s = x_sample.reshape(n_s * seq_s, d)
    st_p, st_s = [], []
    for l in range(depth):
        lp = dict(w_ada=w_ada[l], b_ada=b_ada[l], norm_ffn1=norm_ffn1[l], w13_ffn1=w13_ffn1[l],
                  w2_ffn1=w2_ffn1[l], norm_mix=norm_mix[l], w_in=w_in[l], conv_w=conv_w[l],
                  a_log=a_log[l], dt_bias=dt_bias[l], dn_norm=dn_norm[l], sinks=sinks[l],
                  w_br_dn=w_br_dn[l], w_br_swa=w_br_swa[l], w_out=w_out[l], norm_ffn2=norm_ffn2[l],
                  w13_ffn2=w13_ffn2[l], w2_ffn2=w2_ffn2[l])
        w = _prep_layer_weights(lp)
        mod_p, mod_s = _ada(c_all, lp['w_ada'], lp['b_ada'], n_p)
        final = l == depth - 1
        y_s, ss, ffn_bf16 = _layer(y_s, mod_s.reshape(rows_s.mod_shape), rows_s, w, final_w, final,
                                   (state_dn[l], state_conv[l], cache_swa_k[l], cache_swa_v[l]), None,
                                   n_s, seq_s, "sample")
        y_p, sp = _layer(y_p, mod_p.reshape(rows_p.mod_shape), rows_p, w, final_w, final, None, ffn_bf16,
                         n_p, seq_p, "prompt")
        st_p.append(sp)
        st_s.append(ss)
    stack = lambda sts, i: sts[0][i][None] if depth == 1 else jnp.stack([s[i] for s in sts])
    return (y_p.reshape(n_p, seq_p, d), y_s.reshape(n_s, seq_s, d),
            stack(st_p, 0), stack(st_s, 0), stack(st_p, 1), stack(st_s, 1),
            stack(st_p, 2), stack(st_s, 2), stack(st_p, 3), stack(st_s, 3))
```

```python
import functools

import jax
import jax.numpy as jnp
from jax import lax
from jax.experimental import pallas as pl
from jax.experimental.pallas import tpu as pltpu

F32 = jnp.float32
BF16 = jnp.bfloat16

D_MODEL = 1024
DN_HEADS = 8
DN_DK = 128
DN_DV = 128
DN_QK_W = DN_HEADS * DN_DK
DN_V_W = DN_HEADS * DN_DV
CONV_W = 4
CONV_DIM = 2 * DN_QK_W + DN_V_W
SWA_Q_HEADS = 16
SWA_KV_HEADS = 2
SWA_HD = 64
SWA_Q_W = SWA_Q_HEADS * SWA_HD
SWA_KV_W = SWA_KV_HEADS * SWA_HD
WINDOW = 128
D_FF = 2816
HALF_STEP = 0.5
N_MOD = 9
EPS = 1e-6
MASK_VALUE = -1e30
IN_SPLITS = (CONV_DIM, DN_V_W, DN_HEADS, DN_HEADS, SWA_Q_W, SWA_KV_W, SWA_KV_W, D_MODEL, D_MODEL)

LANES = 128
SUBLANES = 8
VMEM_LIMIT_BYTES = 56 * 1024 * 1024

GDN_CHUNK = 128
GDN_CHUNKS_PER_STEP = 4
GDN_HEAD_GROUP = 8
STATE_RING = 3
SWA_BLOCK = 128
SWA_BLOCKS_PER_STEP = 2
FF_CHUNK = 256
ROW_TILE = 512
INPROJ_ROW_TILE = 512
FFN1_ROW_TILE = 1024
IN_COL_CHUNK = 512
ADA_COL_TILE = 3072


def _cparams(*sem):
    return pltpu.CompilerParams(dimension_semantics=sem, vmem_limit_bytes=VMEM_LIMIT_BYTES)


def _resident(shape):
    nd = len(shape)
    return pl.BlockSpec(shape, lambda *_: (0,) * nd, pipeline_mode=pl.Buffered(1))


def _dot(a, b):
    return jnp.dot(a.astype(BF16), b.astype(BF16), preferred_element_type=F32)


def _dot_nt(a, b):
    return lax.dot_general(a.astype(BF16), b.astype(BF16), (((1,), (1,)), ((), ())),
                           preferred_element_type=F32)


def _dot_tn(a, b):
    return lax.dot_general(a.astype(BF16), b.astype(BF16), (((0,), (0,)), ((), ())),
                           preferred_element_type=F32)


def _split3(a):
    hi = a.astype(BF16)
    r = a - hi.astype(F32)
    mid = r.astype(BF16)
    lo = (r - mid.astype(F32)).astype(BF16)
    return hi, mid, lo


def _dot3(a, b):
    a_hi = a.astype(BF16)
    a_lo = (a - a_hi.astype(F32)).astype(BF16)
    b_hi = b.astype(BF16)
    b_lo = (b - b_hi.astype(F32)).astype(BF16)
    d = functools.partial(jnp.dot, preferred_element_type=F32)
    return d(a_hi, b_hi) + (d(a_lo, b_hi) + d(a_hi, b_lo))


def _silu(x):
    return x * jax.nn.sigmoid(x)


def _rms_mod(x, nw, sc, sh):
    y = x * lax.rsqrt(jnp.mean(x * x, axis=-1, keepdims=True) + EPS)
    return (y * nw) * (1.0 + sc) + sh


def _ada_kernel(c_ref, w_ref, b_ref, op_ref, os_ref, *, n_prompt):
    m = _dot(_silu(c_ref[...]), w_ref[...]) + b_ref[...]
    op_ref[...] = m[:n_prompt]
    os_ref[...] = m[n_prompt:]


def _ada(c_all, w_ada, b_ada, n_prompt):
    n_all, d = c_all.shape
    n_out = w_ada.shape[1]
    tn = ADA_COL_TILE
    assert n_out % tn == 0
    return pl.pallas_call(
        functools.partial(_ada_kernel, n_prompt=n_prompt),
        grid=(n_out // tn,),
        in_specs=[pl.BlockSpec((n_all, d), lambda j: (0, 0)),
                  pl.BlockSpec((d, tn), lambda j: (0, j)),
                  pl.BlockSpec((1, tn), lambda j: (0, j))],
        out_specs=[pl.BlockSpec((n_prompt, tn), lambda j: (0, j)),
                   pl.BlockSpec((n_all - n_prompt, tn), lambda j: (0, j))],
        out_shape=[jax.ShapeDtypeStruct((n_prompt, n_out), F32),
                   jax.ShapeDtypeStruct((n_all - n_prompt, n_out), F32)],
        compiler_params=_cparams("arbitrary"),
        name="ada",
    )(c_all, w_ada, b_ada.reshape(1, n_out))


class _Rows:
    def __init__(self, n_batch, seq, row_tile):
        self.n_batch = n_batch
        self.n_rows = n_batch * seq
        if seq == 1:
            self.tm = n_batch
            self.mod_shape = (1, n_batch, N_MOD * D_MODEL)
            self.mod_block = (1, n_batch, D_MODEL)
            self.tiles_per_batch = None
        else:
            self.tm = min(row_tile, seq)
            assert seq % self.tm == 0
            self.mod_shape = (n_batch, 1, N_MOD * D_MODEL)
            self.mod_block = (1, 1, D_MODEL)
            self.tiles_per_batch = seq // self.tm
        self.grid = (self.n_rows // self.tm,)

    def mod_spec(self, piece):
        if self.tiles_per_batch is None:
            return pl.BlockSpec(self.mod_block, lambda i: (0, 0, piece))
        tpb = self.tiles_per_batch
        return pl.BlockSpec(self.mod_block, lambda i: (i // tpb, 0, piece))

    def row_spec(self, width):
        return pl.BlockSpec((self.tm, width), lambda i: (i, 0))


def _ffn_half_step(x, nw, sc, sh, g, wg_ref, wu_ref, w2_ref):
    h = _rms_mod(x, nw, sc, sh).astype(BF16)
    acc = jnp.zeros(x.shape, F32)
    for c0 in range(0, D_FF, FF_CHUNK):
        gate = jnp.dot(h, wg_ref[:, c0:c0 + FF_CHUNK], preferred_element_type=F32)
        up = jnp.dot(h, wu_ref[:, c0:c0 + FF_CHUNK], preferred_element_type=F32)
        act = (_silu(gate) * up).astype(BF16)
        acc = acc + jnp.dot(act, w2_ref[c0:c0 + FF_CHUNK, :], preferred_element_type=F32)
    return x + (HALF_STEP * g) * acc


def _ffn_kernel(x_ref, sh_ref, sc_ref, g_ref, nw_ref, wg_ref, wu_ref, w2_ref, o_ref):
    o_ref[...] = _ffn_half_step(x_ref[...], nw_ref[...], sc_ref[0], sh_ref[0], g_ref[0],
                                wg_ref, wu_ref, w2_ref)


def _ffn(x, mod, rows, pieces, nw, ffn_w, name):
    sh, sc, g = pieces
    return pl.pallas_call(
        _ffn_kernel,
        grid=rows.grid,
        in_specs=[rows.row_spec(D_MODEL), rows.mod_spec(sh), rows.mod_spec(sc), rows.mod_spec(g),
                  _resident((1, D_MODEL))] + [_resident(a.shape) for a in ffn_w],
        out_specs=rows.row_spec(D_MODEL),
        out_shape=jax.ShapeDtypeStruct((rows.n_rows, D_MODEL), F32),
        compiler_params=_cparams("parallel"),
        name=name,
    )(x, mod, mod, mod, nw, *ffn_w)


def _ffn_stream_step(h_scr, acc_scr, wg_ref, wu_ref, w2_ref, wgb_ref, wub_ref, w2b_ref):
    wg = wg_ref[...].astype(BF16)
    wu = wu_ref[...].astype(BF16)
    w2 = w2_ref[...].astype(BF16)
    wgb_ref[...] = wg
    wub_ref[...] = wu
    w2b_ref[...] = w2
    h = h_scr[...]
    gate = jnp.dot(h, wg, preferred_element_type=F32)
    up = jnp.dot(h, wu, preferred_element_type=F32)
    act = (_silu(gate) * up).astype(BF16)
    acc_scr[...] += jnp.dot(act, w2, preferred_element_type=F32)


def _ffn_stream_specs(w13, w2):
    d = w13.shape[0]
    n_chunks = D_FF // FF_CHUNK
    ins = [pl.BlockSpec((d, FF_CHUNK), lambda j: (0, j)),
           pl.BlockSpec((d, FF_CHUNK), lambda j: (0, n_chunks + j)),
           pl.BlockSpec((FF_CHUNK, d), lambda j: (j, 0))]
    outs = [pl.BlockSpec((d, FF_CHUNK), lambda j: (0, j)),
            pl.BlockSpec((d, FF_CHUNK), lambda j: (0, j)),
            pl.BlockSpec((FF_CHUNK, d), lambda j: (j, 0))]
    shapes = [jax.ShapeDtypeStruct((d, D_FF), BF16), jax.ShapeDtypeStruct((d, D_FF), BF16),
              jax.ShapeDtypeStruct((D_FF, d), BF16)]
    return ins, outs, shapes


def _ffn_stream_kernel(x_ref, sh_ref, sc_ref, g_ref, nw_ref, wg_ref, wu_ref, w2_ref,
                       o_ref, wgb_ref, wub_ref, w2b_ref, h_scr, acc_scr):
    j = pl.program_id(0)

    @pl.when(j == 0)
    def _():
        h_scr[...] = _rms_mod(x_ref[...], nw_ref[...], sc_ref[0], sh_ref[0]).astype(BF16)
        acc_scr[...] = jnp.zeros(acc_scr.shape, F32)

    _ffn_stream_step(h_scr, acc_scr, wg_ref, wu_ref, w2_ref, wgb_ref, wub_ref, w2b_ref)

    @pl.when(j == pl.num_programs(0) - 1)
    def _():
        o_ref[...] = x_ref[...] + (HALF_STEP * g_ref[0]) * acc_scr[...]


def _ffn_stream(x, mod, rows, pieces, nw, w13, w2, name):
    sh, sc, g = pieces
    assert rows.grid == (1,)
    const = lambda shape: pl.BlockSpec(shape, lambda j: (0,) * len(shape))
    w_ins, w_outs, w_shapes = _ffn_stream_specs(w13, w2)
    mod_spec = lambda p: pl.BlockSpec(rows.mod_block, lambda j: (0, 0, p))
    return pl.pallas_call(
        _ffn_stream_kernel,
        grid=(D_FF // FF_CHUNK,),
        in_specs=[const((rows.tm, D_MODEL)), mod_spec(sh), mod_spec(sc), mod_spec(g),
                  const((1, D_MODEL))] + w_ins,
        out_specs=[const((rows.tm, D_MODEL))] + w_outs,
        out_shape=[jax.ShapeDtypeStruct((rows.n_rows, D_MODEL), F32)] + w_shapes,
        scratch_shapes=[pltpu.VMEM((rows.tm, D_MODEL), BF16), pltpu.VMEM((rows.tm, D_MODEL), F32)],
        compiler_params=_cparams("arbitrary"),
        name=name,
    )(x, mod, mod, mod, nw, w13, w13, w2)


IN_A_PIECES = (CONV_DIM, DN_V_W)
IN_B_PIECES = (SWA_Q_W, 2 * SWA_KV_W, 2 * D_MODEL)
IN_OUT_DTYPES = (F32, BF16, BF16, F32, BF16, F32)


N_AB = 2 * DN_HEADS


W_IN_CAST_ROWS = 1024


def _cast_w_in_kernel(w_src, wab_src, w_ref, wab_ref):
    w_ref[...] = w_src[...].astype(BF16)
    row = lax.broadcasted_iota(jnp.int32, wab_ref.shape, 0)
    wab_ref[...] = jnp.where(row < N_AB, wab_src[...], 0.0)


def _cast_w_in(w_in_t):
    n, d = w_in_t.shape
    n_a = sum(IN_A_PIECES)
    assert n == n_a + N_AB + sum(IN_B_PIECES) and n_a % LANES == 0
    tr = W_IN_CAST_ROWS
    return pl.pallas_call(
        _cast_w_in_kernel,
        grid=(pl.cdiv(n, tr),),
        in_specs=[pl.BlockSpec((tr, d), lambda i: (i, 0)),
                  pl.BlockSpec((LANES, d), lambda i: (n_a // LANES, 0))],
        out_specs=[pl.BlockSpec((tr, d), lambda i: (i, 0)),
                   pl.BlockSpec((LANES, d), lambda i: (0, 0))],
        out_shape=[jax.ShapeDtypeStruct((n, d), BF16), jax.ShapeDtypeStruct((LANES, d), F32)],
        compiler_params=_cparams("arbitrary"),
        name="cast_w_in",
    )(w_in_t, w_in_t)


def _dot3_nt(a, b):
    a_hi = a.astype(BF16)
    a_lo = (a - a_hi.astype(F32)).astype(BF16)
    b_hi = b.astype(BF16)
    b_lo = (b - b_hi.astype(F32)).astype(BF16)
    return _dot_nt(a_hi, b_hi) + (_dot_nt(a_lo, b_hi) + _dot_nt(a_hi, b_lo))


def _inproj_kernel(x_ref, sh_ref, sc_ref, nw_ref, w_ref, wab_ref,
                   qkv_ref, z_ref, qsw_ref, kv_ref, gates_ref, ab_ref):
    h = _rms_mod(x_ref[...], nw_ref[...], sc_ref[0], sh_ref[0])
    hb = h.astype(BF16)

    def project(off, width, ref, act=None):
        for c0 in range(0, width, IN_COL_CHUNK):
            cw = min(IN_COL_CHUNK, width - c0)
            val = _dot_nt(hb, w_ref[off + c0:off + c0 + cw, :])
            ref[:, c0:c0 + cw] = (val if act is None else act(val)).astype(ref.dtype)

    project(0, CONV_DIM, qkv_ref)
    project(CONV_DIM, DN_V_W, z_ref, _silu)
    off = sum(IN_A_PIECES) + N_AB
    for ref, width in zip((qsw_ref, kv_ref, gates_ref), IN_B_PIECES):
        project(off, width, ref)
        off += width
    ab_ref[...] = _dot3_nt(h, wab_ref[...])


def _inproj(x, mod, rows, pieces, nw, w_t, w_ab, name):
    sh, sc = pieces
    widths = IN_A_PIECES + IN_B_PIECES + (LANES,)
    return pl.pallas_call(
        _inproj_kernel,
        grid=rows.grid,
        in_specs=[rows.row_spec(D_MODEL), rows.mod_spec(sh), rows.mod_spec(sc),
                  _resident((1, D_MODEL)), _resident(w_t.shape), _resident(w_ab.shape)],
        out_specs=[rows.row_spec(w) for w in widths],
        out_shape=[jax.ShapeDtypeStruct((rows.n_rows, w), dt) for w, dt in zip(widths, IN_OUT_DTYPES)],
        compiler_params=_cparams("parallel"),
        name=name,
    )(x, mod, mod, nw, w_t, w_ab)


def _unit_lower_inverse(mats, row, col):
    n = mats[0].shape[0]
    eye = jnp.where(row == col, 1.0, 0.0).astype(F32)
    in_block = (row >> 4) == (col >> 4)
    p = [jnp.where(in_block, -a, 0.0) for a in mats]
    r = [eye + pi for pi in p]
    q = [_dot(pi, pi) for pi in p]
    for _ in range(2):
        rq = [_dot(qi, jnp.concatenate([ri, qi], axis=1)) for qi, ri in zip(q, r)]
        r = [ri + rqi[:, :n] for ri, rqi in zip(r, rq)]
        q = [rqi[:, n:] for rqi in rq]
    x = [ri + _dot(qi, ri) for qi, ri in zip(q, r)]
    s = 4
    while (1 << s) < n:
        pair = ((row >> (s + 1)) == (col >> (s + 1))) & ((row >> s) > (col >> s))
        t = [_dot(jnp.where(pair, a, 0.0), xi) for a, xi in zip(mats, x)]
        x = [xi - _dot(xi, ti) for xi, ti in zip(x, t)]
        s += 1
    return x


def _gdn_prompt_kernel(qkv_ref, z_ref, ab_ref, cw_ref, alog_ref, dtb_ref, dnw_ref,
                       y_ref, s_out_ref, tail_ref, halo_ref, s_ref):
    t = pl.program_id(1)
    c = GDN_CHUNK
    rows = qkv_ref.shape[0]
    halo = SUBLANES
    heads = range(DN_HEADS)

    @pl.when(t == 0)
    def _():
        s_ref[...] = jnp.zeros(s_ref.shape, F32)
        halo_ref[...] = jnp.zeros(halo_ref.shape, F32)

    row = lax.broadcasted_iota(jnp.int32, (c, c), 0)
    col = lax.broadcasted_iota(jnp.int32, (c, c), 1)
    lower = row >= col
    strict = row > col
    tri = jnp.where(lower, 1.0, 0.0).astype(BF16)

    def l2n(a, scale=1.0):
        return a * (lax.rsqrt(jnp.sum(a * a, axis=-1, keepdims=True) + EPS) * scale)

    for ci in range(rows // c):
        r0 = ci * c

        def conv_silu(c0):
            cols = slice(c0, c0 + LANES)
            cur = qkv_ref[r0:r0 + c, cols]
            y = cur * cw_ref[CONV_W - 1:CONV_W, cols]
            if ci == 0:
                ext = jnp.concatenate([halo_ref[:, cols], cur], axis=0)
            for j in range(CONV_W - 1):
                lag = CONV_W - 1 - j
                if ci == 0:
                    shifted = ext[halo - lag:halo - lag + c]
                else:
                    shifted = qkv_ref[pl.ds(r0 - lag, c), cols]
                y = y + shifted * cw_ref[j:j + 1, cols]
            return _silu(y)

        ab = ab_ref[r0:r0 + c, :]
        g_log = -jnp.exp(alog_ref[...]) * jax.nn.softplus(ab + dtb_ref[...])
        beta_all = jax.nn.sigmoid(ab)
        gc = sum(jnp.dot(tri, piece, preferred_element_type=F32) for piece in _split3(g_log))
        gc_t = gc.T
        gc_last = gc[c - 1:c, :]

        for g0 in range(0, DN_HEADS, GDN_HEAD_GROUP):
            hs = range(g0, g0 + GDN_HEAD_GROUP)
            qn = {h: l2n(conv_silu(h * DN_DK), DN_DK ** -0.5) for h in hs}
            kn = {h: l2n(conv_silu(DN_QK_W + h * DN_DK)) for h in hs}
            v = {h: conv_silu(2 * DN_QK_W + h * DN_DV) for h in hs}
            g_col = {h: gc[:, h:h + 1] for h in hs}
            g_end = {h: gc_last[:, h:h + 1] for h in hs}
            beta = {h: beta_all[:, DN_HEADS + h:DN_HEADS + h + 1] for h in hs}
            decay = {h: jnp.where(lower, jnp.exp(g_col[h] - gc_t[h:h + 1, :]), 0.0) for h in hs}
            kb = {h: kn[h] * beta[h] for h in hs}
            e_col = {h: jnp.exp(g_col[h]) for h in hs}

            kq = {h: _dot_nt(jnp.concatenate([kb[h], qn[h]], axis=0), kn[h]) for h in hs}
            qk = {h: kq[h][c:] * decay[h] for h in hs}
            x_inv = dict(zip(hs, _unit_lower_inverse(
                [jnp.where(strict, kq[h][:c] * decay[h], 0.0) for h in hs], row, col)))
            uw = {h: _dot(x_inv[h], jnp.concatenate([v[h] * beta[h], kb[h] * e_col[h]], axis=1)) for h in hs}
            s_old = {h: s_ref[h] for h in hs}
            ws = {h: _dot(jnp.concatenate([uw[h][:, DN_DV:], qn[h] * e_col[h]], axis=0), s_old[h]) for h in hs}
            v_new = {h: uw[h][:, :DN_DV] - ws[h][:c] for h in hs}
            o = {h: ws[h][c:] + _dot(qk[h], v_new[h]) for h in hs}
            for h in hs:
                k_dec = kn[h] * jnp.exp(g_end[h] - g_col[h])
                s_ref[h] = s_old[h] * jnp.exp(g_end[h]) + _dot_tn(k_dec, v_new[h])
            for h in hs:
                oh = o[h] * lax.rsqrt(jnp.mean(o[h] * o[h], axis=-1, keepdims=True) + EPS) * dnw_ref[...]
                z_act = z_ref[r0:r0 + c, h * DN_DV:(h + 1) * DN_DV].astype(F32)
                y_ref[r0:r0 + c, h * DN_DV:(h + 1) * DN_DV] = (oh * z_act).astype(y_ref.dtype)

    halo_ref[...] = qkv_ref[rows - halo:rows, :]

    @pl.when(t == pl.num_programs(1) - 1)
    def _():
        s_out_ref[0] = s_ref[...]
        tail_ref[0] = qkv_ref[rows - halo:rows, :]


def _gdn_prompt(qkv, z, ab, conv_w, alog_pad, dtb_pad, dn_norm, n_batch, seq):
    rows = GDN_CHUNK * GDN_CHUNKS_PER_STEP
    assert seq % rows == 0
    nt = seq // rows
    row_spec = lambda w: pl.BlockSpec((rows, w), lambda b, t: (b * nt + t, 0))
    const = lambda shape: pl.BlockSpec(shape, lambda b, t: (0,) * len(shape))
    return pl.pallas_call(
        _gdn_prompt_kernel,
        grid=(n_batch, nt),
        in_specs=[row_spec(CONV_DIM), row_spec(DN_V_W), row_spec(LANES),
                  const((CONV_W, CONV_DIM)), const((1, LANES)), const((1, LANES)), const((1, DN_DV))],
        out_specs=[row_spec(DN_V_W),
                   pl.BlockSpec((1, DN_HEADS, DN_DK, DN_DV), lambda b, t: (b, 0, 0, 0)),
                   pl.BlockSpec((1, SUBLANES, CONV_DIM), lambda b, t: (b, 0, 0))],
        out_shape=[jax.ShapeDtypeStruct((n_batch * seq, DN_V_W), BF16),
                   jax.ShapeDtypeStruct((n_batch, DN_HEADS, DN_DK, DN_DV), F32),
                   jax.ShapeDtypeStruct((n_batch, SUBLANES, CONV_DIM), F32)],
        scratch_shapes=[pltpu.VMEM((SUBLANES, CONV_DIM), F32),
                        pltpu.VMEM((DN_HEADS, DN_DK, DN_DV), F32)],
        compiler_params=_cparams("parallel", "arbitrary"),
        name="gdn_prompt",
    )(qkv, z, ab, conv_w, alog_pad, dtb_pad, dn_norm)


def _kv_head_views(k2, v2, lo):
    k2r = pltpu.roll(k2, SWA_HD, axis=1)
    v2r = pltpu.roll(v2, SWA_HD, axis=1)
    k_lo = (jnp.where(lo, k2, 0.0), jnp.where(lo, k2r, 0.0))
    k_hi = (jnp.where(lo, 0.0, k2r), jnp.where(lo, 0.0, k2))
    v_dup = (jnp.where(lo, v2, v2r), jnp.where(lo, v2r, v2))
    return k_lo, k_hi, v_dup


def _swa_prompt_kernel(sink_ref, q_ref, kvc_ref, kvp_ref, o_ref):
    n = pl.program_id(1)
    blk = SWA_BLOCK
    n_sub = q_ref.shape[0] // blk
    lo = lax.broadcasted_iota(jnp.int32, (blk, LANES), 1) < SWA_HD
    own = (lax.broadcasted_iota(jnp.int32, (blk, blk), 0)
           >= lax.broadcasted_iota(jnp.int32, (blk, blk), 1))
    pairs = SWA_Q_HEADS // 2
    pairs_per_kv = pairs // SWA_KV_HEADS
    views = [_kv_head_views(kvp_ref[:, :LANES], kvp_ref[:, LANES:], lo)]
    for sub in range(n_sub):
        rs = slice(sub * blk, (sub + 1) * blk)
        views.append(_kv_head_views(kvc_ref[rs, :LANES], kvc_ref[rs, LANES:], lo))
    k_cat = {(sub, c): jnp.concatenate([views[1 + sub][0][c], views[1 + sub][1][c],
                                        views[sub][0][c], views[sub][1][c]], axis=0).astype(BF16)
             for sub in range(n_sub) for c in range(SWA_KV_HEADS)}
    v_cat = {(sub, c): jnp.concatenate([views[1 + sub][2][c], views[sub][2][c]], axis=0).astype(BF16)
             for sub in range(n_sub) for c in range(SWA_KV_HEADS)}
    groups = [(sub, c) for sub in range(n_sub) for c in range(SWA_KV_HEADS)]

    def score_stage(sub, c):
        return [_dot_nt(q_ref[sub * blk:(sub + 1) * blk, j * LANES:(j + 1) * LANES]
                        * (SWA_HD ** -0.5),
                        k_cat[sub, c])
                for j in range(c * pairs_per_kv, (c + 1) * pairs_per_kv)]

    def finish(sub, c, scores):
        heads = [(jj, half) for jj in range(pairs_per_kv) for half in range(2)]
        p, den = [], []
        for jj, half in heads:
            s_own = scores[jj][:, half * blk:(half + 1) * blk]
            s_prev = scores[jj][:, (2 + half) * blk:(3 + half) * blk]
            if sub == 0:
                s_prev = jnp.where(n > 0, s_prev, MASK_VALUE)
            s = jnp.where(own, s_own, s_prev)
            sink = sink_ref[2 * (c * pairs_per_kv + jj) + half]
            m = jnp.maximum(jnp.max(s, axis=-1, keepdims=True), sink)
            e = jnp.exp(s - m)
            p.append(jnp.concatenate([jnp.where(own, e, 0.0), jnp.where(own, 0.0, e)], axis=1))
            den.append(jnp.sum(e, axis=-1, keepdims=True) + jnp.exp(sink - m))
        out = [_dot(p[i], v_cat[sub, c]) / den[i] for i in range(len(heads))]
        for jj in range(pairs_per_kv):
            j = c * pairs_per_kv + jj
            o_ref[sub * blk:(sub + 1) * blk, j * LANES:(j + 1) * LANES] = (
                jnp.where(lo, out[2 * jj], out[2 * jj + 1]).astype(o_ref.dtype))

    pending = score_stage(*groups[0])
    for g, (sub, c) in enumerate(groups):
        nxt = score_stage(*groups[g + 1]) if g + 1 < len(groups) else None
        finish(sub, c, pending)
        pending = nxt


def _swa_prompt(q, kv, sinks, n_batch, seq):
    blk = SWA_BLOCK
    n_sub = SWA_BLOCKS_PER_STEP
    assert seq % (blk * n_sub) == 0
    ns = seq // (blk * n_sub)
    step_rows = lambda w: pl.BlockSpec((blk * n_sub, w), lambda b, n: (b * ns + n, 0))
    return pl.pallas_call(
        _swa_prompt_kernel,
        grid=(n_batch, ns),
        in_specs=[pl.BlockSpec(memory_space=pltpu.SMEM),
                  step_rows(SWA_Q_W), step_rows(2 * SWA_KV_W),
                  pl.BlockSpec((blk, 2 * SWA_KV_W),
                               lambda b, n: ((b * ns + n) * n_sub - jnp.minimum(n, 1), 0))],
        out_specs=step_rows(SWA_Q_W),
        out_shape=jax.ShapeDtypeStruct((n_batch * seq, SWA_Q_W), BF16),
        compiler_params=_cparams("parallel", "arbitrary"),
        name="swa_prompt",
    )(sinks, q, kv, kv)


def _mix(x, ydn, ysw, gates, g2, wbd_ref, wbs_ref, wout_ref):
    a = _dot(ydn, wbd_ref[...])
    b = _dot(ysw, wbs_ref[...])
    merged = (jax.nn.sigmoid(gates[:, :D_MODEL].astype(F32)) * a
              + jax.nn.sigmoid(gates[:, D_MODEL:].astype(F32)) * b)
    return x + g2 * _dot(merged, wout_ref[...])


def _final_norm(y, fw):
    return y * lax.rsqrt(jnp.mean(y * y, axis=-1, keepdims=True) + EPS) * fw


def _mix_ffn_kernel(x_ref, ydn_ref, ysw_ref, gates_ref, g2_ref, sh_ref, sc_ref, g3_ref, nw_ref,
                    wbd_ref, wbs_ref, wout_ref, wg_ref, wu_ref, w2_ref, fw_ref, o_ref, *, final):
    x = _mix(x_ref[...], ydn_ref[...], ysw_ref[...], gates_ref[...], g2_ref[0], wbd_ref, wbs_ref, wout_ref)
    y = _ffn_half_step(x, nw_ref[...], sc_ref[0], sh_ref[0], g3_ref[0], wg_ref, wu_ref, w2_ref)
    o_ref[...] = _final_norm(y, fw_ref[...]) if final else y


def _mix_ffn(x, y_dn, y_sw, gates, mod, rows, w, ffn_w, final_w, final, name):
    weights = (w['w_br_dn'], w['w_br_swa'], w['w_out']) + tuple(ffn_w)
    return pl.pallas_call(
        functools.partial(_mix_ffn_kernel, final=final),
        grid=rows.grid,
        in_specs=[rows.row_spec(D_MODEL), rows.row_spec(DN_V_W), rows.row_spec(SWA_Q_W),
                  rows.row_spec(2 * D_MODEL), rows.mod_spec(5), rows.mod_spec(6), rows.mod_spec(7),
                  rows.mod_spec(8), _resident((1, D_MODEL))]
                 + [_resident(a.shape) for a in weights] + [_resident((1, D_MODEL))],
        out_specs=rows.row_spec(D_MODEL),
        out_shape=jax.ShapeDtypeStruct((rows.n_rows, D_MODEL), F32),
        compiler_params=_cparams("parallel"),
        name=name,
    )(x, y_dn, y_sw, gates, mod, mod, mod, mod, w['norm_ffn2'], *weights, final_w)


def _mix_ffn_stream_kernel(x_ref, ydn_ref, ysw_ref, gates_ref, g2_ref, sh_ref, sc_ref, g3_ref, nw_ref,
                           wbd_ref, wbs_ref, wout_ref, fw_ref, wg_ref, wu_ref, w2_ref,
                           o_ref, wgb_ref, wub_ref, w2b_ref, x_scr, h_scr, acc_scr, *, final):
    j = pl.program_id(0)

    @pl.when(j == 0)
    def _():
        x = _mix(x_ref[...], ydn_ref[...], ysw_ref[...], gates_ref[...], g2_ref[0],
                 wbd_ref, wbs_ref, wout_ref)
        x_scr[...] = x
        h_scr[...] = _rms_mod(x, nw_ref[...], sc_ref[0], sh_ref[0]).astype(BF16)
        acc_scr[...] = jnp.zeros(acc_scr.shape, F32)

    _ffn_stream_step(h_scr, acc_scr, wg_ref, wu_ref, w2_ref, wgb_ref, wub_ref, w2b_ref)

    @pl.when(j == pl.num_programs(0) - 1)
    def _():
        y = x_scr[...] + (HALF_STEP * g3_ref[0]) * acc_scr[...]
        o_ref[...] = _final_norm(y, fw_ref[...]) if final else y


def _mix_ffn_stream(x, y_dn, y_sw, gates, mod, rows, w, w13, w2, final_w, final, name):
    assert rows.grid == (1,)
    const = lambda shape: pl.BlockSpec(shape, lambda j: (0,) * len(shape))
    mod_spec = lambda p: pl.BlockSpec(rows.mod_block, lambda j: (0, 0, p))
    mix_w = (w['w_br_dn'], w['w_br_swa'], w['w_out'])
    w_ins, w_outs, w_shapes = _ffn_stream_specs(w13, w2)
    tm = rows.tm
    return pl.pallas_call(
        functools.partial(_mix_ffn_stream_kernel, final=final),
        grid=(D_FF // FF_CHUNK,),
        in_specs=[const((tm, D_MODEL)), const((tm, DN_V_W)), const((tm, SWA_Q_W)), const((tm, 2 * D_MODEL)),
                  mod_spec(5), mod_spec(6), mod_spec(7), mod_spec(8), const((1, D_MODEL))]
                 + [const(a.shape) for a in mix_w] + [const((1, D_MODEL))] + w_ins,
        out_specs=[const((tm, D_MODEL))] + w_outs,
        out_shape=[jax.ShapeDtypeStruct((rows.n_rows, D_MODEL), F32)] + w_shapes,
        scratch_shapes=[pltpu.VMEM((tm, D_MODEL), F32), pltpu.VMEM((tm, D_MODEL), BF16),
                        pltpu.VMEM((tm, D_MODEL), F32)],
        compiler_params=_cparams("arbitrary"),
        name=name,
    )(x, y_dn, y_sw, gates, mod, mod, mod, mod, w['norm_ffn2'], *mix_w, final_w, w13, w13, w2)


def _gdn_step_prep_kernel(qkv_ref, cs_ref, ab_ref, cw_ref, alog_ref, dtb_ref,
                          cs_out_ref, qt_ref, kt_ref, v_ref, dec_ref, beta_ref):
    nb = qkv_ref.shape[0]
    for j in range(CONV_W - 2):
        cs_out_ref[j] = cs_ref[j + 1]
    cs_out_ref[CONV_W - 2] = qkv_ref[...]
    ab = ab_ref[...]
    dec = jnp.exp(-jnp.exp(alog_ref[...]) * jax.nn.softplus(ab + dtb_ref[...]))
    beta = jax.nn.sigmoid(ab)

    def conv_silu(c0):
        cols = slice(c0, c0 + LANES)
        y = qkv_ref[:, cols] * cw_ref[CONV_W - 1:CONV_W, cols]
        for j in range(CONV_W - 1):
            y = y + cs_ref[j, :, cols] * cw_ref[j:j + 1, cols]
        return _silu(y)

    for h in range(DN_HEADS):
        q = conv_silu(h * DN_DK)
        k = conv_silu(DN_QK_W + h * DN_DK)
        qn = q * lax.rsqrt(jnp.sum(q * q, axis=-1, keepdims=True) + EPS) * (DN_DK ** -0.5)
        kn = k * lax.rsqrt(jnp.sum(k * k, axis=-1, keepdims=True) + EPS)
        qt_ref[h] = qn.T
        kt_ref[h] = kn.T
        sl = slice(h * DN_DV, (h + 1) * DN_DV)
        v_ref[:, sl] = conv_silu(2 * DN_QK_W + h * DN_DV)
        dec_ref[:, sl] = jnp.broadcast_to(dec[:, h:h + 1], (nb, DN_DV))
        beta_ref[:, sl] = jnp.broadcast_to(beta[:, DN_HEADS + h:DN_HEADS + h + 1], (nb, DN_DV))


def _gdn_step_prep(qkv, conv_state, ab, conv_w, alog_pad, dtb_pad):
    nb = qkv.shape[0]
    full = _resident
    return pl.pallas_call(
        _gdn_step_prep_kernel,
        grid=(1,),
        in_specs=[full((nb, CONV_DIM)), full(conv_state.shape), full((nb, LANES)),
                  full((CONV_W, CONV_DIM)), full((1, LANES)), full((1, LANES))],
        out_specs=[full(conv_state.shape), full((DN_HEADS, DN_DK, nb)), full((DN_HEADS, DN_DK, nb)),
                   full((nb, DN_V_W)), full((nb, DN_V_W)), full((nb, DN_V_W))],
        out_shape=[jax.ShapeDtypeStruct(conv_state.shape, F32),
                   jax.ShapeDtypeStruct((DN_HEADS, DN_DK, nb), F32),
                   jax.ShapeDtypeStruct((DN_HEADS, DN_DK, nb), F32),
                   jax.ShapeDtypeStruct((nb, DN_V_W), F32),
                   jax.ShapeDtypeStruct((nb, DN_V_W), F32),
                   jax.ShapeDtypeStruct((nb, DN_V_W), F32)],
        compiler_params=_cparams("arbitrary"),
        name="gdn_step_prep",
    )(qkv, conv_state, ab, conv_w, alog_pad, dtb_pad)


def _gdn_step_kernel(s_hbm, qt_ref, kt_ref, v_ref, dec_ref, beta_ref, z_ref, dnw_ref,
                     s_out_ref, y_ref, ring, sems, o_scr):
    h = pl.program_id(0)
    n_heads = pl.num_programs(0)
    nb = s_out_ref.shape[0]

    def fetch(head):
        slot = head % STATE_RING
        return pltpu.make_async_copy(s_hbm.at[:, head], ring.at[slot], sems.at[slot])

    @pl.when(h == 0)
    def _():
        for head in range(STATE_RING - 1):
            fetch(head).start()

    @pl.when(h + STATE_RING - 1 < n_heads)
    def _():
        fetch(h + STATE_RING - 1).start()

    fetch(h).wait()
    s_ref = ring.at[h % STATE_RING]
    qt = qt_ref[0]
    kt = kt_ref[0]
    for b in range(nb):
        k_col = kt[:, b:b + 1]
        q_col = qt[:, b:b + 1]
        s1 = s_ref[b] * dec_ref[b:b + 1, :]
        kv = jnp.sum(s1 * k_col, axis=0, keepdims=True)
        delta = (v_ref[b:b + 1, :] - kv) * beta_ref[b:b + 1, :]
        s2 = s1 + k_col * delta
        s_out_ref[b, 0] = s2
        o_scr[b:b + 1, :] = jnp.sum(s2 * q_col, axis=0, keepdims=True)
    o = o_scr[...]
    o = o * lax.rsqrt(jnp.mean(o * o, axis=-1, keepdims=True) + EPS) * dnw_ref[...]
    y_ref[...] = (o * z_ref[...].astype(F32)).astype(y_ref.dtype)


def _gdn_step(state, qt, kt, v, dec, beta, z, dn_norm):
    nb = state.shape[0]
    head_cols = pl.BlockSpec((nb, DN_DV), lambda h: (0, h))
    head_t = pl.BlockSpec((1, DN_DK, nb), lambda h: (h, 0, 0))
    s_spec = pl.BlockSpec((nb, 1, DN_DK, DN_DV), lambda h: (0, h, 0, 0))
    return pl.pallas_call(
        _gdn_step_kernel,
        grid=(DN_HEADS,),
        in_specs=[pl.BlockSpec(memory_space=pl.ANY), head_t, head_t, head_cols, head_cols, head_cols,
                  head_cols, pl.BlockSpec((1, DN_DV), lambda h: (0, 0))],
        out_specs=[s_spec, head_cols],
        out_shape=[jax.ShapeDtypeStruct(state.shape, F32),
                   jax.ShapeDtypeStruct((nb, DN_V_W), BF16)],
        scratch_shapes=[pltpu.VMEM((STATE_RING, nb, DN_DK, DN_DV), F32),
                        pltpu.SemaphoreType.DMA((STATE_RING,)),
                        pltpu.VMEM((nb, DN_DV), F32)],
        compiler_params=_cparams("arbitrary"),
        name="gdn_step",
    )(state, qt, kt, v, dec, beta, z, dn_norm)


SWA_STEP_BATCH = 16


def _swa_step_kernel(q_ref, kvn_ref, ck_ref, cv_ref, slo_ref, shi_ref, o_ref, ck_out_ref, cv_out_ref):
    length = ck_ref.shape[1]
    last = lax.broadcasted_iota(jnp.int32, (length, LANES), 0) == length - 1
    lo_k = lax.broadcasted_iota(jnp.int32, (length, LANES), 1) < SWA_HD
    pairs = SWA_Q_HEADS // 2
    first_kv = lax.broadcasted_iota(jnp.int32, (pairs, LANES), 0) < pairs // SWA_KV_HEADS
    lo_o = lax.broadcasted_iota(jnp.int32, (pairs, LANES), 1) < SWA_HD
    samples = range(q_ref.shape[0])
    k2 = [jnp.where(last, kvn_ref[b:b + 1, 0:LANES], pltpu.roll(ck_ref[b], length - 1, axis=0))
          for b in samples]
    v2 = [jnp.where(last, kvn_ref[b:b + 1, LANES:2 * LANES], pltpu.roll(cv_ref[b], length - 1, axis=0))
          for b in samples]
    for b in samples:
        ck_out_ref[b] = k2[b]
        cv_out_ref[b] = v2[b]
    views = [_kv_head_views(k2[b], v2[b], lo_k) for b in samples]
    k_cat = [jnp.concatenate([views[b][0][0], views[b][0][1], views[b][1][0], views[b][1][1]], axis=0)
             for b in samples]
    v_cat = [jnp.concatenate([views[b][2][0], views[b][2][1]], axis=1) for b in samples]
    scores = [_dot_nt(q_ref[b] * (SWA_HD ** -0.5), k_cat[b]) for b in samples]
    p, den = [], []
    for b in samples:
        halves_p, halves_den = [], []
        for half, sink_ref in enumerate((slo_ref, shi_ref)):
            s = jnp.where(first_kv, scores[b][:, 2 * half * length:(2 * half + 1) * length],
                          scores[b][:, (2 * half + 1) * length:(2 * half + 2) * length])
            sink = sink_ref[...]
            m = jnp.maximum(jnp.max(s, axis=-1, keepdims=True), sink)
            e = jnp.exp(s - m)
            halves_p.append(e)
            halves_den.append(jnp.sum(e, axis=-1, keepdims=True) + jnp.exp(sink - m))
        p.append(jnp.concatenate(halves_p, axis=0))
        den.append(halves_den)
    pv = [_dot(p[b], v_cat[b]) for b in samples]
    for b in samples:
        halves = [jnp.where(first_kv, pv[b][half * pairs:(half + 1) * pairs, :LANES],
                            pv[b][half * pairs:(half + 1) * pairs, LANES:]) / den[b][half]
                  for half in range(2)]
        o_ref[b] = jnp.where(lo_o, halves[0], halves[1]).astype(o_ref.dtype)


def _swa_step(q3, kv_new, cache_k, cache_v, sink_lo, sink_hi):
    nb, length, _ = cache_k.shape
    tb = SWA_STEP_BATCH
    pairs = SWA_Q_HEADS // 2
    q_spec = pl.BlockSpec((tb, pairs, LANES), lambda i: (i, 0, 0))
    c_spec = pl.BlockSpec((tb, length, LANES), lambda i: (i, 0, 0))
    sink_spec = pl.BlockSpec((pairs, LANES), lambda i: (0, 0))
    return pl.pallas_call(
        _swa_step_kernel,
        grid=(nb // tb,),
        in_specs=[q_spec, pl.BlockSpec((tb, 2 * SWA_KV_W), lambda i: (i, 0)), c_spec, c_spec,
                  sink_spec, sink_spec],
        out_specs=[q_spec, c_spec, c_spec],
        out_shape=[jax.ShapeDtypeStruct(q3.shape, BF16),
                   jax.ShapeDtypeStruct(cache_k.shape, F32),
                   jax.ShapeDtypeStruct(cache_v.shape, F32)],
        compiler_params=_cparams("parallel"),
        name="swa_step",
    )(q3, kv_new, cache_k, cache_v, sink_lo, sink_hi)


def _prep_layer_weights(lp):
    w_t, w_ab = _cast_w_in(jnp.swapaxes(lp['w_in'], 0, 1))
    pad_heads = lambda a: jnp.pad(a.astype(F32), (0, LANES - DN_HEADS)).reshape(1, LANES)
    sinks = lp['sinks'].astype(F32)
    pairs = SWA_Q_HEADS // 2
    return dict(
        w13_ffn1=lp['w13_ffn1'], w2_ffn1=lp['w2_ffn1'], w13_ffn2=lp['w13_ffn2'], w2_ffn2=lp['w2_ffn2'],
        w_t=w_t, w_ab=w_ab,
        w_br_dn=lp['w_br_dn'].astype(BF16), w_br_swa=lp['w_br_swa'].astype(BF16),
        w_out=lp['w_out'].astype(BF16),
        norm_ffn1=lp['norm_ffn1'].reshape(1, D_MODEL), norm_mix=lp['norm_mix'].reshape(1, D_MODEL),
        norm_ffn2=lp['norm_ffn2'].reshape(1, D_MODEL),
        conv_w=lp['conv_w'], alog_pad=pad_heads(lp['a_log']), dtb_pad=pad_heads(lp['dt_bias']),
        dn_norm=lp['dn_norm'].reshape(1, DN_DV), sinks=sinks,
        sink_lo=jnp.broadcast_to(sinks.reshape(pairs, 2)[:, 0:1], (pairs, LANES)),
        sink_hi=jnp.broadcast_to(sinks.reshape(pairs, 2)[:, 1:2], (pairs, LANES)),
    )


def _layer(x, mod, rows, w, final_w, final, past, ffn_bf16, n_batch, seq, tag):
    if past is None:
        x = _ffn(x, mod, _Rows(n_batch, seq, FFN1_ROW_TILE), (0, 1, 2), w['norm_ffn1'], ffn_bf16[0],
                 "ffn1_" + tag)
    else:
        x, *ffn1_b = _ffn_stream(x, mod, rows, (0, 1, 2), w['norm_ffn1'], w['w13_ffn1'], w['w2_ffn1'],
                                 "ffn1_" + tag)
    if past is None:
        in_rows = _Rows(n_batch, seq, INPROJ_ROW_TILE)
        qkv, z, q_sw, kv, gates, ab = _inproj(
            x, mod, in_rows, (3, 4), w['norm_mix'], w['w_t'], w['w_ab'], "inproj_" + tag)
        y_dn, s_new, tail = _gdn_prompt(qkv, z, ab, w['conv_w'], w['alog_pad'], w['dtb_pad'],
                                        w['dn_norm'], n_batch, seq)
        y_sw = _swa_prompt(q_sw, kv, w['sinks'], n_batch, seq)
        conv_new = tail[:, SUBLANES - (CONV_W - 1):]
        keep = min(WINDOW, seq)
        kv3 = kv.reshape(n_batch, seq, 2 * SWA_KV_W)[:, seq - keep:]
        k_buf = kv3[:, :, :SWA_KV_W].reshape(n_batch, keep, SWA_KV_HEADS, SWA_HD)
        v_buf = kv3[:, :, SWA_KV_W:].reshape(n_batch, keep, SWA_KV_HEADS, SWA_HD)
    else:
        s0, conv_buf, k_old, v_old = past
        length = k_old.shape[1]
        qkv, z, q_sw, kv, gates, ab = _inproj(
            x, mod, rows, (3, 4), w['norm_mix'], w['w_t'], w['w_ab'], "inproj_" + tag)
        conv_new, qt, kt, v, dec, beta = _gdn_step_prep(qkv, jnp.swapaxes(conv_buf, 0, 1), ab,
                                                        w['conv_w'], w['alog_pad'], w['dtb_pad'])
        conv_new = jnp.swapaxes(conv_new, 0, 1)
        s_new, y_dn = _gdn_step(s0, qt, kt, v, dec, beta, z, w['dn_norm'])
        o3, k_buf, v_buf = _swa_step(q_sw.reshape(n_batch, SWA_Q_HEADS // 2, LANES), kv,
                                     k_old.reshape(n_batch, length, SWA_KV_W),
                                     v_old.reshape(n_batch, length, SWA_KV_W),
                                     w['sink_lo'], w['sink_hi'])
        y_sw = o3.reshape(n_batch, SWA_Q_W)
        k_buf = k_buf.reshape(n_batch, length, SWA_KV_HEADS, SWA_HD)
        v_buf = v_buf.reshape(n_batch, length, SWA_KV_HEADS, SWA_HD)
    if past is None:
        x = _mix_ffn(x, y_dn, y_sw, gates, mod, rows, w, ffn_bf16[1], final_w, final, "mix_ffn2_" + tag)
        return x, (s_new, conv_new, k_buf, v_buf)
    x, *ffn2_b = _mix_ffn_stream(x, y_dn, y_sw, gates, mod, rows, w, w['w13_ffn2'], w['w2_ffn2'],
                                 final_w, final, "mix_ffn2_" + tag)
    return x, (s_new, conv_new, k_buf, v_buf), (ffn1_b, ffn2_b)


def kernel(x_prompt, x_sample, state_dn, state_conv, cache_swa_k, cache_swa_v, c_prompt, c_sample,
           w_ada, b_ada, norm_ffn1, w13_ffn1, w2_ffn1, norm_mix, w_in, conv_w, a_log, dt_bias,
           dn_norm, sinks, w_br_dn, w_br_swa, w_out, norm_ffn2, w13_ffn2, w2_ffn2, final_norm):
    n_p, seq_p, d = x_prompt.shape
    n_s, seq_s, _ = x_sample.shape
    depth = w_ada.shape[0]
    assert d == D_MODEL and seq_s == 1 and seq_p % ROW_TILE == 0 and seq_p % GDN_CHUNK == 0
    assert w_in.shape[2] == sum(IN_SPLITS) and w13_ffn1.shape[2] == 2 * D_FF
    assert cache_swa_k.shape[2] == WINDOW and n_s % SWA_STEP_BATCH == 0 and n_p % SUBLANES == 0
    rows_p = _Rows(n_p, seq_p, ROW_TILE)
    rows_s = _Rows(n_s, seq_s, ROW_TILE)
    c_all = jnp.concatenate([c_prompt, c_sample], axis=0)
    final_w = final_norm.reshape(1, D_MODEL)
    y_p = x_prompt.reshape(n_p * seq_p, d)
    y_s = x_sample.reshape(n_s * seq_s, d)
    st_p, st_s = [], []
    for l in range(depth):
        lp = dict(w_ada=w_ada[l], b_ada=b_ada[l], norm_ffn1=norm_ffn1[l], w13_ffn1=w13_ffn1[l],
                  w2_ffn1=w2_ffn1[l], norm_mix=norm_mix[l], w_in=w_in[l], conv_w=conv_w[l],
                  a_log=a_log[l], dt_bias=dt_bias[l], dn_norm=dn_norm[l], sinks=sinks[l],
                  w_br_dn=w_br_dn[l], w_br_swa=w_br_swa[l], w_out=w_out[l], norm_ffn2=norm_ffn2[l],
                  w13_ffn2=w13_ffn2[l], w2_ffn2=w2_ffn2[l])
        w = _prep_layer_weights(lp)
        mod_p, mod_s = _ada(c_all, lp['w_ada'], lp['b_ada'], n_p)
        final = l == depth - 1
        y_s, ss, ffn_bf16 = _layer(y_s, mod_s.reshape(rows_s.mod_shape), rows_s, w, final_w, final,
                                   (state_dn[l], state_conv[l], cache_swa_k[l], cache_swa_v[l]), None,
                                   n_s, seq_s, "sample")
        y_p, sp = _layer(y_p, mod_p.reshape(rows_p.mod_shape), rows_p, w, final_w, final, None, ffn_bf16,
                         n_p, seq_p, "prompt")
        st_p.append(sp)
        st_s.append(ss)
    stack = lambda sts, i: sts[0][i][None] if depth == 1 else jnp.stack([s[i] for s in sts])
    return (y_p.reshape(n_p, seq_p, d), y_s.reshape(n_s, seq_s, d),
            stack(st_p, 0), stack(st_s, 0), stack(st_p, 1), stack(st_s, 1),
            stack(st_p, 2), stack(st_s, 2), stack(st_p, 3), stack(st_s, 3))
```

```python
import functools

import jax
import jax.numpy as jnp
from jax import lax
from jax.experimental import pallas as pl
from jax.experimental.pallas import tpu as pltpu

F32 = jnp.float32
BF16 = jnp.bfloat16

D_MODEL = 1024
DN_HEADS = 8
DN_DK = 128
DN_DV = 128
DN_QK_W = DN_HEADS * DN_DK
DN_V_W = DN_HEADS * DN_DV
CONV_W = 4
CONV_DIM = 2 * DN_QK_W + DN_V_W
SWA_Q_HEADS = 16
SWA_KV_HEADS = 2
SWA_HD = 64
SWA_Q_W = SWA_Q_HEADS * SWA_HD
SWA_KV_W = SWA_KV_HEADS * SWA_HD
WINDOW = 128
D_FF = 2816
HALF_STEP = 0.5
N_MOD = 9
EPS = 1e-6
MASK_VALUE = -1e30
IN_SPLITS = (CONV_DIM, DN_V_W, DN_HEADS, DN_HEADS, SWA_Q_W, SWA_KV_W, SWA_KV_W, D_MODEL, D_MODEL)

LANES = 128
SUBLANES = 8
VMEM_LIMIT_BYTES = 56 * 1024 * 1024

GDN_CHUNK = 128
GDN_CHUNKS_PER_STEP = 4
SWA_BLOCK = 128
SWA_BLOCKS_PER_STEP = 2
FF_CHUNK = 256
ROW_TILE = 512
INPROJ_ROW_TILE = 512
FFN1_ROW_TILE = 1024
IN_COL_CHUNK = 512
ADA_COL_TILE = 3072


def _cparams(*sem):
    return pltpu.CompilerParams(dimension_semantics=sem, vmem_limit_bytes=VMEM_LIMIT_BYTES)


def _resident(shape):
    nd = len(shape)
    return pl.BlockSpec(shape, lambda *_: (0,) * nd, pipeline_mode=pl.Buffered(1))


def _dot(a, b):
    return jnp.dot(a.astype(BF16), b.astype(BF16), preferred_element_type=F32)


def _dot_nt(a, b):
    return lax.dot_general(a.astype(BF16), b.astype(BF16), (((1,), (1,)), ((), ())),
                           preferred_element_type=F32)


def _dot_tn(a, b):
    return lax.dot_general(a.astype(BF16), b.astype(BF16), (((0,), (0,)), ((), ())),
                           preferred_element_type=F32)


def _split3(a):
    hi = a.astype(BF16)
    r = a - hi.astype(F32)
    mid = r.astype(BF16)
    lo = (r - mid.astype(F32)).astype(BF16)
    return hi, mid, lo


def _dot3(a, b):
    a_hi = a.astype(BF16)
    a_lo = (a - a_hi.astype(F32)).astype(BF16)
    b_hi = b.astype(BF16)
    b_lo = (b - b_hi.astype(F32)).astype(BF16)
    d = functools.partial(jnp.dot, preferred_element_type=F32)
    return d(a_hi, b_hi) + (d(a_lo, b_hi) + d(a_hi, b_lo))


def _silu(x):
    return x * jax.nn.sigmoid(x)


def _rms_mod(x, nw, sc, sh):
    y = x * lax.rsqrt(jnp.mean(x * x, axis=-1, keepdims=True) + EPS)
    return (y * nw) * (1.0 + sc) + sh


def _ada_kernel(c_ref, w_ref, b_ref, op_ref, os_ref, *, n_prompt):
    m = _dot(_silu(c_ref[...]), w_ref[...]) + b_ref[...]
    op_ref[...] = m[:n_prompt]
    os_ref[...] = m[n_prompt:]


def _ada(c_all, w_ada, b_ada, n_prompt):
    n_all, d = c_all.shape
    n_out = w_ada.shape[1]
    tn = ADA_COL_TILE
    assert n_out % tn == 0
    return pl.pallas_call(
        functools.partial(_ada_kernel, n_prompt=n_prompt),
        grid=(n_out // tn,),
        in_specs=[pl.BlockSpec((n_all, d), lambda j: (0, 0)),
                  pl.BlockSpec((d, tn), lambda j: (0, j)),
                  pl.BlockSpec((1, tn), lambda j: (0, j))],
        out_specs=[pl.BlockSpec((n_prompt, tn), lambda j: (0, j)),
                   pl.BlockSpec((n_all - n_prompt, tn), lambda j: (0, j))],
        out_shape=[jax.ShapeDtypeStruct((n_prompt, n_out), F32),
                   jax.ShapeDtypeStruct((n_all - n_prompt, n_out), F32)],
        compiler_params=_cparams("arbitrary"),
        name="ada",
    )(c_all, w_ada, b_ada.reshape(1, n_out))


class _Rows:
    def __init__(self, n_batch, seq, row_tile):
        self.n_batch = n_batch
        self.n_rows = n_batch * seq
        if seq == 1:
            self.tm = n_batch
            self.mod_shape = (1, n_batch, N_MOD * D_MODEL)
            self.mod_block = (1, n_batch, D_MODEL)
            self.tiles_per_batch = None
        else:
            self.tm = min(row_tile, seq)
            assert seq % self.tm == 0
            self.mod_shape = (n_batch, 1, N_MOD * D_MODEL)
            self.mod_block = (1, 1, D_MODEL)
            self.tiles_per_batch = seq // self.tm
        self.grid = (self.n_rows // self.tm,)

    def mod_spec(self, piece):
        if self.tiles_per_batch is None:
            return pl.BlockSpec(self.mod_block, lambda i: (0, 0, piece))
        tpb = self.tiles_per_batch
        return pl.BlockSpec(self.mod_block, lambda i: (i // tpb, 0, piece))

    def row_spec(self, width):
        return pl.BlockSpec((self.tm, width), lambda i: (i, 0))


def _ffn_half_step(x, nw, sc, sh, g, wg_ref, wu_ref, w2_ref):
    h = _rms_mod(x, nw, sc, sh).astype(BF16)
    acc = jnp.zeros(x.shape, F32)
    for c0 in range(0, D_FF, FF_CHUNK):
        gate = jnp.dot(h, wg_ref[:, c0:c0 + FF_CHUNK], preferred_element_type=F32)
        up = jnp.dot(h, wu_ref[:, c0:c0 + FF_CHUNK], preferred_element_type=F32)
        act = (_silu(gate) * up).astype(BF16)
        acc = acc + jnp.dot(act, w2_ref[c0:c0 + FF_CHUNK, :], preferred_element_type=F32)
    return x + (HALF_STEP * g) * acc


def _ffn_kernel(x_ref, sh_ref, sc_ref, g_ref, nw_ref, wg_ref, wu_ref, w2_ref, o_ref):
    o_ref[...] = _ffn_half_step(x_ref[...], nw_ref[...], sc_ref[0], sh_ref[0], g_ref[0],
                                wg_ref, wu_ref, w2_ref)


def _ffn(x, mod, rows, pieces, nw, ffn_w, name):
    sh, sc, g = pieces
    return pl.pallas_call(
        _ffn_kernel,
        grid=rows.grid,
        in_specs=[rows.row_spec(D_MODEL), rows.mod_spec(sh), rows.mod_spec(sc), rows.mod_spec(g),
                  _resident((1, D_MODEL))] + [_resident(a.shape) for a in ffn_w],
        out_specs=rows.row_spec(D_MODEL),
        out_shape=jax.ShapeDtypeStruct((rows.n_rows, D_MODEL), F32),
        compiler_params=_cparams("parallel"),
        name=name,
    )(x, mod, mod, mod, nw, *ffn_w)


def _ffn_stream_step(h_scr, acc_scr, wg_ref, wu_ref, w2_ref, wgb_ref, wub_ref, w2b_ref):
    wg = wg_ref[...].astype(BF16)
    wu = wu_ref[...].astype(BF16)
    w2 = w2_ref[...].astype(BF16)
    wgb_ref[...] = wg
    wub_ref[...] = wu
    w2b_ref[...] = w2
    h = h_scr[...]
    gate = jnp.dot(h, wg, preferred_element_type=F32)
    up = jnp.dot(h, wu, preferred_element_type=F32)
    act = (_silu(gate) * up).astype(BF16)
    acc_scr[...] += jnp.dot(act, w2, preferred_element_type=F32)


def _ffn_stream_specs(w13, w2):
    d = w13.shape[0]
    n_chunks = D_FF // FF_CHUNK
    ins = [pl.BlockSpec((d, FF_CHUNK), lambda j: (0, j)),
           pl.BlockSpec((d, FF_CHUNK), lambda j: (0, n_chunks + j)),
           pl.BlockSpec((FF_CHUNK, d), lambda j: (j, 0))]
    outs = [pl.BlockSpec((d, FF_CHUNK), lambda j: (0, j)),
            pl.BlockSpec((d, FF_CHUNK), lambda j: (0, j)),
            pl.BlockSpec((FF_CHUNK, d), lambda j: (j, 0))]
    shapes = [jax.ShapeDtypeStruct((d, D_FF), BF16), jax.ShapeDtypeStruct((d, D_FF), BF16),
              jax.ShapeDtypeStruct((D_FF, d), BF16)]
    return ins, outs, shapes


def _ffn_stream_kernel(x_ref, sh_ref, sc_ref, g_ref, nw_ref, wg_ref, wu_ref, w2_ref,
                       o_ref, wgb_ref, wub_ref, w2b_ref, h_scr, acc_scr):
    j = pl.program_id(0)

    @pl.when(j == 0)
    def _():
        h_scr[...] = _rms_mod(x_ref[...], nw_ref[...], sc_ref[0], sh_ref[0]).astype(BF16)
        acc_scr[...] = jnp.zeros(acc_scr.shape, F32)

    _ffn_stream_step(h_scr, acc_scr, wg_ref, wu_ref, w2_ref, wgb_ref, wub_ref, w2b_ref)

    @pl.when(j == pl.num_programs(0) - 1)
    def _():
        o_ref[...] = x_ref[...] + (HALF_STEP * g_ref[0]) * acc_scr[...]


def _ffn_stream(x, mod, rows, pieces, nw, w13, w2, name):
    sh, sc, g = pieces
    assert rows.grid == (1,)
    const = lambda shape: pl.BlockSpec(shape, lambda j: (0,) * len(shape))
    w_ins, w_outs, w_shapes = _ffn_stream_specs(w13, w2)
    mod_spec = lambda p: pl.BlockSpec(rows.mod_block, lambda j: (0, 0, p))
    return pl.pallas_call(
        _ffn_stream_kernel,
        grid=(D_FF // FF_CHUNK,),
        in_specs=[const((rows.tm, D_MODEL)), mod_spec(sh), mod_spec(sc), mod_spec(g),
                  const((1, D_MODEL))] + w_ins,
        out_specs=[const((rows.tm, D_MODEL))] + w_outs,
        out_shape=[jax.ShapeDtypeStruct((rows.n_rows, D_MODEL), F32)] + w_shapes,
        scratch_shapes=[pltpu.VMEM((rows.tm, D_MODEL), BF16), pltpu.VMEM((rows.tm, D_MODEL), F32)],
        compiler_params=_cparams("arbitrary"),
        name=name,
    )(x, mod, mod, mod, nw, w13, w13, w2)


IN_A_PIECES = (CONV_DIM, DN_V_W)
IN_B_PIECES = (SWA_Q_W, 2 * SWA_KV_W, 2 * D_MODEL)
IN_OUT_DTYPES = (F32, BF16, BF16, F32, BF16, F32)


N_AB = 2 * DN_HEADS


W_IN_CAST_ROWS = 1024


def _cast_w_in_kernel(w_src, wab_src, w_ref, wab_ref):
    w_ref[...] = w_src[...].astype(BF16)
    row = lax.broadcasted_iota(jnp.int32, wab_ref.shape, 0)
    wab_ref[...] = jnp.where(row < N_AB, wab_src[...], 0.0)


def _cast_w_in(w_in_t):
    n, d = w_in_t.shape
    n_a = sum(IN_A_PIECES)
    assert n == n_a + N_AB + sum(IN_B_PIECES) and n_a % LANES == 0
    tr = W_IN_CAST_ROWS
    return pl.pallas_call(
        _cast_w_in_kernel,
        grid=(pl.cdiv(n, tr),),
        in_specs=[pl.BlockSpec((tr, d), lambda i: (i, 0)),
                  pl.BlockSpec((LANES, d), lambda i: (n_a // LANES, 0))],
        out_specs=[pl.BlockSpec((tr, d), lambda i: (i, 0)),
                   pl.BlockSpec((LANES, d), lambda i: (0, 0))],
        out_shape=[jax.ShapeDtypeStruct((n, d), BF16), jax.ShapeDtypeStruct((LANES, d), F32)],
        compiler_params=_cparams("arbitrary"),
        name="cast_w_in",
    )(w_in_t, w_in_t)


def _dot3_nt(a, b):
    a_hi = a.astype(BF16)
    a_lo = (a - a_hi.astype(F32)).astype(BF16)
    b_hi = b.astype(BF16)
    b_lo = (b - b_hi.astype(F32)).astype(BF16)
    return _dot_nt(a_hi, b_hi) + (_dot_nt(a_lo, b_hi) + _dot_nt(a_hi, b_lo))


def _inproj_kernel(x_ref, sh_ref, sc_ref, nw_ref, w_ref, wab_ref, dnw_ref,
                   qkv_ref, z_ref, qsw_ref, kv_ref, gates_ref, ab_ref):
    h = _rms_mod(x_ref[...], nw_ref[...], sc_ref[0], sh_ref[0])
    hb = h.astype(BF16)

    def project(off, width, ref, act=None):
        for c0 in range(0, width, IN_COL_CHUNK):
            cw = min(IN_COL_CHUNK, width - c0)
            val = _dot_nt(hb, w_ref[off + c0:off + c0 + cw, :])
            ref[:, c0:c0 + cw] = (val if act is None else act(val, c0, cw)).astype(ref.dtype)

    project(0, CONV_DIM, qkv_ref)
    project(CONV_DIM, DN_V_W, z_ref, lambda val, c0, cw: _silu(val) * dnw_ref[:, c0:c0 + cw])
    off = sum(IN_A_PIECES) + N_AB
    for ref, width in zip((qsw_ref, kv_ref, gates_ref), IN_B_PIECES):
        project(off, width, ref)
        off += width
    ab_ref[...] = _dot3_nt(h, wab_ref[...])


def _inproj(x, mod, rows, pieces, nw, w_t, w_ab, dnw_heads, name):
    sh, sc = pieces
    widths = IN_A_PIECES + IN_B_PIECES + (LANES,)
    return pl.pallas_call(
        _inproj_kernel,
        grid=rows.grid,
        in_specs=[rows.row_spec(D_MODEL), rows.mod_spec(sh), rows.mod_spec(sc),
                  _resident((1, D_MODEL)), _resident(w_t.shape), _resident(w_ab.shape),
                  _resident(dnw_heads.shape)],
        out_specs=[rows.row_spec(w) for w in widths],
        out_shape=[jax.ShapeDtypeStruct((rows.n_rows, w), dt) for w, dt in zip(widths, IN_OUT_DTYPES)],
        compiler_params=_cparams("parallel"),
        name=name,
    )(x, mod, mod, nw, w_t, w_ab, dnw_heads)


def _unit_lower_inverse(mats, row, col):
    n = mats[0].shape[0]
    eye = jnp.where(row == col, 1.0, 0.0).astype(F32)
    in_block = (row >> 4) == (col >> 4)
    p = [jnp.where(in_block, -a, 0.0) for a in mats]
    r = [eye + pi for pi in p]
    q = [_dot(pi, pi) for pi in p]
    for _ in range(2):
        rq = [_dot(qi, jnp.concatenate([ri, qi], axis=1)) for qi, ri in zip(q, r)]
        r = [ri + rqi[:, :n] for ri, rqi in zip(r, rq)]
        q = [rqi[:, n:] for rqi in rq]
    x = [ri + _dot(qi, ri) for qi, ri in zip(q, r)]
    s = 4
    while (1 << s) < n:
        pair = ((row >> (s + 1)) == (col >> (s + 1))) & ((row >> s) > (col >> s))
        t = [_dot(jnp.where(pair, a, 0.0), xi) for a, xi in zip(mats, x)]
        x = [xi - _dot(xi, ti) for xi, ti in zip(x, t)]
        s += 1
    return x


def _gdn_prompt_kernel(qkv_ref, z_ref, ab_ref, cw_ref, alog_ref, dtb_ref, dnw_ref,
                       y_ref, s_out_ref, tail_ref, halo_ref, s_ref):
    t = pl.program_id(1)
    c = GDN_CHUNK
    rows = qkv_ref.shape[0]
    halo = SUBLANES
    heads = range(DN_HEADS)

    @pl.when(t == 0)
    def _():
        s_ref[...] = jnp.zeros(s_ref.shape, F32)
        halo_ref[...] = jnp.zeros(halo_ref.shape, F32)

    row = lax.broadcasted_iota(jnp.int32, (c, c), 0)
    col = lax.broadcasted_iota(jnp.int32, (c, c), 1)
    lower = row >= col
    strict = row > col
    tri = jnp.where(lower, 1.0, 0.0).astype(BF16)

    def l2n(a, scale=1.0):
        return a * (lax.rsqrt(jnp.sum(a * a, axis=-1, keepdims=True) + EPS) * scale)

    for ci in range(rows // c):
        r0 = ci * c

        def conv_silu(c0):
            cols = slice(c0, c0 + LANES)
            cur = qkv_ref[r0:r0 + c, cols]
            y = cur * cw_ref[CONV_W - 1:CONV_W, cols]
            if ci == 0:
                ext = jnp.concatenate([halo_ref[:, cols], cur], axis=0)
            for j in range(CONV_W - 1):
                lag = CONV_W - 1 - j
                if ci == 0:
                    shifted = ext[halo - lag:halo - lag + c]
                else:
                    shifted = qkv_ref[pl.ds(r0 - lag, c), cols]
                y = y + shifted * cw_ref[j:j + 1, cols]
            return _silu(y)

        ab = ab_ref[r0:r0 + c, :]
        g_log = -jnp.exp(alog_ref[...]) * jax.nn.softplus(ab + dtb_ref[...])
        beta_all = jax.nn.sigmoid(ab)
        gc = sum(jnp.dot(tri, piece, preferred_element_type=F32) for piece in _split3(g_log))
        gc_t = gc.T
        gc_last = gc[c - 1:c, :]

        qn = [l2n(conv_silu(h * DN_DK), DN_DK ** -0.5) for h in heads]
        kn = [l2n(conv_silu(DN_QK_W + h * DN_DK)) for h in heads]
        v = [conv_silu(2 * DN_QK_W + h * DN_DV) for h in heads]
        g_col = [gc[:, h:h + 1] for h in heads]
        g_end = [gc_last[:, h:h + 1] for h in heads]
        beta = [beta_all[:, DN_HEADS + h:DN_HEADS + h + 1] for h in heads]
        decay = [jnp.where(lower, jnp.exp(g_col[h] - gc_t[h:h + 1, :]), 0.0) for h in heads]
        kb = [kn[h] * beta[h] for h in heads]
        e_col = [jnp.exp(g_col[h]) for h in heads]

        kq = [_dot_nt(jnp.concatenate([kb[h], qn[h]], axis=0), kn[h]) for h in heads]
        a_mat = [jnp.where(strict, kq[h][:c] * decay[h], 0.0) for h in heads]
        qk = [kq[h][c:] * decay[h] for h in heads]
        x_inv = _unit_lower_inverse(a_mat, row, col)
        uw = [_dot(x_inv[h], jnp.concatenate([v[h] * beta[h], kb[h] * e_col[h]], axis=1)) for h in heads]
        s_old = [s_ref[h] for h in heads]
        ws = [_dot(jnp.concatenate([uw[h][:, DN_DV:], qn[h] * e_col[h]], axis=0), s_old[h]) for h in heads]
        v_new = [uw[h][:, :DN_DV] - ws[h][:c] for h in heads]
        o = [ws[h][c:] + _dot(qk[h], v_new[h]) for h in heads]
        for h in heads:
            k_dec = kn[h] * jnp.exp(g_end[h] - g_col[h])
            s_ref[h] = s_old[h] * jnp.exp(g_end[h]) + _dot_tn(k_dec, v_new[h])
        for h in heads:
            oh = o[h] * lax.rsqrt(jnp.mean(o[h] * o[h], axis=-1, keepdims=True) + EPS)
            z_act = z_ref[r0:r0 + c, h * DN_DV:(h + 1) * DN_DV].astype(F32)
            y_ref[r0:r0 + c, h * DN_DV:(h + 1) * DN_DV] = (oh * z_act).astype(y_ref.dtype)

    halo_ref[...] = qkv_ref[rows - halo:rows, :]

    @pl.when(t == pl.num_programs(1) - 1)
    def _():
        s_out_ref[0] = s_ref[...]
        tail_ref[0] = qkv_ref[rows - halo:rows, :]


def _gdn_prompt(qkv, z, ab, conv_w, alog_pad, dtb_pad, dn_norm, n_batch, seq):
    rows = GDN_CHUNK * GDN_CHUNKS_PER_STEP
    assert seq % rows == 0
    nt = seq // rows
    row_spec = lambda w: pl.BlockSpec((rows, w), lambda b, t: (b * nt + t, 0))
    const = lambda shape: pl.BlockSpec(shape, lambda b, t: (0,) * len(shape))
    return pl.pallas_call(
        _gdn_prompt_kernel,
        grid=(n_batch, nt),
        in_specs=[row_spec(CONV_DIM), row_spec(DN_V_W), row_spec(LANES),
                  const((CONV_W, CONV_DIM)), const((1, LANES)), const((1, LANES)), const((1, DN_DV))],
        out_specs=[row_spec(DN_V_W),
                   pl.BlockSpec((1, DN_HEADS, DN_DK, DN_DV), lambda b, t: (b, 0, 0, 0)),
                   pl.BlockSpec((1, SUBLANES, CONV_DIM), lambda b, t: (b, 0, 0))],
        out_shape=[jax.ShapeDtypeStruct((n_batch * seq, DN_V_W), BF16),
                   jax.ShapeDtypeStruct((n_batch, DN_HEADS, DN_DK, DN_DV), F32),
                   jax.ShapeDtypeStruct((n_batch, SUBLANES, CONV_DIM), F32)],
        scratch_shapes=[pltpu.VMEM((SUBLANES, CONV_DIM), F32),
                        pltpu.VMEM((DN_HEADS, DN_DK, DN_DV), F32)],
        compiler_params=_cparams("parallel", "arbitrary"),
        name="gdn_prompt",
    )(qkv, z, ab, conv_w, alog_pad, dtb_pad, dn_norm)


def _kv_head_views(k2, v2, lo):
    k2r = pltpu.roll(k2, SWA_HD, axis=1)
    v2r = pltpu.roll(v2, SWA_HD, axis=1)
    k_lo = (jnp.where(lo, k2, 0.0), jnp.where(lo, k2r, 0.0))
    k_hi = (jnp.where(lo, 0.0, k2r), jnp.where(lo, 0.0, k2))
    v_dup = (jnp.where(lo, v2, v2r), jnp.where(lo, v2r, v2))
    return k_lo, k_hi, v_dup


def _swa_prompt_kernel(sink_ref, q_ref, kvc_ref, kvp_ref, o_ref):
    n = pl.program_id(1)
    blk = SWA_BLOCK
    n_sub = q_ref.shape[0] // blk
    lo = lax.broadcasted_iota(jnp.int32, (blk, LANES), 1) < SWA_HD
    own = (lax.broadcasted_iota(jnp.int32, (blk, blk), 0)
           >= lax.broadcasted_iota(jnp.int32, (blk, blk), 1))
    pairs = SWA_Q_HEADS // 2
    pairs_per_kv = pairs // SWA_KV_HEADS
    views = [_kv_head_views(kvp_ref[:, :LANES], kvp_ref[:, LANES:], lo)]
    for sub in range(n_sub):
        rs = slice(sub * blk, (sub + 1) * blk)
        views.append(_kv_head_views(kvc_ref[rs, :LANES], kvc_ref[rs, LANES:], lo))
    k_cat = {(sub, c): jnp.concatenate([views[1 + sub][0][c], views[1 + sub][1][c],
                                        views[sub][0][c], views[sub][1][c]], axis=0).astype(BF16)
             for sub in range(n_sub) for c in range(SWA_KV_HEADS)}
    v_cat = {(sub, c): jnp.concatenate([views[1 + sub][2][c], views[sub][2][c]], axis=0).astype(BF16)
             for sub in range(n_sub) for c in range(SWA_KV_HEADS)}
    groups = [(sub, c) for sub in range(n_sub) for c in range(SWA_KV_HEADS)]

    def score_stage(sub, c):
        return [_dot_nt(q_ref[sub * blk:(sub + 1) * blk, j * LANES:(j + 1) * LANES]
                        * (SWA_HD ** -0.5),
                        k_cat[sub, c])
                for j in range(c * pairs_per_kv, (c + 1) * pairs_per_kv)]

    def finish(sub, c, scores):
        heads = [(jj, half) for jj in range(pairs_per_kv) for half in range(2)]
        p, den = [], []
        for jj, half in heads:
            s_own = scores[jj][:, half * blk:(half + 1) * blk]
            s_prev = scores[jj][:, (2 + half) * blk:(3 + half) * blk]
            if sub == 0:
                s_prev = jnp.where(n > 0, s_prev, MASK_VALUE)
            s = jnp.where(own, s_own, s_prev)
            sink = sink_ref[2 * (c * pairs_per_kv + jj) + half]
            m = jnp.maximum(jnp.max(s, axis=-1, keepdims=True), sink)
            e = jnp.exp(s - m)
            p.append(jnp.concatenate([jnp.where(own, e, 0.0), jnp.where(own, 0.0, e)], axis=1))
            den.append(jnp.sum(e, axis=-1, keepdims=True) + jnp.exp(sink - m))
        out = [_dot(p[i], v_cat[sub, c]) / den[i] for i in range(len(heads))]
        for jj in range(pairs_per_kv):
            j = c * pairs_per_kv + jj
            o_ref[sub * blk:(sub + 1) * blk, j * LANES:(j + 1) * LANES] = (
                jnp.where(lo, out[2 * jj], out[2 * jj + 1]).astype(o_ref.dtype))

    pending = score_stage(*groups[0])
    for g, (sub, c) in enumerate(groups):
        nxt = score_stage(*groups[g + 1]) if g + 1 < len(groups) else None
        finish(sub, c, pending)
        pending = nxt


def _swa_prompt(q, kv, sinks, n_batch, seq):
    blk = SWA_BLOCK
    n_sub = SWA_BLOCKS_PER_STEP
    assert seq % (blk * n_sub) == 0
    ns = seq // (blk * n_sub)
    step_rows = lambda w: pl.BlockSpec((blk * n_sub, w), lambda b, n: (b * ns + n, 0))
    return pl.pallas_call(
        _swa_prompt_kernel,
        grid=(n_batch, ns),
        in_specs=[pl.BlockSpec(memory_space=pltpu.SMEM),
                  step_rows(SWA_Q_W), step_rows(2 * SWA_KV_W),
                  pl.BlockSpec((blk, 2 * SWA_KV_W),
                               lambda b, n: ((b * ns + n) * n_sub - jnp.minimum(n, 1), 0))],
        out_specs=step_rows(SWA_Q_W),
        out_shape=jax.ShapeDtypeStruct((n_batch * seq, SWA_Q_W), BF16),
        compiler_params=_cparams("parallel", "arbitrary"),
        name="swa_prompt",
    )(sinks, q, kv, kv)


def _mix(x, ydn, ysw, gates, g2, wbd_ref, wbs_ref, wout_ref):
    a = _dot(ydn, wbd_ref[...])
    b = _dot(ysw, wbs_ref[...])
    merged = (jax.nn.sigmoid(gates[:, :D_MODEL].astype(F32)) * a
              + jax.nn.sigmoid(gates[:, D_MODEL:].astype(F32)) * b)
    return x + g2 * _dot(merged, wout_ref[...])


def _final_norm(y, fw):
    return y * lax.rsqrt(jnp.mean(y * y, axis=-1, keepdims=True) + EPS) * fw


def _mix_ffn_kernel(x_ref, ydn_ref, ysw_ref, gates_ref, g2_ref, sh_ref, sc_ref, g3_ref, nw_ref,
                    wbd_ref, wbs_ref, wout_ref, wg_ref, wu_ref, w2_ref, fw_ref, o_ref, *, final):
    x = _mix(x_ref[...], ydn_ref[...], ysw_ref[...], gates_ref[...], g2_ref[0], wbd_ref, wbs_ref, wout_ref)
    y = _ffn_half_step(x, nw_ref[...], sc_ref[0], sh_ref[0], g3_ref[0], wg_ref, wu_ref, w2_ref)
    o_ref[...] = _final_norm(y, fw_ref[...]) if final else y


def _mix_ffn(x, y_dn, y_sw, gates, mod, rows, w, ffn_w, final_w, final, name):
    weights = (w['w_br_dn'], w['w_br_swa'], w['w_out']) + tuple(ffn_w)
    return pl.pallas_call(
        functools.partial(_mix_ffn_kernel, final=final),
        grid=rows.grid,
        in_specs=[rows.row_spec(D_MODEL), rows.row_spec(DN_V_W), rows.row_spec(SWA_Q_W),
                  rows.row_spec(2 * D_MODEL), rows.mod_spec(5), rows.mod_spec(6), rows.mod_spec(7),
                  rows.mod_spec(8), _resident((1, D_MODEL))]
                 + [_resident(a.shape) for a in weights] + [_resident((1, D_MODEL))],
        out_specs=rows.row_spec(D_MODEL),
        out_shape=jax.ShapeDtypeStruct((rows.n_rows, D_MODEL), F32),
        compiler_params=_cparams("parallel"),
        name=name,
    )(x, y_dn, y_sw, gates, mod, mod, mod, mod, w['norm_ffn2'], *weights, final_w)


def _mix_ffn_stream_kernel(x_ref, ydn_ref, ysw_ref, gates_ref, g2_ref, sh_ref, sc_ref, g3_ref, nw_ref,
                           wbd_ref, wbs_ref, wout_ref, fw_ref, wg_ref, wu_ref, w2_ref,
                           o_ref, wgb_ref, wub_ref, w2b_ref, x_scr, h_scr, acc_scr, *, final):
    j = pl.program_id(0)

    @pl.when(j == 0)
    def _():
        x = _mix(x_ref[...], ydn_ref[...], ysw_ref[...], gates_ref[...], g2_ref[0],
                 wbd_ref, wbs_ref, wout_ref)
        x_scr[...] = x
        h_scr[...] = _rms_mod(x, nw_ref[...], sc_ref[0], sh_ref[0]).astype(BF16)
        acc_scr[...] = jnp.zeros(acc_scr.shape, F32)

    _ffn_stream_step(h_scr, acc_scr, wg_ref, wu_ref, w2_ref, wgb_ref, wub_ref, w2b_ref)

    @pl.when(j == pl.num_programs(0) - 1)
    def _():
        y = x_scr[...] + (HALF_STEP * g3_ref[0]) * acc_scr[...]
        o_ref[...] = _final_norm(y, fw_ref[...]) if final else y


def _mix_ffn_stream(x, y_dn, y_sw, gates, mod, rows, w, w13, w2, final_w, final, name):
    assert rows.grid == (1,)
    const = lambda shape: pl.BlockSpec(shape, lambda j: (0,) * len(shape))
    mod_spec = lambda p: pl.BlockSpec(rows.mod_block, lambda j: (0, 0, p))
    mix_w = (w['w_br_dn'], w['w_br_swa'], w['w_out'])
    w_ins, w_outs, w_shapes = _ffn_stream_specs(w13, w2)
    tm = rows.tm
    return pl.pallas_call(
        functools.partial(_mix_ffn_stream_kernel, final=final),
        grid=(D_FF // FF_CHUNK,),
        in_specs=[const((tm, D_MODEL)), const((tm, DN_V_W)), const((tm, SWA_Q_W)), const((tm, 2 * D_MODEL)),
                  mod_spec(5), mod_spec(6), mod_spec(7), mod_spec(8), const((1, D_MODEL))]
                 + [const(a.shape) for a in mix_w] + [const((1, D_MODEL))] + w_ins,
        out_specs=[const((tm, D_MODEL))] + w_outs,
        out_shape=[jax.ShapeDtypeStruct((rows.n_rows, D_MODEL), F32)] + w_shapes,
        scratch_shapes=[pltpu.VMEM((tm, D_MODEL), F32), pltpu.VMEM((tm, D_MODEL), BF16),
                        pltpu.VMEM((tm, D_MODEL), F32)],
        compiler_params=_cparams("arbitrary"),
        name=name,
    )(x, y_dn, y_sw, gates, mod, mod, mod, mod, w['norm_ffn2'], *mix_w, final_w, w13, w13, w2)


def _gdn_step_prep_kernel(qkv_ref, cs_ref, ab_ref, cw_ref, alog_ref, dtb_ref,
                          cs_out_ref, qt_ref, kt_ref, v_ref, dec_ref, beta_ref):
    nb = qkv_ref.shape[0]
    for j in range(CONV_W - 2):
        cs_out_ref[j] = cs_ref[j + 1]
    cs_out_ref[CONV_W - 2] = qkv_ref[...]
    ab = ab_ref[...]
    dec = jnp.exp(-jnp.exp(alog_ref[...]) * jax.nn.softplus(ab + dtb_ref[...]))
    beta = jax.nn.sigmoid(ab)

    def conv_silu(c0):
        cols = slice(c0, c0 + LANES)
        y = qkv_ref[:, cols] * cw_ref[CONV_W - 1:CONV_W, cols]
        for j in range(CONV_W - 1):
            y = y + cs_ref[j, :, cols] * cw_ref[j:j + 1, cols]
        return _silu(y)

    for h in range(DN_HEADS):
        q = conv_silu(h * DN_DK)
        k = conv_silu(DN_QK_W + h * DN_DK)
        qn = q * lax.rsqrt(jnp.sum(q * q, axis=-1, keepdims=True) + EPS) * (DN_DK ** -0.5)
        kn = k * lax.rsqrt(jnp.sum(k * k, axis=-1, keepdims=True) + EPS)
        qt_ref[h] = qn.T
        kt_ref[h] = kn.T
        sl = slice(h * DN_DV, (h + 1) * DN_DV)
        v_ref[:, sl] = conv_silu(2 * DN_QK_W + h * DN_DV)
        dec_ref[:, sl] = jnp.broadcast_to(dec[:, h:h + 1], (nb, DN_DV))
        beta_ref[:, sl] = jnp.broadcast_to(beta[:, DN_HEADS + h:DN_HEADS + h + 1], (nb, DN_DV))


def _gdn_step_prep(qkv, conv_state, ab, conv_w, alog_pad, dtb_pad):
    nb = qkv.shape[0]
    full = _resident
    return pl.pallas_call(
        _gdn_step_prep_kernel,
        grid=(1,),
        in_specs=[full((nb, CONV_DIM)), full(conv_state.shape), full((nb, LANES)),
                  full((CONV_W, CONV_DIM)), full((1, LANES)), full((1, LANES))],
        out_specs=[full(conv_state.shape), full((DN_HEADS, DN_DK, nb)), full((DN_HEADS, DN_DK, nb)),
                   full((nb, DN_V_W)), full((nb, DN_V_W)), full((nb, DN_V_W))],
        out_shape=[jax.ShapeDtypeStruct(conv_state.shape, F32),
                   jax.ShapeDtypeStruct((DN_HEADS, DN_DK, nb), F32),
                   jax.ShapeDtypeStruct((DN_HEADS, DN_DK, nb), F32),
                   jax.ShapeDtypeStruct((nb, DN_V_W), F32),
                   jax.ShapeDtypeStruct((nb, DN_V_W), F32),
                   jax.ShapeDtypeStruct((nb, DN_V_W), F32)],
        compiler_params=_cparams("arbitrary"),
        name="gdn_step_prep",
    )(qkv, conv_state, ab, conv_w, alog_pad, dtb_pad)


def _gdn_step_kernel(s_ref, qt_ref, kt_ref, v_ref, dec_ref, beta_ref, z_ref, dnw_ref,
                     s_out_ref, y_ref, o_scr):
    nb = s_ref.shape[0]
    qt = qt_ref[0]
    kt = kt_ref[0]
    for b in range(nb):
        k_col = kt[:, b:b + 1]
        q_col = qt[:, b:b + 1]
        s1 = s_ref[b, 0] * dec_ref[b:b + 1, :]
        kv = jnp.sum(s1 * k_col, axis=0, keepdims=True)
        delta = (v_ref[b:b + 1, :] - kv) * beta_ref[b:b + 1, :]
        s2 = s1 + k_col * delta
        s_out_ref[b, 0] = s2
        o_scr[b:b + 1, :] = jnp.sum(s2 * q_col, axis=0, keepdims=True)
    o = o_scr[...]
    o = o * lax.rsqrt(jnp.mean(o * o, axis=-1, keepdims=True) + EPS)
    y_ref[...] = (o * z_ref[...].astype(F32)).astype(y_ref.dtype)


def _gdn_step(state, qt, kt, v, dec, beta, z, dn_norm):
    nb = state.shape[0]
    head_cols = pl.BlockSpec((nb, DN_DV), lambda h: (0, h))
    head_t = pl.BlockSpec((1, DN_DK, nb), lambda h: (h, 0, 0))
    s_spec = pl.BlockSpec((nb, 1, DN_DK, DN_DV), lambda h: (0, h, 0, 0))
    return pl.pallas_call(
        _gdn_step_kernel,
        grid=(DN_HEADS,),
        in_specs=[s_spec, head_t, head_t, head_cols, head_cols, head_cols, head_cols,
                  pl.BlockSpec((1, DN_DV), lambda h: (0, 0))],
        out_specs=[s_spec, head_cols],
        out_shape=[jax.ShapeDtypeStruct(state.shape, F32),
                   jax.ShapeDtypeStruct((nb, DN_V_W), BF16)],
        scratch_shapes=[pltpu.VMEM((nb, DN_DV), F32)],
        compiler_params=_cparams("parallel"),
        name="gdn_step",
    )(state, qt, kt, v, dec, beta, z, dn_norm)


SWA_STEP_BATCH = 16


def _swa_step_kernel(q_ref, kvn_ref, ck_ref, cv_ref, slo_ref, shi_ref, o_ref, ck_out_ref, cv_out_ref):
    length = ck_ref.shape[1]
    last = lax.broadcasted_iota(jnp.int32, (length, LANES), 0) == length - 1
    lo_k = lax.broadcasted_iota(jnp.int32, (length, LANES), 1) < SWA_HD
    pairs = SWA_Q_HEADS // 2
    first_kv = lax.broadcasted_iota(jnp.int32, (pairs, LANES), 0) < pairs // SWA_KV_HEADS
    lo_o = lax.broadcasted_iota(jnp.int32, (pairs, LANES), 1) < SWA_HD
    samples = range(q_ref.shape[0])
    k2 = [jnp.where(last, kvn_ref[b:b + 1, 0:LANES], pltpu.roll(ck_ref[b], length - 1, axis=0))
          for b in samples]
    v2 = [jnp.where(last, kvn_ref[b:b + 1, LANES:2 * LANES], pltpu.roll(cv_ref[b], length - 1, axis=0))
          for b in samples]
    for b in samples:
        ck_out_ref[b] = k2[b]
        cv_out_ref[b] = v2[b]
    views = [_kv_head_views(k2[b], v2[b], lo_k) for b in samples]
    k_cat = [jnp.concatenate([views[b][0][0], views[b][0][1], views[b][1][0], views[b][1][1]], axis=0)
             for b in samples]
    v_cat = [jnp.concatenate([views[b][2][0], views[b][2][1]], axis=1) for b in samples]
    scores = [_dot_nt(q_ref[b] * (SWA_HD ** -0.5), k_cat[b]) for b in samples]
    p, den = [], []
    for b in samples:
        halves_p, halves_den = [], []
        for half, sink_ref in enumerate((slo_ref, shi_ref)):
            s = jnp.where(first_kv, scores[b][:, 2 * half * length:(2 * half + 1) * length],
                          scores[b][:, (2 * half + 1) * length:(2 * half + 2) * length])
            sink = sink_ref[...]
            m = jnp.maximum(jnp.max(s, axis=-1, keepdims=True), sink)
            e = jnp.exp(s - m)
            halves_p.append(e)
            halves_den.append(jnp.sum(e, axis=-1, keepdims=True) + jnp.exp(sink - m))
        p.append(jnp.concatenate(halves_p, axis=0))
        den.append(halves_den)
    pv = [_dot(p[b], v_cat[b]) for b in samples]
    for b in samples:
        halves = [jnp.where(first_kv, pv[b][half * pairs:(half + 1) * pairs, :LANES],
                            pv[b][half * pairs:(half + 1) * pairs, LANES:]) / den[b][half]
                  for half in range(2)]
        o_ref[b] = jnp.where(lo_o, halves[0], halves[1]).astype(o_ref.dtype)


def _swa_step(q3, kv_new, cache_k, cache_v, sink_lo, sink_hi):
    nb, length, _ = cache_k.shape
    tb = SWA_STEP_BATCH
    pairs = SWA_Q_HEADS // 2
    q_spec = pl.BlockSpec((tb, pairs, LANES), lambda i: (i, 0, 0))
    c_spec = pl.BlockSpec((tb, length, LANES), lambda i: (i, 0, 0))
    sink_spec = pl.BlockSpec((pairs, LANES), lambda i: (0, 0))
    return pl.pallas_call(
        _swa_step_kernel,
        grid=(nb // tb,),
        in_specs=[q_spec, pl.BlockSpec((tb, 2 * SWA_KV_W), lambda i: (i, 0)), c_spec, c_spec,
                  sink_spec, sink_spec],
        out_specs=[q_spec, c_spec, c_spec],
        out_shape=[jax.ShapeDtypeStruct(q3.shape, BF16),
                   jax.ShapeDtypeStruct(cache_k.shape, F32),
                   jax.ShapeDtypeStruct(cache_v.shape, F32)],
        compiler_params=_cparams("parallel"),
        name="swa_step",
    )(q3, kv_new, cache_k, cache_v, sink_lo, sink_hi)


def _prep_layer_weights(lp):
    w_t, w_ab = _cast_w_in(jnp.swapaxes(lp['w_in'], 0, 1))
    pad_heads = lambda a: jnp.pad(a.astype(F32), (0, LANES - DN_HEADS)).reshape(1, LANES)
    sinks = lp['sinks'].astype(F32)
    pairs = SWA_Q_HEADS // 2
    return dict(
        w13_ffn1=lp['w13_ffn1'], w2_ffn1=lp['w2_ffn1'], w13_ffn2=lp['w13_ffn2'], w2_ffn2=lp['w2_ffn2'],
        w_t=w_t, w_ab=w_ab,
        w_br_dn=lp['w_br_dn'].astype(BF16), w_br_swa=lp['w_br_swa'].astype(BF16),
        w_out=lp['w_out'].astype(BF16),
        norm_ffn1=lp['norm_ffn1'].reshape(1, D_MODEL), norm_mix=lp['norm_mix'].reshape(1, D_MODEL),
        norm_ffn2=lp['norm_ffn2'].reshape(1, D_MODEL),
        conv_w=lp['conv_w'], alog_pad=pad_heads(lp['a_log']), dtb_pad=pad_heads(lp['dt_bias']),
        dn_norm=lp['dn_norm'].reshape(1, DN_DV), sinks=sinks,
        dnw_heads=jnp.tile(lp['dn_norm'].astype(F32), DN_HEADS).reshape(1, DN_V_W),
        sink_lo=jnp.broadcast_to(sinks.reshape(pairs, 2)[:, 0:1], (pairs, LANES)),
        sink_hi=jnp.broadcast_to(sinks.reshape(pairs, 2)[:, 1:2], (pairs, LANES)),
    )


def _layer(x, mod, rows, w, final_w, final, past, ffn_bf16, n_batch, seq, tag):
    if past is None:
        x = _ffn(x, mod, _Rows(n_batch, seq, FFN1_ROW_TILE), (0, 1, 2), w['norm_ffn1'], ffn_bf16[0],
                 "ffn1_" + tag)
    else:
        x, *ffn1_b = _ffn_stream(x, mod, rows, (0, 1, 2), w['norm_ffn1'], w['w13_ffn1'], w['w2_ffn1'],
                                 "ffn1_" + tag)
    if past is None:
        in_rows = _Rows(n_batch, seq, INPROJ_ROW_TILE)
        qkv, z, q_sw, kv, gates, ab = _inproj(
            x, mod, in_rows, (3, 4), w['norm_mix'], w['w_t'], w['w_ab'], w['dnw_heads'],
            "inproj_" + tag)
        y_dn, s_new, tail = _gdn_prompt(qkv, z, ab, w['conv_w'], w['alog_pad'], w['dtb_pad'],
                                        w['dn_norm'], n_batch, seq)
        y_sw = _swa_prompt(q_sw, kv, w['sinks'], n_batch, seq)
        conv_new = tail[:, SUBLANES - (CONV_W - 1):]
        keep = min(WINDOW, seq)
        kv3 = kv.reshape(n_batch, seq, 2 * SWA_KV_W)[:, seq - keep:]
        k_buf = kv3[:, :, :SWA_KV_W].reshape(n_batch, keep, SWA_KV_HEADS, SWA_HD)
        v_buf = kv3[:, :, SWA_KV_W:].reshape(n_batch, keep, SWA_KV_HEADS, SWA_HD)
    else:
        s0, conv_buf, k_old, v_old = past
        length = k_old.shape[1]
        qkv, z, q_sw, kv, gates, ab = _inproj(
            x, mod, rows, (3, 4), w['norm_mix'], w['w_t'], w['w_ab'], w['dnw_heads'],
            "inproj_" + tag)
        conv_new, qt, kt, v, dec, beta = _gdn_step_prep(qkv, jnp.swapaxes(conv_buf, 0, 1), ab,
                                                        w['conv_w'], w['alog_pad'], w['dtb_pad'])
        conv_new = jnp.swapaxes(conv_new, 0, 1)
        s_new, y_dn = _gdn_step(s0, qt, kt, v, dec, beta, z, w['dn_norm'])
        o3, k_buf, v_buf = _swa_step(q_sw.reshape(n_batch, SWA_Q_HEADS // 2, LANES), kv,
                                     k_old.reshape(n_batch, length, SWA_KV_W),
                                     v_old.reshape(n_batch, length, SWA_KV_W),
                                     w['sink_lo'], w['sink_hi'])
        y_sw = o3.reshape(n_batch, SWA_Q_W)
        k_buf = k_buf.reshape(n_batch, length, SWA_KV_HEADS, SWA_HD)
        v_buf = v_buf.reshape(n_batch, length, SWA_KV_HEADS, SWA_HD)
    if past is None:
        x = _mix_ffn(x, y_dn, y_sw, gates, mod, rows, w, ffn_bf16[1], final_w, final, "mix_ffn2_" + tag)
        return x, (s_new, conv_new, k_buf, v_buf)
    x, *ffn2_b = _mix_ffn_stream(x, y_dn, y_sw, gates, mod, rows, w, w['w13_ffn2'], w['w2_ffn2'],
                                 final_w, final, "mix_ffn2_" + tag)
    return x, (s_new, conv_new, k_buf, v_buf), (ffn1_b, ffn2_b)


def kernel(x_prompt, x_sample, state_dn, state_conv, cache_swa_k, cache_swa_v, c_prompt, c_sample,
           w_ada, b_ada, norm_ffn1, w13_ffn1, w2_ffn1, norm_mix, w_in, conv_w, a_log, dt_bias,
           dn_norm, sinks, w_br_dn, w_br_swa, w_out, norm_ffn2, w13_ffn2, w2_ffn2, final_norm):
    n_p, seq_p, d = x_prompt.shape
    n_s, seq_s, _ = x_sample.shape
    depth = w_ada.shape[0]
    assert d == D_MODEL and seq_s == 1 and seq_p % ROW_TILE == 0 and seq_p % GDN_CHUNK == 0
    assert w_in.shape[2] == sum(IN_SPLITS) and w13_ffn1.shape[2] == 2 * D_FF
    assert cache_swa_k.shape[2] == WINDOW and n_s % SWA_STEP_BATCH == 0 and n_p % SUBLANES == 0
    rows_p = _Rows(n_p, seq_p, ROW_TILE)
    rows_s = _Rows(n_s, seq_s, ROW_TILE)
    c_all = jnp.concatenate([c_prompt, c_sample], axis=0)
    final_w = final_norm.reshape(1, D_MODEL)
    y_p = x_prompt.reshape(n_p * seq_p, d)
    y_s = x_sample.reshape(n_s * seq_s, d)
    st_p, st_s = [], []
    for l in range(depth):
        lp = dict(w_ada=w_ada[l], b_ada=b_ada[l], norm_ffn1=norm_ffn1[l], w13_ffn1=w13_ffn1[l],
                  w2_ffn1=w2_ffn1[l], norm_mix=norm_mix[l], w_in=w_in[l], conv_w=conv_w[l],
                  a_log=a_log[l], dt_bias=dt_bias[l], dn_norm=dn_norm[l], sinks=sinks[l],
                  w_br_dn=w_br_dn[l], w_br_swa=w_br_swa[l], w_out=w_out[l], norm_ffn2=norm_ffn2[l],
                  w13_ffn2=w13_ffn2[l], w2_ffn2=w2_ffn2[l])
        w = _prep_layer_weights(lp)
        mod_p, mod_s = _ada(c_all, lp['w_ada'], lp['b_ada'], n_p)
        final = l == depth - 1
        y_s, ss, ffn_bf16 = _layer(y_s, mod_s.reshape(rows_s.mod_shape), rows_s, w, final_w, final,
                                   (state_dn[l], state_conv[l], cache_swa_k[l], cache_swa_v[l]), None,
                                   n_s, seq_s, "sample")
        y_p, sp = _layer(y_p, mod_p.reshape(rows_p.mod_shape), rows_p, w, final_w, final, None, ffn_bf16,
                         n_p, seq_p, "prompt")
        st_p.append(sp)
        st_s.append(ss)
    stack = lambda sts, i: sts[0][i][None] if depth == 1 else jnp.stack([s[i] for s in sts])
    return (y_p.reshape(n_p, seq_p, d), y_s.reshape(n_s, seq_s, d),
            stack(st_p, 0), stack(st_s, 0), stack(st_p, 1), stack(st_s, 1),
            stack(st_p, 2), stack(st_s, 2), stack(st_p, 3), stack(st_s, 3))
```
